```python
import math
import jax, jax.numpy as jnp
from jax import lax
import numpy as np

D_MODEL = 1024
BATCH = 8
SEQ = 4096
DEPTH = 1

D_MIX = D_MODEL
A_HEAD_DIM = 64
A_HEADS = (D_MIX // 2) // A_HEAD_DIM
A_WIDTH = A_HEADS * A_HEAD_DIM
DILATED_PATTERNS = ((128, 1), (512, 4), (2048, 16))
BLOCK = 128
B_HEADS = 8
QK_NOPE_DIM = 64
QK_ROPE_DIM = 32
V_HEAD_DIM = (D_MIX - A_WIDTH) // B_HEADS
B_WIDTH = B_HEADS * V_HEAD_DIM
Q_LORA_RANK = 256
KV_LORA_RANK = 128
ROPE_THETA = 10000.0
REL_BUCKETS = 32
REL_MAX_DISTANCE = 2048
EPS = 1e-6

IN_SPLITS = (A_WIDTH, A_WIDTH, A_WIDTH, A_WIDTH,
             Q_LORA_RANK, KV_LORA_RANK, QK_ROPE_DIM, B_WIDTH)
IN_COLS = sum(IN_SPLITS)

kernel_name = "hybrid_dilated_swa_mla_gated"


def rmsnorm(t, gain):
    tf = t.astype(jnp.float32)
    return tf * lax.rsqrt(jnp.mean(tf * tf, axis=-1, keepdims=True) + EPS) * gain.astype(jnp.float32)


def rope(t, cos, sin):
    t1, t2 = jnp.split(t, 2, axis=-1)
    return jnp.concatenate([t1 * cos - t2 * sin, t1 * sin + t2 * cos], axis=-1)


def t5_bucket(dist):
    max_exact = REL_BUCKETS // 2
    d = jnp.maximum(dist.astype(jnp.float32), 1.0)
    large = max_exact + (jnp.log(d / max_exact) / math.log(REL_MAX_DISTANCE / max_exact)
                         * (REL_BUCKETS - max_exact)).astype(jnp.int32)
    large = jnp.minimum(large, REL_BUCKETS - 1)
    return jnp.where(dist < max_exact, dist, large)


def dilated_pattern(q, k, v, rel_bias, window, dilation):
    B, S, H, D = q.shape
    steps = window // dilation
    span = dilation * BLOCK
    Sp = -(-S // span) * span
    nb = Sp // span
    pad = ((0, 0), (0, Sp - S), (0, 0), (0, 0))

    def blocks(t):
        return jnp.pad(t, pad).reshape(B, nb, BLOCK, dilation, H, D)

    def with_prev(t):
        prev = jnp.pad(t[:, :-1], ((0, 0), (1, 0), (0, 0), (0, 0), (0, 0), (0, 0)))
        return jnp.concatenate([prev, t], axis=2)

    qb = blocks(q)
    kw = with_prev(blocks(k))
    vw = with_prev(blocks(v))

    qi = jnp.arange(BLOCK)[:, None]
    ki = jnp.arange(2 * BLOCK)[None, :]
    j = qi + BLOCK - ki
    bias = rel_bias[t5_bucket(jnp.maximum(j, 0) * dilation)]
    bias = jnp.transpose(bias, (2, 0, 1)).astype(jnp.float32)
    valid = ((j >= 0) & (j <= steps))[None] & \
        ((jnp.arange(nb)[:, None, None] > 0) | (ki >= BLOCK)[None])

    s = jnp.einsum('bnqrhd,bnkrhd->bnrhqk', qb, kw) * (A_HEAD_DIM ** -0.5) + bias
    s = jnp.where(valid[None, :, None, None], s, -jnp.inf)
    m = jnp.max(s, axis=-1, keepdims=True)
    p = jnp.exp(s - m)
    den = jnp.sum(p, axis=-1)
    den_q = jnp.transpose(den, (0, 1, 4, 2, 3))
    o = jnp.einsum('bnrhqk,bnkrhd->bnqrhd', p, vw) / den_q[..., None]
    o = o.reshape(B, Sp, H, D)[:, :S]
    m = jnp.transpose(m[..., 0], (0, 1, 4, 2, 3)).reshape(B, Sp, H)[:, :S]
    den = den_q.reshape(B, Sp, H)[:, :S]
    return o, m, den


def dilated_window_attention(q, k, v, rel_bias):
    outs, log_dens = [], []
    for window, dilation in DILATED_PATTERNS:
        o, m, den = dilated_pattern(q, k, v, rel_bias, window, dilation)
        outs.append(o)
        log_dens.append(m + jnp.log(den))
    alpha = jax.nn.softmax(jnp.stack(log_dens, axis=0), axis=0)
    return jnp.sum(alpha[..., None] * jnp.stack(outs, axis=0), axis=0)


def latent_attention(c_q, c_kv, k_rope_in, positions, q_c_gain, w_uq, kv_c_gain, w_ukv,
                     qn_gain, qr_gain, kn_gain, kr_gain):
    B, S, _ = c_q.shape
    q = jnp.matmul(rmsnorm(c_q, q_c_gain), w_uq.astype(jnp.float32))
    q = q.reshape(B, S, B_HEADS, QK_NOPE_DIM + QK_ROPE_DIM)
    kv = jnp.matmul(rmsnorm(c_kv, kv_c_gain), w_ukv.astype(jnp.float32))
    kv = kv.reshape(B, S, B_HEADS, QK_NOPE_DIM + V_HEAD_DIM)
    q_nope = rmsnorm(q[..., :QK_NOPE_DIM], qn_gain)
    q_rope = rmsnorm(q[..., QK_NOPE_DIM:], qr_gain)
    k_nope = rmsnorm(kv[..., :QK_NOPE_DIM], kn_gain)
    v = kv[..., QK_NOPE_DIM:]
    k_rope = rmsnorm(k_rope_in, kr_gain)

    inv_freq = ROPE_THETA ** (-jnp.arange(0, QK_ROPE_DIM, 2, dtype=jnp.float32) / QK_ROPE_DIM)
    ang = positions.astype(jnp.float32)[..., None] * inv_freq
    cos, sin = jnp.cos(ang), jnp.sin(ang)
    q_rope = rope(q_rope, cos[:, :, None], sin[:, :, None])
    k_rope = rope(k_rope, cos, sin)

    scale = (QK_NOPE_DIM + QK_ROPE_DIM) ** -0.5
    nb = S // BLOCK
    qn_b = q_nope.reshape(B, nb, BLOCK, B_HEADS, QK_NOPE_DIM).transpose(1, 0, 2, 3, 4)
    qr_b = q_rope.reshape(B, nb, BLOCK, B_HEADS, QK_ROPE_DIM).transpose(1, 0, 2, 3, 4)
    kpos = jnp.arange(S)

    def one_block(args):
        qn, qr, i = args
        s = (jnp.einsum('bqhd,bkhd->bhqk', qn, k_nope)
             + jnp.einsum('bqhd,bkd->bhqk', qr, k_rope)) * scale
        qpos = i * BLOCK + jnp.arange(BLOCK)
        s = jnp.where((qpos[:, None] >= kpos[None, :])[None, None], s, -jnp.inf)
        p = jax.nn.softmax(s, axis=-1)
        return jnp.einsum('bhqk,bkhd->bqhd', p, v)

    o = lax.map(one_block, (qn_b, qr_b, jnp.arange(nb)))
    return o.transpose(1, 0, 2, 3, 4).reshape(B, S, B_WIDTH)


def hybrid_layer(x, positions, rel_bias, norm_gain, w_in, a_q_gain, a_k_gain, q_c_gain, w_uq,
                 kv_c_gain, w_ukv, qn_gain, qr_gain, kn_gain, kr_gain, w_out):
    B, S, _ = x.shape
    h = rmsnorm(x, norm_gain)
    proj = jnp.matmul(h, w_in.astype(jnp.float32))
    idx = np.cumsum(IN_SPLITS)[:-1].tolist()
    q_a, k_a, v_a, z_a, c_q, c_kv, k_rope, z_b = jnp.split(proj, idx, axis=-1)

    q_a = rmsnorm(q_a.reshape(B, S, A_HEADS, A_HEAD_DIM), a_q_gain)
    k_a = rmsnorm(k_a.reshape(B, S, A_HEADS, A_HEAD_DIM), a_k_gain)
    v_a = v_a.reshape(B, S, A_HEADS, A_HEAD_DIM)
    o_a = dilated_window_attention(q_a, k_a, v_a, rel_bias).reshape(B, S, A_WIDTH)

    o_b = latent_attention(c_q, c_kv, k_rope, positions, q_c_gain, w_uq, kv_c_gain, w_ukv,
                           qn_gain, qr_gain, kn_gain, kr_gain)

    mixed = jnp.concatenate([o_a * jax.nn.silu(z_a), o_b * jax.nn.silu(z_b)], axis=-1)
    out = jnp.matmul(mixed, w_out.astype(jnp.float32))
    return (x.astype(jnp.float32) + out).astype(x.dtype)


def setup_inputs(seed: int = 0) -> dict:
    key = jax.random.key(seed)
    ks = jax.random.split(key, 18)
    f32 = jnp.float32

    def nrm(k, shape, scale):
        return jax.random.normal(k, shape, f32) * scale

    def gain(k, shape):
        return 1.0 + 0.05 * jax.random.normal(k, shape, f32)

    x = jax.random.normal(ks[0], (BATCH, SEQ, D_MODEL), f32)
    offsets = jax.random.randint(ks[1], (BATCH, 1), 0, 1024, dtype=jnp.int32)
    positions = (jnp.arange(SEQ, dtype=jnp.int32)[None, :] + offsets).astype(jnp.int32)
    return {
        "x": x,
        "positions": positions,
        "rel_bias": nrm(ks[2], (REL_BUCKETS, A_HEADS), 0.1),
        "norm_gain": gain(ks[3], (DEPTH, D_MODEL)),
        "w_in": nrm(ks[4], (DEPTH, D_MODEL, IN_COLS), D_MODEL ** -0.5),
        "a_q_gain": gain(ks[5], (DEPTH, A_HEAD_DIM)),
        "a_k_gain": gain(ks[6], (DEPTH, A_HEAD_DIM)),
        "q_c_gain": gain(ks[7], (DEPTH, Q_LORA_RANK)),
        "w_uq": nrm(ks[8], (DEPTH, Q_LORA_RANK, B_HEADS * (QK_NOPE_DIM + QK_ROPE_DIM)), Q_LORA_RANK ** -0.5),
        "kv_c_gain": gain(ks[9], (DEPTH, KV_LORA_RANK)),
        "w_ukv": nrm(ks[10], (DEPTH, KV_LORA_RANK, B_HEADS * (QK_NOPE_DIM + V_HEAD_DIM)), KV_LORA_RANK ** -0.5),
        "qn_gain": gain(ks[11], (DEPTH, QK_NOPE_DIM)),
        "qr_gain": gain(ks[12], (DEPTH, QK_ROPE_DIM)),
        "kn_gain": gain(ks[13], (DEPTH, QK_NOPE_DIM)),
        "kr_gain": gain(ks[14], (DEPTH, QK_ROPE_DIM)),
        "w_out": nrm(ks[15], (DEPTH, D_MIX, D_MODEL), D_MIX ** -0.5),
    }


def reference(x, positions, rel_bias, norm_gain, w_in, a_q_gain, a_k_gain, q_c_gain, w_uq,
              kv_c_gain, w_ukv, qn_gain, qr_gain, kn_gain, kr_gain, w_out):
    for l in range(DEPTH):
        x = hybrid_layer(x, positions, rel_bias, norm_gain[l], w_in[l], a_q_gain[l], a_k_gain[l],
                         q_c_gain[l], w_uq[l], kv_c_gain[l], w_ukv[l], qn_gain[l], qr_gain[l],
                         kn_gain[l], kr_gain[l], w_out[l])
    return x
```

```python
import functools
import math

import jax
import jax.numpy as jnp
import numpy as np
from jax import lax
from jax.experimental import pallas as pl
from jax.experimental.pallas import tpu as pltpu

F32 = jnp.float32
BF16 = jnp.bfloat16

LANES = 128
HEAD_DIM = 64
PAIR = 2 * HEAD_DIM
ROPE_DIM = 32
KV_RANK = 128
Q_RANK = 256
BLOCK = 128
DILATIONS = (1, 4, 16)
MAX_DIL = 16
ROPE_THETA = 10000.0
REL_BUCKETS = 32
REL_MAX_DISTANCE = 2048
EPS = 1e-6
NEG = -1e30

ROW_TILE = 512
MLA_TILE = 256

VMEM_LIMIT = 56 * 1024 * 1024


def _nt_dot(a, b):
    return lax.dot_general(a, b, (((1,), (1,)), ((), ())), preferred_element_type=F32)


def _lo_mask(shape):
    return lax.broadcasted_iota(jnp.int32, shape, len(shape) - 1) < HEAD_DIM


def _pair_rms(t, gain_row):
    lo = _lo_mask(t.shape)
    sq = t * t
    s_lo = jnp.sum(jnp.where(lo, sq, 0.0), axis=1, keepdims=True)
    s_hi = jnp.sum(jnp.where(lo, 0.0, sq), axis=1, keepdims=True)
    r = jnp.where(lo, lax.rsqrt(s_lo * (1.0 / HEAD_DIM) + EPS),
                  lax.rsqrt(s_hi * (1.0 / HEAD_DIM) + EPS))
    return t * r * gain_row


def _silu(z):
    return z * (1.0 / (1.0 + jnp.exp(-z)))


def _proj_kernel(x_ref, pos_ref, ng_ref, w_in_ref, gq_ref, gk_ref, gcq_ref, wuq_ref,
                 gckv_ref, wukv_ref, gqn_ref, gqr_ref, gkn_ref, gkr_ref, freq_ref, sgn_ref,
                 qa_ref, ka_ref, va_ref, ga_ref, qn_ref, qr_ref, kcat_ref, vb_ref, gb_ref,
                 sq_ref, sk_ref, sv_ref, *, off):
    tm = x_ref.shape[1]
    sub = tm // MAX_DIL
    x = x_ref[0]
    h = x * lax.rsqrt(jnp.mean(x * x, axis=-1, keepdims=True) + EPS) * ng_ref[...]
    hb = h.astype(BF16)

    def seg(name, width):
        a = off[name]
        return jnp.dot(hb, w_in_ref[:, a:a + width], preferred_element_type=F32)

    n_pair = sq_ref.shape[0]

    def emit_a(t, gain_ref, scr_ref, out_ref):
        for p in range(n_pair):
            c = t[:, p * PAIR:(p + 1) * PAIR]
            if gain_ref is not None:
                c = _pair_rms(c, gain_ref[...])
            scr_ref[p] = c
        for p in range(n_pair):
            for r in range(MAX_DIL):
                out_ref[0, p, r] = scr_ref[p, pl.ds(r, sub, stride=MAX_DIL), :]

    emit_a(seg("qa", n_pair * PAIR), gq_ref, sq_ref, qa_ref)
    emit_a(seg("ka", n_pair * PAIR), gk_ref, sk_ref, ka_ref)
    emit_a(seg("va", n_pair * PAIR), None, sv_ref, va_ref)
    ga_ref[0] = _silu(seg("za", n_pair * PAIR)).astype(BF16)
    gb_ref[0] = _silu(seg("zb", n_pair * PAIR)).astype(BF16)

    ang = pos_ref[0].astype(F32) * freq_ref[...]
    cos_t = jnp.cos(ang)
    sin_t = jnp.sin(ang) * sgn_ref[...]
    lane = lax.broadcasted_iota(jnp.int32, (tm, LANES), 1)
    first_half = lane < (ROPE_DIM // 2)

    def rope(t):
        partner = jnp.where(first_half, pltpu.roll(t, LANES - ROPE_DIM // 2, 1),
                            pltpu.roll(t, ROPE_DIM // 2, 1))
        return t * cos_t + partner * sin_t

    def rope_rms(t, gain_row):
        ss = jnp.sum(t * t, axis=1, keepdims=True) * (1.0 / ROPE_DIM)
        return rope(t * lax.rsqrt(ss + EPS) * gain_row)

    cq = seg("cq", Q_RANK)
    cq = cq * lax.rsqrt(jnp.mean(cq * cq, axis=-1, keepdims=True) + EPS) * gcq_ref[...]
    q = jnp.dot(cq.astype(BF16), wuq_ref[...], preferred_element_type=F32)
    nope_w = n_pair * PAIR
    for p in range(n_pair):
        qn_ref[0, p] = _pair_rms(q[:, p * PAIR:(p + 1) * PAIR], gqn_ref[...]).astype(BF16)
    for hd in range(2 * n_pair):
        t = q[:, nope_w + hd * LANES:nope_w + (hd + 1) * LANES]
        qr_ref[0, hd // 2, :, (hd % 2) * LANES:(hd % 2 + 1) * LANES] = (
            rope_rms(t, gqr_ref[...]).astype(BF16))

    ckv = seg("ckv", KV_RANK)
    ckv = ckv * lax.rsqrt(jnp.mean(ckv * ckv, axis=-1, keepdims=True) + EPS) * gckv_ref[...]
    kv = jnp.dot(ckv.astype(BF16), wukv_ref[...], preferred_element_type=F32)
    kr = rope_rms(seg("kr", LANES), gkr_ref[...]).astype(BF16)
    for p in range(n_pair):
        kn = _pair_rms(kv[:, p * PAIR:(p + 1) * PAIR], gkn_ref[...])
        kcat_ref[0, p, :, 0:LANES] = kn.astype(BF16)
        kcat_ref[0, p, :, LANES:2 * LANES] = kr
        vb_ref[0, p] = kv[:, nope_w + p * PAIR:nope_w + (p + 1) * PAIR].astype(BF16)


def _input_projection(x, pos3, ng, w_in_p, gq, gk, gcq, wuq_p, gckv, wukv_p, gqn, gqr, gkn, gkr,
                      freq, sgn, off):
    b, s, d = x.shape
    tm = ROW_TILE
    n_pair = 4
    sub = tm // MAX_DIL
    grid = (b, s // tm)

    def full(a):
        return pl.BlockSpec(a.shape, lambda i, j: (0,) * a.ndim)

    a_shape = jax.ShapeDtypeStruct((b, n_pair, MAX_DIL, s // MAX_DIL, PAIR), F32)
    a_spec = pl.BlockSpec((1, n_pair, MAX_DIL, sub, PAIR), lambda i, j: (i, 0, 0, j, 0))
    row_spec = lambda w: pl.BlockSpec((1, tm, w), lambda i, j: (i, j, 0))
    pair_spec = lambda w: pl.BlockSpec((1, n_pair, tm, w), lambda i, j: (i, 0, j, 0))

    out_shape = (
        a_shape, a_shape, a_shape,
        jax.ShapeDtypeStruct((b, s, n_pair * PAIR), BF16),
        jax.ShapeDtypeStruct((b, n_pair, s, PAIR), BF16),
        jax.ShapeDtypeStruct((b, n_pair, s, 2 * LANES), BF16),
        jax.ShapeDtypeStruct((b, n_pair, s, 2 * LANES), BF16),
        jax.ShapeDtypeStruct((b, n_pair, s, PAIR), BF16),
        jax.ShapeDtypeStruct((b, s, n_pair * PAIR), BF16),
    )
    out_specs = (
        a_spec, a_spec, a_spec,
        row_spec(n_pair * PAIR),
        pair_spec(PAIR), pair_spec(2 * LANES), pair_spec(2 * LANES), pair_spec(PAIR),
        row_spec(n_pair * PAIR),
    )
    in_arrays = (x, pos3, ng, w_in_p, gq, gk, gcq, wuq_p, gckv, wukv_p, gqn, gqr, gkn, gkr,
                 freq, sgn)
    in_specs = [row_spec(d), row_spec(1)] + [full(a) for a in in_arrays[2:]]
    scr = pltpu.VMEM((n_pair, tm, PAIR), F32)
    return pl.pallas_call(
        functools.partial(_proj_kernel, off=off),
        grid=grid,
        in_specs=in_specs,
        out_specs=out_specs,
        out_shape=out_shape,
        scratch_shapes=[scr, scr, scr],
        compiler_params=pltpu.CompilerParams(
            dimension_semantics=("arbitrary", "arbitrary"), vmem_limit_bytes=VMEM_LIMIT),
        name="input_projection",
    )(*in_arrays)


def _dilated_kernel(q_ref, k_ref, v_ref, bias_ref, g_ref, out_ref, obuf, lbuf, xo_ref):
    n_res = q_ref.shape[2]
    rows = q_ref.shape[3]
    lo = _lo_mask((BLOCK, PAIR))

    def block_step(pat, chunks, n):
        csize = BLOCK // len(chunks)
        first = n == 0

        def gather(ref, shift):
            parts = [ref[0, 0, r, pl.ds(pl.multiple_of(st - shift, 8), csize), :]
                     for r, st in chunks]
            return parts[0] if len(parts) == 1 else jnp.concatenate(parts, axis=0)

        prev_shift = jnp.where(first, 0, csize)
        qc = gather(q_ref, 0)
        kk = jnp.concatenate([gather(k_ref, prev_shift), gather(k_ref, 0)], axis=0).astype(BF16)
        vv = jnp.concatenate([gather(v_ref, prev_shift), gather(v_ref, 0)], axis=0).astype(BF16)
        lhs = jnp.concatenate([jnp.where(lo, qc, 0.0), jnp.where(lo, 0.0, qc)],
                              axis=0).astype(BF16)
        s = _nt_dot(lhs, kk) + bias_ref[pat, jnp.where(first, 1, 0), 0]
        m = jnp.max(s, axis=1, keepdims=True)
        p = jnp.exp(s - m)
        l = jnp.sum(p, axis=1, keepdims=True)
        o = jnp.dot(p.astype(BF16), vv, preferred_element_type=F32) / l
        lg = m + jnp.log(l)
        o2 = jnp.where(lo, o[:BLOCK], o[BLOCK:])
        l2 = jnp.where(lo, lg[:BLOCK], lg[BLOCK:])
        for c, (r, st) in enumerate(chunks):
            st = pl.multiple_of(st, 8)
            obuf[pat, r, pl.ds(st, csize), :] = o2[c * csize:(c + 1) * csize]
            lbuf[pat, r, pl.ds(st, csize), :] = l2[c * csize:(c + 1) * csize]

    for pat, dil in enumerate(DILATIONS):
        n_chunk = MAX_DIL // dil
        csize = BLOCK // n_chunk
        n_blocks = rows // csize
        n_streams = dil

        def body(i, carry, pat=pat, n_chunk=n_chunk, csize=csize, n_blocks=n_blocks, dil=dil):
            stream = i // n_blocks
            n = i % n_blocks
            chunks = [(stream + dil * c, n * csize) for c in range(n_chunk)]
            block_step(pat, chunks, n)
            return carry

        lax.fori_loop(0, n_streams * n_blocks, body, 0)

    def merge(r, carry):
        l0, l1, l2 = lbuf[0, r], lbuf[1, r], lbuf[2, r]
        lm = jnp.maximum(jnp.maximum(l0, l1), l2)
        w0, w1, w2 = jnp.exp(l0 - lm), jnp.exp(l1 - lm), jnp.exp(l2 - lm)
        o = (w0 * obuf[0, r] + w1 * obuf[1, r] + w2 * obuf[2, r]) / (w0 + w1 + w2)
        xo_ref[pl.ds(r, rows, stride=n_res), :] = o
        return carry

    lax.fori_loop(0, n_res, merge, 0)
    out_ref[0] = (xo_ref[...] * g_ref[0].astype(F32)).astype(BF16)


def _dilated_attention(qa, ka, va, bias, ga):
    b, n_pair, n_res, rows, _ = qa.shape
    s = n_res * rows
    a_spec = pl.BlockSpec((1, 1, n_res, rows, PAIR), lambda i, p: (i, p, 0, 0, 0))
    lane_spec = pl.BlockSpec((1, s, PAIR), lambda i, p: (i, 0, p))
    bias_spec = pl.BlockSpec((len(DILATIONS), 2, 1, 2 * BLOCK, 2 * BLOCK),
                             lambda i, p: (0, 0, p, 0, 0))
    buf = pltpu.VMEM((len(DILATIONS), n_res, rows, PAIR), F32)
    return pl.pallas_call(
        _dilated_kernel,
        grid=(b, n_pair),
        in_specs=[a_spec, a_spec, a_spec, bias_spec, lane_spec],
        out_specs=lane_spec,
        out_shape=jax.ShapeDtypeStruct((b, s, n_pair * PAIR), BF16),
        scratch_shapes=[buf, buf, pltpu.VMEM((s, PAIR), F32)],
        compiler_params=pltpu.CompilerParams(
            dimension_semantics=("arbitrary", "arbitrary"), vmem_limit_bytes=VMEM_LIMIT),
        name="dilated_attention",
    )(qa, ka, va, bias, ga)


def _mla_kernel(qn_ref, qr_ref, k_ref, v_ref, g_ref, out_ref, m_ref, l_ref, acc_ref):
    t = MLA_TILE
    s_len = k_ref.shape[2]
    lo_q = _lo_mask((t, PAIR))

    def q_tile(qi, carry):
        row0 = pl.multiple_of(qi * t, t)
        qn = qn_ref[0, 0, pl.ds(row0, t), :].astype(F32)
        qr = qr_ref[0, 0, pl.ds(row0, t), :]
        q_even = jnp.where(lo_q, qn, 0.0).astype(BF16)
        q_odd = jnp.where(lo_q, 0.0, qn).astype(BF16)
        lhs = jnp.concatenate([
            jnp.concatenate([q_even, qr[:, :LANES]], axis=1),
            jnp.concatenate([q_odd, qr[:, LANES:]], axis=1)], axis=0)
        m_ref[...] = jnp.full(m_ref.shape, NEG, F32)
        l_ref[...] = jnp.zeros(l_ref.shape, F32)
        acc_ref[...] = jnp.zeros(acc_ref.shape, F32)

        def kv_step(j, masked):
            col0 = pl.multiple_of(j * t, t)
            kb = k_ref[0, 0, pl.ds(col0, t), :]
            vb = v_ref[0, 0, pl.ds(col0, t), :]
            s = _nt_dot(lhs, kb)
            if masked:
                row = lax.broadcasted_iota(jnp.int32, s.shape, 0) & (t - 1)
                col = lax.broadcasted_iota(jnp.int32, s.shape, 1)
                s = jnp.where(col <= row, s, NEG)
            m_prev = m_ref[...]
            m_next = jnp.maximum(m_prev, jnp.max(s, axis=1, keepdims=True))
            alpha = jnp.exp(m_prev - m_next)
            p = jnp.exp(s - jnp.concatenate([m_next] * (t // LANES), axis=1))
            l_ref[...] = alpha * l_ref[...] + jnp.sum(p, axis=1, keepdims=True)
            acc_ref[...] = alpha * acc_ref[...] + jnp.dot(
                p.astype(BF16), vb, preferred_element_type=F32)
            m_ref[...] = m_next

        def full_step(j, c):
            kv_step(j, False)
            return c

        lax.fori_loop(0, qi, full_step, 0)
        kv_step(qi, True)
        o = acc_ref[...] / l_ref[...]
        o2 = jnp.where(lo_q, o[:t], o[t:])
        gate = g_ref[0, pl.ds(row0, t), :].astype(F32)
        out_ref[0, pl.ds(row0, t), :] = (o2 * gate).astype(BF16)
        return carry

    lax.fori_loop(0, s_len // t, q_tile, 0)


def _latent_attention(qn, qr, kcat, vb, gb):
    b, n_pair, s, _ = qn.shape
    t = MLA_TILE
    pair_spec = lambda w: pl.BlockSpec((1, 1, s, w), lambda i, p: (i, p, 0, 0))
    lane_spec = pl.BlockSpec((1, s, PAIR), lambda i, p: (i, 0, p))
    stat = pltpu.VMEM((2 * t, LANES), F32)
    return pl.pallas_call(
        _mla_kernel,
        grid=(b, n_pair),
        in_specs=[pair_spec(PAIR), pair_spec(2 * LANES), pair_spec(2 * LANES), pair_spec(PAIR),
                  lane_spec],
        out_specs=lane_spec,
        out_shape=jax.ShapeDtypeStruct((b, s, n_pair * PAIR), BF16),
        scratch_shapes=[stat, stat, stat],
        compiler_params=pltpu.CompilerParams(
            dimension_semantics=("arbitrary", "arbitrary"), vmem_limit_bytes=VMEM_LIMIT),
        name="latent_attention",
    )(qn, qr, kcat, vb, gb)


def _out_kernel(ma_ref, mb_ref, x_ref, wa_ref, wb_ref, out_ref):
    acc = jnp.dot(ma_ref[0], wa_ref[...], preferred_element_type=F32)
    acc = acc + jnp.dot(mb_ref[0], wb_ref[...], preferred_element_type=F32)
    out_ref[0] = x_ref[0] + acc


def _output_projection(ma, mb, x, wa, wb):
    b, s, d = x.shape
    tm = ROW_TILE
    row_spec = lambda w: pl.BlockSpec((1, tm, w), lambda i, j: (i, j, 0))
    full = lambda a: pl.BlockSpec(a.shape, lambda i, j: (0,) * a.ndim)
    return pl.pallas_call(
        _out_kernel,
        grid=(b, s // tm),
        in_specs=[row_spec(ma.shape[-1]), row_spec(mb.shape[-1]), row_spec(d), full(wa), full(wb)],
        out_specs=row_spec(d),
        out_shape=jax.ShapeDtypeStruct((b, s, d), x.dtype),
        compiler_params=pltpu.CompilerParams(
            dimension_semantics=("arbitrary", "arbitrary"), vmem_limit_bytes=VMEM_LIMIT),
        name="output_projection",
    )(ma, mb, x, wa, wb)


def _t5_bucket(dist):
    max_exact = REL_BUCKETS // 2
    d = jnp.maximum(dist.astype(F32), 1.0)
    large = max_exact + (jnp.log(d / max_exact) / math.log(REL_MAX_DISTANCE / max_exact)
                         * (REL_BUCKETS - max_exact)).astype(jnp.int32)
    large = jnp.minimum(large, REL_BUCKETS - 1)
    return jnp.where(dist < max_exact, dist, large)


def _dilated_bias(rel_bias, n_heads):
    tiles = []
    for dil in DILATIONS:
        n_chunk = MAX_DIL // dil
        csize = BLOCK // n_chunk
        idx = np.arange(BLOCK)
        true_idx = n_chunk * (idx % csize) + idx // csize
        qi = true_idx[:, None]
        ki = np.concatenate([true_idx, true_idx + BLOCK])[None, :]
        j = qi + BLOCK - ki
        valid = (j >= 0) & (j <= BLOCK)
        bias = rel_bias[_t5_bucket(jnp.asarray(np.maximum(j, 0) * dil, jnp.int32))]
        bias = jnp.transpose(bias, (2, 0, 1)).astype(F32)
        normal = jnp.where(valid[None], bias, NEG)
        first = jnp.where((valid & (ki >= BLOCK))[None], bias, NEG)
        both = jnp.stack([normal, first], axis=0)
        tiles.append(both.reshape(2, n_heads // 2, 2 * BLOCK, 2 * BLOCK))
    return jnp.stack(tiles, axis=0)


def _layer(x, pos3, rel_bias, norm_gain, w_in, a_q_gain, a_k_gain, q_c_gain, w_uq, kv_c_gain,
           w_ukv, qn_gain, qr_gain, kn_gain, kr_gain, w_out):
    d = x.shape[-1]
    n_heads = 8
    a_width = n_heads * HEAD_DIM
    row = lambda v: v.reshape(1, -1).astype(F32)
    pad_rope = lambda v: jnp.pad(v, (0, LANES - ROPE_DIM))

    splits = (a_width, a_width, a_width, a_width, Q_RANK, KV_RANK, ROPE_DIM, a_width)
    starts = np.concatenate([[0], np.cumsum(splits)])
    kr0, kr1 = int(starts[6]), int(starts[7])
    w_in_p = jnp.concatenate(
        [w_in[:, :kr1], jnp.zeros((d, LANES - ROPE_DIM), w_in.dtype), w_in[:, kr1:]],
        axis=1).astype(BF16)
    off = dict(qa=int(starts[0]), ka=int(starts[1]), va=int(starts[2]), za=int(starts[3]),
               cq=int(starts[4]), ckv=int(starts[5]), kr=kr0, zb=kr0 + LANES)

    qk_dim = HEAD_DIM + ROPE_DIM
    wq = w_uq.reshape(Q_RANK, n_heads, qk_dim)
    wq_rope = jnp.pad(wq[:, :, HEAD_DIM:], ((0, 0), (0, 0), (0, LANES - ROPE_DIM)))
    wuq_p = jnp.concatenate([wq[:, :, :HEAD_DIM].reshape(Q_RANK, -1),
                             wq_rope.reshape(Q_RANK, -1)], axis=1).astype(BF16)
    wkv = w_ukv.reshape(KV_RANK, n_heads, 2 * HEAD_DIM)
    wukv_p = jnp.concatenate([wkv[:, :, :HEAD_DIM].reshape(KV_RANK, -1),
                              wkv[:, :, HEAD_DIM:].reshape(KV_RANK, -1)], axis=1).astype(BF16)

    scale_a = HEAD_DIM ** -0.5
    scale_b = qk_dim ** -0.5
    inv_freq = ROPE_THETA ** (-jnp.arange(0, ROPE_DIM, 2, dtype=F32) / ROPE_DIM)
    freq = row(pad_rope(jnp.concatenate([inv_freq, inv_freq])))
    sgn = row(pad_rope(jnp.concatenate([-jnp.ones(ROPE_DIM // 2, F32),
                                        jnp.ones(ROPE_DIM // 2, F32)])))

    qa, ka, va, ga, qn, qr, kcat, vb, gb = _input_projection(
        x, pos3, row(norm_gain), w_in_p,
        row(jnp.tile(a_q_gain, 2) * scale_a), row(jnp.tile(a_k_gain, 2)),
        row(q_c_gain), wuq_p, row(kv_c_gain), wukv_p,
        row(jnp.tile(qn_gain, 2) * scale_b), row(pad_rope(qr_gain) * scale_b),
        row(jnp.tile(kn_gain, 2)), row(pad_rope(kr_gain)), freq, sgn, off)

    mixed_a = _dilated_attention(qa, ka, va, _dilated_bias(rel_bias, n_heads), ga)
    mixed_b = _latent_attention(qn, qr, kcat, vb, gb)
    w_out_b = w_out.astype(BF16)
    return _output_projection(mixed_a, mixed_b, x, w_out_b[:a_width], w_out_b[a_width:])


def kernel(x, positions, rel_bias, norm_gain, w_in, a_q_gain, a_k_gain, q_c_gain, w_uq, kv_c_gain,
           w_ukv, qn_gain, qr_gain, kn_gain, kr_gain, w_out):
    pos3 = positions[..., None]
    for l in range(norm_gain.shape[0]):
        x = _layer(x, pos3, rel_bias, norm_gain[l], w_in[l], a_q_gain[l], a_k_gain[l],
                   q_c_gain[l], w_uq[l], kv_c_gain[l], w_ukv[l], qn_gain[l], qr_gain[l],
                   kn_gain[l], kr_gain[l], w_out[l])
    return x
```

```python
import functools
import math

import jax
import jax.numpy as jnp
import numpy as np
from jax import lax
from jax.experimental import pallas as pl
from jax.experimental.pallas import tpu as pltpu

F32 = jnp.float32
BF16 = jnp.bfloat16

LANES = 128
HEAD_DIM = 64
PAIR = 2 * HEAD_DIM
ROPE_DIM = 32
KV_RANK = 128
Q_RANK = 256
BLOCK = 128
DILATIONS = (1, 4, 16)
MAX_DIL = 16
ROPE_THETA = 10000.0
REL_BUCKETS = 32
REL_MAX_DISTANCE = 2048
EPS = 1e-6
NEG = -1e30
EXP_HEADROOM = 60.0

ROW_TILE = 512
MLA_Q_TILE = 512
MLA_K_TILE = 512
MLA_Q_SUB = 256

VMEM_LIMIT = 56 * 1024 * 1024


def _nt_dot(a, b):
    return lax.dot_general(a, b, (((1,), (1,)), ((), ())), preferred_element_type=F32)


def _lo_mask(shape):
    return lax.broadcasted_iota(jnp.int32, shape, len(shape) - 1) < HEAD_DIM


def _pair_rms(t, gain_row):
    lo = _lo_mask(t.shape)
    sq = t * t
    s_lo = jnp.sum(jnp.where(lo, sq, 0.0), axis=1, keepdims=True)
    s_hi = jnp.sum(jnp.where(lo, 0.0, sq), axis=1, keepdims=True)
    r = jnp.where(lo, lax.rsqrt(s_lo * (1.0 / HEAD_DIM) + EPS),
                  lax.rsqrt(s_hi * (1.0 / HEAD_DIM) + EPS))
    return t * r * gain_row


def _silu(z):
    return z * (1.0 / (1.0 + jnp.exp(-z)))


def _proj_kernel(x_ref, pos_ref, ng_ref, w_in_ref, gq_ref, gk_ref, gcq_ref, wuq_ref,
                 gckv_ref, wukv_ref, gqn_ref, gqr_ref, gkn_ref, gkr_ref, freq_ref, sgn_ref,
                 qa_ref, ka_ref, va_ref, ga_ref, qn_ref, qr_ref, kcat_ref, vt_ref, gb_ref,
                 sq_ref, sk_ref, sv_ref, *, off):
    tm = x_ref.shape[1]
    sub = tm // MAX_DIL
    x = x_ref[0]
    h = x * lax.rsqrt(jnp.mean(x * x, axis=-1, keepdims=True) + EPS) * ng_ref[...]
    hb = h.astype(BF16)

    def seg(name, width):
        a = off[name]
        return jnp.dot(hb, w_in_ref[:, a:a + width], preferred_element_type=F32)

    n_pair = sq_ref.shape[0]

    def emit_a(t, gain_ref, scr_ref, out_ref):
        for p in range(n_pair):
            c = t[:, p * PAIR:(p + 1) * PAIR]
            if gain_ref is not None:
                c = _pair_rms(c, gain_ref[...])
            scr_ref[p] = c
        for p in range(n_pair):
            for r in range(MAX_DIL):
                out_ref[0, p, r] = scr_ref[p, pl.ds(r, sub, stride=MAX_DIL), :]

    emit_a(seg("qa", n_pair * PAIR), gq_ref, sq_ref, qa_ref)
    emit_a(seg("ka", n_pair * PAIR), gk_ref, sk_ref, ka_ref)
    emit_a(seg("va", n_pair * PAIR), None, sv_ref, va_ref)
    ga_ref[0] = _silu(seg("za", n_pair * PAIR)).astype(BF16)
    gb_ref[0] = _silu(seg("zb", n_pair * PAIR)).astype(BF16)

    ang = pos_ref[0].astype(F32) * freq_ref[...]
    cos_t = jnp.cos(ang)
    sin_t = jnp.sin(ang) * sgn_ref[...]
    lane = lax.broadcasted_iota(jnp.int32, (tm, LANES), 1)
    first_half = lane < (ROPE_DIM // 2)

    def rope(t):
        partner = jnp.where(first_half, pltpu.roll(t, LANES - ROPE_DIM // 2, 1),
                            pltpu.roll(t, ROPE_DIM // 2, 1))
        return t * cos_t + partner * sin_t

    def rope_rms(t, gain_row):
        ss = jnp.sum(t * t, axis=1, keepdims=True) * (1.0 / ROPE_DIM)
        return rope(t * lax.rsqrt(ss + EPS) * gain_row)

    cq = seg("cq", Q_RANK)
    cq = cq * lax.rsqrt(jnp.mean(cq * cq, axis=-1, keepdims=True) + EPS) * gcq_ref[...]
    q = jnp.dot(cq.astype(BF16), wuq_ref[...], preferred_element_type=F32)
    nope_w = n_pair * PAIR
    for p in range(n_pair):
        qn_ref[0, p] = _pair_rms(q[:, p * PAIR:(p + 1) * PAIR], gqn_ref[...]).astype(BF16)
    for hd in range(2 * n_pair):
        t = q[:, nope_w + hd * LANES:nope_w + (hd + 1) * LANES]
        qr_ref[0, hd // 2, :, (hd % 2) * LANES:(hd % 2 + 1) * LANES] = (
            rope_rms(t, gqr_ref[...]).astype(BF16))

    ckv = seg("ckv", KV_RANK)
    ckv = ckv * lax.rsqrt(jnp.mean(ckv * ckv, axis=-1, keepdims=True) + EPS) * gckv_ref[...]
    kv = jnp.dot(ckv.astype(BF16), wukv_ref[...], preferred_element_type=F32)
    kr = rope_rms(seg("kr", LANES), gkr_ref[...]).astype(BF16)
    for p in range(n_pair):
        kn = _pair_rms(kv[:, p * PAIR:(p + 1) * PAIR], gkn_ref[...])
        kcat_ref[0, p, :, 0:LANES] = kn.astype(BF16)
        kcat_ref[0, p, :, LANES:2 * LANES] = kr
        vt_ref[0, p] = kv[:, nope_w + p * PAIR:nope_w + (p + 1) * PAIR].T.astype(BF16)


def _input_projection(x, pos3, ng, w_in_p, gq, gk, gcq, wuq_p, gckv, wukv_p, gqn, gqr, gkn, gkr,
                      freq, sgn, off):
    b, s, d = x.shape
    tm = ROW_TILE
    n_pair = 4
    sub = tm // MAX_DIL
    grid = (b, s // tm)

    def full(a):
        return pl.BlockSpec(a.shape, lambda i, j: (0,) * a.ndim)

    a_shape = jax.ShapeDtypeStruct((b, n_pair, MAX_DIL, s // MAX_DIL, PAIR), F32)
    a_spec = pl.BlockSpec((1, n_pair, MAX_DIL, sub, PAIR), lambda i, j: (i, 0, 0, j, 0))
    row_spec = lambda w: pl.BlockSpec((1, tm, w), lambda i, j: (i, j, 0))
    pair_spec = lambda w: pl.BlockSpec((1, n_pair, tm, w), lambda i, j: (i, 0, j, 0))

    out_shape = (
        a_shape, a_shape, a_shape,
        jax.ShapeDtypeStruct((b, s, n_pair * PAIR), BF16),
        jax.ShapeDtypeStruct((b, n_pair, s, PAIR), BF16),
        jax.ShapeDtypeStruct((b, n_pair, s, 2 * LANES), BF16),
        jax.ShapeDtypeStruct((b, n_pair, s, 2 * LANES), BF16),
        jax.ShapeDtypeStruct((b, n_pair, PAIR, s), BF16),
        jax.ShapeDtypeStruct((b, s, n_pair * PAIR), BF16),
    )
    out_specs = (
        a_spec, a_spec, a_spec,
        row_spec(n_pair * PAIR),
        pair_spec(PAIR), pair_spec(2 * LANES), pair_spec(2 * LANES),
        pl.BlockSpec((1, n_pair, PAIR, tm), lambda i, j: (i, 0, 0, j)),
        row_spec(n_pair * PAIR),
    )
    in_arrays = (x, pos3, ng, w_in_p, gq, gk, gcq, wuq_p, gckv, wukv_p, gqn, gqr, gkn, gkr,
                 freq, sgn)
    in_specs = [row_spec(d), row_spec(1)] + [full(a) for a in in_arrays[2:]]
    scr = pltpu.VMEM((n_pair, tm, PAIR), F32)
    return pl.pallas_call(
        functools.partial(_proj_kernel, off=off),
        grid=grid,
        in_specs=in_specs,
        out_specs=out_specs,
        out_shape=out_shape,
        scratch_shapes=[scr, scr, scr],
        compiler_params=pltpu.CompilerParams(
            dimension_semantics=("arbitrary", "arbitrary"), vmem_limit_bytes=VMEM_LIMIT),
        name="input_projection",
    )(*in_arrays)


def _dilated_kernel(q_ref, k_ref, v_ref, bias_ref, g_ref, out_ref, obuf, lbuf, xo_ref):
    n_res = q_ref.shape[2]
    rows = q_ref.shape[3]
    lo = _lo_mask((BLOCK, PAIR))

    def block_step(pat, chunks, n):
        csize = BLOCK // len(chunks)
        first = n == 0

        def gather(ref, shift):
            parts = [ref[0, 0, r, pl.ds(pl.multiple_of(st - shift, 8), csize), :]
                     for r, st in chunks]
            return parts[0] if len(parts) == 1 else jnp.concatenate(parts, axis=0)

        prev_shift = jnp.where(first, 0, csize)
        qc = gather(q_ref, 0)
        kk = jnp.concatenate([gather(k_ref, prev_shift), gather(k_ref, 0)], axis=0).astype(BF16)
        vv = jnp.concatenate([gather(v_ref, prev_shift), gather(v_ref, 0)], axis=0).astype(BF16)
        lhs = jnp.concatenate([jnp.where(lo, qc, 0.0), jnp.where(lo, 0.0, qc)],
                              axis=0).astype(BF16)
        s = _nt_dot(lhs, kk) + bias_ref[pat, jnp.where(first, 1, 0), 0]
        m = jnp.max(s, axis=1, keepdims=True)
        p = jnp.exp(s - m)
        l = jnp.sum(p, axis=1, keepdims=True)
        o = jnp.dot(p.astype(BF16), vv, preferred_element_type=F32) / l
        lg = m + jnp.log(l)
        o2 = jnp.where(lo, o[:BLOCK], o[BLOCK:])
        l2 = jnp.where(lo, lg[:BLOCK], lg[BLOCK:])
        for c, (r, st) in enumerate(chunks):
            st = pl.multiple_of(st, 8)
            obuf[pat, r, pl.ds(st, csize), :] = o2[c * csize:(c + 1) * csize]
            lbuf[pat, r, pl.ds(st, csize), :] = l2[c * csize:(c + 1) * csize]

    for pat, dil in enumerate(DILATIONS):
        n_chunk = MAX_DIL // dil
        csize = BLOCK // n_chunk
        n_blocks = rows // csize
        n_streams = dil

        def body(i, carry, pat=pat, n_chunk=n_chunk, csize=csize, n_blocks=n_blocks, dil=dil):
            stream = i // n_blocks
            n = i % n_blocks
            chunks = [(stream + dil * c, n * csize) for c in range(n_chunk)]
            block_step(pat, chunks, n)
            return carry

        lax.fori_loop(0, n_streams * n_blocks, body, 0)

    def merge(r, carry):
        l0, l1, l2 = lbuf[0, r], lbuf[1, r], lbuf[2, r]
        lm = jnp.maximum(jnp.maximum(l0, l1), l2)
        w0, w1, w2 = jnp.exp(l0 - lm), jnp.exp(l1 - lm), jnp.exp(l2 - lm)
        o = (w0 * obuf[0, r] + w1 * obuf[1, r] + w2 * obuf[2, r]) / (w0 + w1 + w2)
        xo_ref[pl.ds(r, rows, stride=n_res), :] = o
        return carry

    lax.fori_loop(0, n_res, merge, 0)
    out_ref[0] = (xo_ref[...] * g_ref[0].astype(F32)).astype(BF16)


def _dilated_attention(qa, ka, va, bias, ga):
    b, n_pair, n_res, rows, _ = qa.shape
    s = n_res * rows
    a_spec = pl.BlockSpec((1, 1, n_res, rows, PAIR), lambda i, p: (i, p, 0, 0, 0))
    lane_spec = pl.BlockSpec((1, s, PAIR), lambda i, p: (i, 0, p))
    bias_spec = pl.BlockSpec((len(DILATIONS), 2, 1, 2 * BLOCK, 2 * BLOCK),
                             lambda i, p: (0, 0, p, 0, 0))
    buf = pltpu.VMEM((len(DILATIONS), n_res, rows, PAIR), F32)
    return pl.pallas_call(
        _dilated_kernel,
        grid=(b, n_pair),
        in_specs=[a_spec, a_spec, a_spec, bias_spec, lane_spec],
        out_specs=lane_spec,
        out_shape=jax.ShapeDtypeStruct((b, s, n_pair * PAIR), BF16),
        scratch_shapes=[buf, buf, pltpu.VMEM((s, PAIR), F32)],
        compiler_params=pltpu.CompilerParams(
            dimension_semantics=("arbitrary", "arbitrary"), vmem_limit_bytes=VMEM_LIMIT),
        name="dilated_attention",
    )(qa, ka, va, bias, ga)


def _mla_kernel(qn_ref, qr_ref, k_ref, vt_ref, g_ref, out_ref, qt_ref, acc_ref):
    tq, tk, qs = MLA_Q_TILE, MLA_K_TILE, MLA_Q_SUB
    n_half = tq // qs
    n_chain = 2 * n_half
    s_len = k_ref.shape[2]
    lo_q = _lo_mask((qs, PAIR))

    def scores(c, kb):
        return jnp.dot(kb, qt_ref[c], preferred_element_type=F32)

    def chain_update(c, kb, vtb, m, l, key_offset):
        st = scores(c, kb)
        if key_offset is not None:
            key = lax.broadcasted_iota(jnp.int32, st.shape, 0) + key_offset
            qry = lax.broadcasted_iota(jnp.int32, st.shape, 1)
            st = jnp.where(key <= qry, st, NEG)
        m_new = jnp.maximum(m, jnp.max(st, axis=0, keepdims=True))
        alpha = jnp.exp(m - m_new)
        p = jnp.exp(st - m_new)
        l_new = alpha * l + jnp.sum(p, axis=0, keepdims=True)
        pv = jnp.dot(vtb, p.astype(BF16), preferred_element_type=F32)
        acc_ref[c] = alpha * acc_ref[c] + pv
        return m_new, l_new

    def q_tile(qi, carry):
        row0 = pl.multiple_of(qi * tq, tq)
        for hf in range(n_half):
            r = pl.multiple_of(row0 + hf * qs, qs)
            qn = qn_ref[0, 0, pl.ds(r, qs), :].astype(F32)
            qr = qr_ref[0, 0, pl.ds(r, qs), :].astype(F32)
            q_even = jnp.concatenate([jnp.where(lo_q, qn, 0.0), qr[:, :LANES]], axis=1)
            q_odd = jnp.concatenate([jnp.where(lo_q, 0.0, qn), qr[:, LANES:]], axis=1)
            qt_ref[2 * hf] = q_even.T.astype(BF16)
            qt_ref[2 * hf + 1] = q_odd.T.astype(BF16)
        n_full = qi * (tq // tk)

        def kv_block(j):
            col0 = pl.multiple_of(j * tk, tk)
            return k_ref[0, 0, pl.ds(col0, tk), :], vt_ref[0, 0, :, pl.ds(col0, tk)]

        def diagonal(ms, ls):
            for hf in range(n_half):
                n_keys = (hf + 1) * qs
                kb = k_ref[0, 0, pl.ds(row0, n_keys), :]
                vtb = vt_ref[0, 0, :, pl.ds(row0, n_keys)]
                for c in (2 * hf, 2 * hf + 1):
                    ms[c], ls[c] = chain_update(c, kb, vtb, ms[c], ls[c], -hf * qs)
            return ms, ls

        def fresh_stats():
            acc_ref[...] = jnp.zeros(acc_ref.shape, F32)
            return ([jnp.full((1, qs), NEG, F32) for _ in range(n_chain)],
                    [jnp.zeros((1, qs), F32) for _ in range(n_chain)])

        ms, ls = diagonal(*fresh_stats())

        def anchored_step(j, st_in):
            kb, vtb = kv_block(j)
            tops, sums = list(st_in[:n_chain]), list(st_in[n_chain:])
            ahead = 2
            sts = [scores(c, kb) for c in range(ahead)]
            for c in range(n_chain):
                if c + ahead < n_chain:
                    sts.append(scores(c + ahead, kb))
                st = sts[c]
                tops[c] = jnp.maximum(tops[c], jnp.max(st, axis=0, keepdims=True))
                p = jnp.exp(st - ms[c])
                sums[c] = sums[c] + jnp.sum(p, axis=0, keepdims=True)
                acc_ref[c] += jnp.dot(vtb, p.astype(BF16), preferred_element_type=F32)
            return tuple(tops) + tuple(sums)

        st_out = lax.fori_loop(0, n_full, anchored_step, tuple(ms) + tuple(ls))
        tops, ls = list(st_out[:n_chain]), list(st_out[n_chain:])
        unsafe = functools.reduce(
            jnp.logical_or, [jnp.any(tops[c] > ms[c] + EXP_HEADROOM) for c in range(n_chain)])

        def running_max_pass():
            m2, l2 = fresh_stats()

            def full_step(j, st_in):
                kb, vtb = kv_block(j)
                mm, ll = list(st_in[:n_chain]), list(st_in[n_chain:])
                for c in range(n_chain):
                    mm[c], ll[c] = chain_update(c, kb, vtb, mm[c], ll[c], None)
                return tuple(mm) + tuple(ll)

            st2 = lax.fori_loop(0, n_full, full_step, tuple(m2) + tuple(l2))
            _, l2 = diagonal(list(st2[:n_chain]), list(st2[n_chain:]))
            return tuple(l2)

        ls = lax.cond(unsafe, running_max_pass, lambda: tuple(ls))
        even_rows = lax.broadcasted_iota(jnp.int32, (PAIR, qs), 0) < HEAD_DIM
        for hf in range(n_half):
            ot = jnp.where(even_rows, acc_ref[2 * hf] / ls[2 * hf],
                           acc_ref[2 * hf + 1] / ls[2 * hf + 1])
            r = pl.multiple_of(row0 + hf * qs, qs)
            gate = g_ref[0, pl.ds(r, qs), :].astype(F32)
            out_ref[0, pl.ds(r, qs), :] = (ot.T * gate).astype(BF16)
        return carry

    lax.fori_loop(0, s_len // tq, q_tile, 0)


def _latent_attention(qn, qr, kcat, vt, gb):
    b, n_pair, s, _ = qn.shape
    n_chain = 2 * (MLA_Q_TILE // MLA_Q_SUB)
    pair_spec = lambda w: pl.BlockSpec((1, 1, s, w), lambda i, p: (i, p, 0, 0))
    vt_spec = pl.BlockSpec((1, 1, PAIR, s), lambda i, p: (i, p, 0, 0))
    lane_spec = pl.BlockSpec((1, s, PAIR), lambda i, p: (i, 0, p))
    return pl.pallas_call(
        _mla_kernel,
        grid=(b, n_pair),
        in_specs=[pair_spec(PAIR), pair_spec(2 * LANES), pair_spec(2 * LANES), vt_spec, lane_spec],
        out_specs=lane_spec,
        out_shape=jax.ShapeDtypeStruct((b, s, n_pair * PAIR), BF16),
        scratch_shapes=[pltpu.VMEM((n_chain, 2 * LANES, MLA_Q_SUB), BF16),
                        pltpu.VMEM((n_chain, PAIR, MLA_Q_SUB), F32)],
        compiler_params=pltpu.CompilerParams(
            dimension_semantics=("arbitrary", "arbitrary"), vmem_limit_bytes=VMEM_LIMIT),
        name="latent_attention",
    )(qn, qr, kcat, vt, gb)


def _out_kernel(ma_ref, mb_ref, x_ref, wa_ref, wb_ref, out_ref):
    acc = jnp.dot(ma_ref[0], wa_ref[...], preferred_element_type=F32)
    acc = acc + jnp.dot(mb_ref[0], wb_ref[...], preferred_element_type=F32)
    out_ref[0] = x_ref[0] + acc


def _output_projection(ma, mb, x, wa, wb):
    b, s, d = x.shape
    tm = ROW_TILE
    row_spec = lambda w: pl.BlockSpec((1, tm, w), lambda i, j: (i, j, 0))
    full = lambda a: pl.BlockSpec(a.shape, lambda i, j: (0,) * a.ndim)
    return pl.pallas_call(
        _out_kernel,
        grid=(b, s // tm),
        in_specs=[row_spec(ma.shape[-1]), row_spec(mb.shape[-1]), row_spec(d), full(wa), full(wb)],
        out_specs=row_spec(d),
        out_shape=jax.ShapeDtypeStruct((b, s, d), x.dtype),
        compiler_params=pltpu.CompilerParams(
            dimension_semantics=("arbitrary", "arbitrary"), vmem_limit_bytes=VMEM_LIMIT),
        name="output_projection",
    )(ma, mb, x, wa, wb)


def _t5_bucket(dist):
    max_exact = REL_BUCKETS // 2
    d = np.maximum(dist.astype(np.float32), np.float32(1.0))
    large = max_exact + (np.log(d / np.float32(max_exact)) / np.float32(
        math.log(REL_MAX_DISTANCE / max_exact)) * np.float32(REL_BUCKETS - max_exact)
                         ).astype(np.int32)
    large = np.minimum(large, REL_BUCKETS - 1)
    return np.where(dist < max_exact, dist, large)


def _dilated_bias(rel_bias, n_heads):
    tiles = []
    for dil in DILATIONS:
        n_chunk = MAX_DIL // dil
        csize = BLOCK // n_chunk
        idx = np.arange(BLOCK)
        true_idx = n_chunk * (idx % csize) + idx // csize
        qi = true_idx[:, None]
        ki = np.concatenate([true_idx, true_idx + BLOCK])[None, :]
        j = qi + BLOCK - ki
        valid = (j >= 0) & (j <= BLOCK)
        bucket = _t5_bucket(np.maximum(j, 0) * dil)
        bias = jnp.zeros((n_heads,) + bucket.shape, F32)
        for bkt in np.unique(bucket[valid]):
            bias = jnp.where((bucket == bkt)[None],
                             rel_bias[int(bkt)].astype(F32)[:, None, None], bias)
        normal = jnp.where(valid[None], bias, NEG)
        first = jnp.where((valid & (ki >= BLOCK))[None], bias, NEG)
        both = jnp.stack([normal, first], axis=0)
        tiles.append(both.reshape(2, n_heads // 2, 2 * BLOCK, 2 * BLOCK))
    return jnp.stack(tiles, axis=0)


def _layer(x, pos3, rel_bias, norm_gain, w_in, a_q_gain, a_k_gain, q_c_gain, w_uq, kv_c_gain,
           w_ukv, qn_gain, qr_gain, kn_gain, kr_gain, w_out):
    d = x.shape[-1]
    n_heads = 8
    a_width = n_heads * HEAD_DIM
    row = lambda v: v.reshape(1, -1).astype(F32)
    pad_rope = lambda v: jnp.pad(v, (0, LANES - ROPE_DIM))

    splits = (a_width, a_width, a_width, a_width, Q_RANK, KV_RANK, ROPE_DIM, a_width)
    starts = np.concatenate([[0], np.cumsum(splits)])
    kr0, kr1 = int(starts[6]), int(starts[7])
    w_in_p = jnp.concatenate(
        [w_in[:, :kr1], jnp.zeros((d, LANES - ROPE_DIM), w_in.dtype), w_in[:, kr1:]],
        axis=1).astype(BF16)
    off = dict(qa=int(starts[0]), ka=int(starts[1]), va=int(starts[2]), za=int(starts[3]),
               cq=int(starts[4]), ckv=int(starts[5]), kr=kr0, zb=kr0 + LANES)

    qk_dim = HEAD_DIM + ROPE_DIM
    wq = w_uq.reshape(Q_RANK, n_heads, qk_dim)
    wq_rope = jnp.pad(wq[:, :, HEAD_DIM:], ((0, 0), (0, 0), (0, LANES - ROPE_DIM)))
    wuq_p = jnp.concatenate([wq[:, :, :HEAD_DIM].reshape(Q_RANK, -1),
                             wq_rope.reshape(Q_RANK, -1)], axis=1).astype(BF16)
    wkv = w_ukv.reshape(KV_RANK, n_heads, 2 * HEAD_DIM)
    wukv_p = jnp.concatenate([wkv[:, :, :HEAD_DIM].reshape(KV_RANK, -1),
                              wkv[:, :, HEAD_DIM:].reshape(KV_RANK, -1)], axis=1).astype(BF16)

    scale_a = HEAD_DIM ** -0.5
    scale_b = qk_dim ** -0.5
    inv_freq = ROPE_THETA ** (-jnp.arange(0, ROPE_DIM, 2, dtype=F32) / ROPE_DIM)
    freq = row(pad_rope(jnp.concatenate([inv_freq, inv_freq])))
    sgn = row(pad_rope(jnp.concatenate([-jnp.ones(ROPE_DIM // 2, F32),
                                        jnp.ones(ROPE_DIM // 2, F32)])))

    qa, ka, va, ga, qn, qr, kcat, vt, gb = _input_projection(
        x, pos3, row(norm_gain), w_in_p,
        row(jnp.tile(a_q_gain, 2) * scale_a), row(jnp.tile(a_k_gain, 2)),
        row(q_c_gain), wuq_p, row(kv_c_gain), wukv_p,
        row(jnp.tile(qn_gain, 2) * scale_b), row(pad_rope(qr_gain) * scale_b),
        row(jnp.tile(kn_gain, 2)), row(pad_rope(kr_gain)), freq, sgn, off)

    mixed_a = _dilated_attention(qa, ka, va, _dilated_bias(rel_bias, n_heads), ga)
    mixed_b = _latent_attention(qn, qr, kcat, vt, gb)
    w_out_b = w_out.astype(BF16)
    return _output_projection(mixed_a, mixed_b, x, w_out_b[:a_width], w_out_b[a_width:])


def kernel(x, positions, rel_bias, norm_gain, w_in, a_q_gain, a_k_gain, q_c_gain, w_uq, kv_c_gain,
           w_ukv, qn_gain, qr_gain, kn_gain, kr_gain, w_out):
    pos3 = positions[..., None]
    for l in range(norm_gain.shape[0]):
        x = _layer(x, pos3, rel_bias, norm_gain[l], w_in[l], a_q_gain[l], a_k_gain[l],
                   q_c_gain[l], w_uq[l], kv_c_gain[l], w_ukv[l], qn_gain[l], qr_gain[l],
                   kn_gain[l], kr_gain[l], w_out[l])
    return x
```

```python
import functools
import math

import jax
import jax.numpy as jnp
import numpy as np
from jax import lax
from jax.experimental import pallas as pl
from jax.experimental.pallas import tpu as pltpu

F32 = jnp.float32
BF16 = jnp.bfloat16

LANES = 128
HEAD_DIM = 64
PAIR = 2 * HEAD_DIM
ROPE_DIM = 32
KV_RANK = 128
Q_RANK = 256
BLOCK = 128
DILATIONS = (1, 4, 16)
MAX_DIL = 16
ROPE_THETA = 10000.0
REL_BUCKETS = 32
REL_MAX_DISTANCE = 2048
EPS = 1e-6
NEG = -1e30
EXP_HEADROOM = 60.0
SAFE_MIN, SAFE_MAX = 1e-30, 1e30
LOG2E = math.log2(math.e)

ROW_TILE = 512
MLA_Q_TILE = 512
MLA_K_TILE = 512
MLA_Q_SUB = 256

VMEM_LIMIT = 56 * 1024 * 1024


def _nt_dot(a, b):
    return lax.dot_general(a, b, (((1,), (1,)), ((), ())), preferred_element_type=F32)


def _lo_mask(shape):
    return lax.broadcasted_iota(jnp.int32, shape, len(shape) - 1) < HEAD_DIM


def _pair_rms(t, gain_row):
    lo = _lo_mask(t.shape)
    sq = t * t
    s_lo = jnp.sum(jnp.where(lo, sq, 0.0), axis=1, keepdims=True)
    s_hi = jnp.sum(jnp.where(lo, 0.0, sq), axis=1, keepdims=True)
    r = jnp.where(lo, lax.rsqrt(s_lo * (1.0 / HEAD_DIM) + EPS),
                  lax.rsqrt(s_hi * (1.0 / HEAD_DIM) + EPS))
    return t * r * gain_row


def _silu(z):
    return z * (1.0 / (1.0 + jnp.exp(-z)))


def _proj_kernel(x_ref, pos_ref, ng_ref, w_in_ref, gq_ref, gk_ref, gcq_ref, wuq_ref,
                 gckv_ref, wukv_ref, gqn_ref, gqr_ref, gkn_ref, gkr_ref, freq_ref, sgn_ref,
                 qa_ref, ka_ref, va_ref, ga_ref, qn_ref, qr_ref, kcat_ref, vt_ref, gb_ref,
                 sq_ref, sk_ref, sv_ref, *, off):
    tm = x_ref.shape[1]
    sub = tm // MAX_DIL
    x = x_ref[0]
    h = x * lax.rsqrt(jnp.mean(x * x, axis=-1, keepdims=True) + EPS) * ng_ref[...]
    hb = h.astype(BF16)

    def seg(name, width):
        a = off[name]
        return jnp.dot(hb, w_in_ref[:, a:a + width], preferred_element_type=F32)

    n_pair = sq_ref.shape[0]

    def emit_a(t, gain_ref, scr_ref, out_ref):
        for p in range(n_pair):
            c = t[:, p * PAIR:(p + 1) * PAIR]
            if gain_ref is not None:
                c = _pair_rms(c, gain_ref[...])
            scr_ref[p] = c
        for p in range(n_pair):
            for r in range(MAX_DIL):
                out_ref[0, p, r] = scr_ref[p, pl.ds(r, sub, stride=MAX_DIL), :]

    emit_a(seg("qa", n_pair * PAIR), gq_ref, sq_ref, qa_ref)
    emit_a(seg("ka", n_pair * PAIR), gk_ref, sk_ref, ka_ref)
    emit_a(seg("va", n_pair * PAIR), None, sv_ref, va_ref)
    ga_ref[0] = _silu(seg("za", n_pair * PAIR)).astype(BF16)
    gb_ref[0] = _silu(seg("zb", n_pair * PAIR)).astype(BF16)

    ang = pos_ref[0].astype(F32) * freq_ref[...]
    cos_t = jnp.cos(ang)
    sin_t = jnp.sin(ang) * sgn_ref[...]
    lane = lax.broadcasted_iota(jnp.int32, (tm, LANES), 1)
    first_half = lane < (ROPE_DIM // 2)

    def rope(t):
        partner = jnp.where(first_half, pltpu.roll(t, LANES - ROPE_DIM // 2, 1),
                            pltpu.roll(t, ROPE_DIM // 2, 1))
        return t * cos_t + partner * sin_t

    def rope_rms(t, gain_row):
        ss = jnp.sum(t * t, axis=1, keepdims=True) * (1.0 / ROPE_DIM)
        return rope(t * lax.rsqrt(ss + EPS) * gain_row)

    cq = seg("cq", Q_RANK)
    cq = cq * lax.rsqrt(jnp.mean(cq * cq, axis=-1, keepdims=True) + EPS) * gcq_ref[...]
    q = jnp.dot(cq.astype(BF16), wuq_ref[...], preferred_element_type=F32)
    nope_w = n_pair * PAIR
    for p in range(n_pair):
        qn_ref[0, p] = _pair_rms(q[:, p * PAIR:(p + 1) * PAIR], gqn_ref[...]).astype(BF16)
    for hd in range(2 * n_pair):
        t = q[:, nope_w + hd * LANES:nope_w + (hd + 1) * LANES]
        qr_ref[0, hd // 2, :, (hd % 2) * LANES:(hd % 2 + 1) * LANES] = (
            rope_rms(t, gqr_ref[...]).astype(BF16))

    ckv = seg("ckv", KV_RANK)
    ckv = ckv * lax.rsqrt(jnp.mean(ckv * ckv, axis=-1, keepdims=True) + EPS) * gckv_ref[...]
    kv = jnp.dot(ckv.astype(BF16), wukv_ref[...], preferred_element_type=F32)
    kr = rope_rms(seg("kr", LANES), gkr_ref[...]).astype(BF16)
    for p in range(n_pair):
        kn = _pair_rms(kv[:, p * PAIR:(p + 1) * PAIR], gkn_ref[...])
        kcat_ref[0, p, :, 0:LANES] = kn.astype(BF16)
        kcat_ref[0, p, :, LANES:2 * LANES] = kr
        vt_ref[0, p] = kv[:, nope_w + p * PAIR:nope_w + (p + 1) * PAIR].T.astype(BF16)


def _input_projection(x, pos3, ng, w_in_p, gq, gk, gcq, wuq_p, gckv, wukv_p, gqn, gqr, gkn, gkr,
                      freq, sgn, off):
    b, s, d = x.shape
    tm = ROW_TILE
    n_pair = 4
    sub = tm // MAX_DIL
    grid = (b, s // tm)

    def full(a):
        return pl.BlockSpec(a.shape, lambda i, j: (0,) * a.ndim)

    a_shape = jax.ShapeDtypeStruct((b, n_pair, MAX_DIL, s // MAX_DIL, PAIR), F32)
    a_spec = pl.BlockSpec((1, n_pair, MAX_DIL, sub, PAIR), lambda i, j: (i, 0, 0, j, 0))
    row_spec = lambda w: pl.BlockSpec((1, tm, w), lambda i, j: (i, j, 0))
    pair_spec = lambda w: pl.BlockSpec((1, n_pair, tm, w), lambda i, j: (i, 0, j, 0))

    out_shape = (
        a_shape, a_shape, a_shape,
        jax.ShapeDtypeStruct((b, s, n_pair * PAIR), BF16),
        jax.ShapeDtypeStruct((b, n_pair, s, PAIR), BF16),
        jax.ShapeDtypeStruct((b, n_pair, s, 2 * LANES), BF16),
        jax.ShapeDtypeStruct((b, n_pair, s, 2 * LANES), BF16),
        jax.ShapeDtypeStruct((b, n_pair, PAIR, s), BF16),
        jax.ShapeDtypeStruct((b, s, n_pair * PAIR), BF16),
    )
    out_specs = (
        a_spec, a_spec, a_spec,
        row_spec(n_pair * PAIR),
        pair_spec(PAIR), pair_spec(2 * LANES), pair_spec(2 * LANES),
        pl.BlockSpec((1, n_pair, PAIR, tm), lambda i, j: (i, 0, 0, j)),
        row_spec(n_pair * PAIR),
    )
    in_arrays = (x, pos3, ng, w_in_p, gq, gk, gcq, wuq_p, gckv, wukv_p, gqn, gqr, gkn, gkr,
                 freq, sgn)
    in_specs = [row_spec(d), row_spec(1)] + [full(a) for a in in_arrays[2:]]
    scr = pltpu.VMEM((n_pair, tm, PAIR), F32)
    return pl.pallas_call(
        functools.partial(_proj_kernel, off=off),
        grid=grid,
        in_specs=in_specs,
        out_specs=out_specs,
        out_shape=out_shape,
        scratch_shapes=[scr, scr, scr],
        compiler_params=pltpu.CompilerParams(
            dimension_semantics=("arbitrary", "arbitrary"), vmem_limit_bytes=VMEM_LIMIT),
        name="input_projection",
    )(*in_arrays)


DILATED_GROUP = 4


def _dilated_schedule(pat, i, g):
    if pat == 0:
        return 0, DILATED_GROUP * i + g
    if pat == 1:
        return g, i
    return 2 * i + g // 2, g % 2


def _dilated_kernel(q_ref, k_ref, v_ref, bias_ref, g_ref, out_ref, obuf, lbuf, mbuf, xo_ref):
    n_res = q_ref.shape[2]
    rows = q_ref.shape[3]
    lo = _lo_mask((BLOCK, PAIR))
    ones = jnp.ones((2 * BLOCK, LANES), BF16)

    def pattern_geometry(pat):
        dil = DILATIONS[pat]
        n_chunk = MAX_DIL // dil
        csize = BLOCK // n_chunk
        return dil, n_chunk, csize

    def load_block(pat, stream, n):
        dil, n_chunk, csize = pattern_geometry(pat)
        chunks = [(stream + dil * c, n * csize) for c in range(n_chunk)]
        first = n == 0

        def gather(ref, shift):
            parts = [ref[0, 0, r, pl.ds(pl.multiple_of(st - shift, 8), csize), :]
                     for r, st in chunks]
            return parts[0] if len(parts) == 1 else jnp.concatenate(parts, axis=0)

        prev_shift = jnp.where(first, 0, csize)
        qc = gather(q_ref, 0)
        lhs = jnp.concatenate([jnp.where(lo, qc, 0.0), jnp.where(lo, 0.0, qc)],
                              axis=0).astype(BF16)
        kk = jnp.concatenate([gather(k_ref, prev_shift), gather(k_ref, 0)], axis=0).astype(BF16)
        vv = jnp.concatenate([gather(v_ref, prev_shift), gather(v_ref, 0)], axis=0).astype(BF16)
        vv1 = jnp.concatenate([vv, ones], axis=1)
        bias = bias_ref[pat, jnp.where(first, 1, 0), 0]
        return chunks, csize, lhs, kk, vv1, bias

    def split_heads(r):
        return jnp.where(lo, r[:BLOCK], r[BLOCK:])

    def store_chunks(buf, lead, chunks, csize, val):
        for c, (r, st) in enumerate(chunks):
            buf[lead + (r, pl.ds(pl.multiple_of(st, 8), csize), slice(None))] = (
                val[c * csize:(c + 1) * csize])

    def fast_body(i, carry):
        slots = [(pat, g) for g in range(DILATED_GROUP) for pat in range(len(DILATIONS))]
        blocks = [load_block(pat, *_dilated_schedule(pat, i, g)) for pat, g in slots]
        ahead = 3
        scores = [_nt_dot(b[2], b[3]) for b in blocks[:ahead]]
        for k, (chunks, csize, _, _, vv1, bias) in enumerate(blocks):
            if k + ahead < len(blocks):
                nb = blocks[k + ahead]
                scores.append(_nt_dot(nb[2], nb[3]))
            p = jnp.exp2(scores[k] + bias).astype(BF16)
            r = jnp.dot(p, vv1, preferred_element_type=F32)
            lead = (slots[k][0],)
            store_chunks(obuf, lead, chunks, csize, split_heads(r[:, :LANES]))
            store_chunks(lbuf, lead, chunks, csize, split_heads(r[:, LANES:]))
        return carry

    lax.fori_loop(0, 32 // DILATED_GROUP, fast_body, 0)

    def pattern_sum(r, stats):
        num = obuf[0, r] + obuf[1, r] + obuf[2, r]
        den = lbuf[0, r] + lbuf[1, r] + lbuf[2, r]
        obuf[0, r] = num
        lbuf[0, r] = den
        den_lo, den_hi, num_hi = stats
        return (jnp.minimum(den_lo, jnp.min(den, axis=0, keepdims=True)),
                jnp.maximum(den_hi, jnp.max(den, axis=0, keepdims=True)),
                jnp.maximum(num_hi, jnp.max(jnp.abs(num), axis=0, keepdims=True)))

    row = lambda v: jnp.full((1, PAIR), v, F32)
    den_lo, den_hi, num_hi = lax.fori_loop(0, n_res, pattern_sum, (row(1.0), row(1.0), row(0.0)))
    unsafe = jnp.logical_not((jnp.min(den_lo) >= SAFE_MIN) & (jnp.max(den_hi) <= SAFE_MAX)
                             & (jnp.max(num_hi) <= SAFE_MAX))

    @pl.when(unsafe)
    def _():
        for pat in range(len(DILATIONS)):
            def body(i, carry, pat=pat):
                n_blocks = 32 // DILATIONS[pat]
                chunks, csize, lhs, kk, vv1, bias = load_block(pat, i // n_blocks, i % n_blocks)
                t = _nt_dot(lhs, kk) + bias
                m = jnp.max(t, axis=1, keepdims=True)
                p = jnp.exp2(t - m).astype(BF16)
                r = jnp.dot(p, vv1, preferred_element_type=F32)
                num_blk, den_blk = split_heads(r[:, :LANES]), split_heads(r[:, LANES:])
                m2 = jnp.where(lo, m[:BLOCK], m[BLOCK:])
                if pat == 0:
                    store_chunks(obuf, (0,), chunks, csize, num_blk)
                    store_chunks(lbuf, (0,), chunks, csize, den_blk)
                    store_chunks(mbuf, (), chunks, csize, m2)
                else:
                    for c, (r16, st) in enumerate(chunks):
                        rows_c = pl.ds(pl.multiple_of(st, 8), csize)
                        piece = slice(c * csize, (c + 1) * csize)
                        m_old = mbuf[r16, rows_c, :]
                        m_new = jnp.maximum(m_old, m2[piece])
                        a, b = jnp.exp2(m_old - m_new), jnp.exp2(m2[piece] - m_new)
                        obuf[0, r16, rows_c, :] = a * obuf[0, r16, rows_c, :] + b * num_blk[piece]
                        lbuf[0, r16, rows_c, :] = a * lbuf[0, r16, rows_c, :] + b * den_blk[piece]
                        mbuf[r16, rows_c, :] = m_new
                return carry

            lax.fori_loop(0, 32, body, 0)

    def merge(r, carry):
        xo_ref[pl.ds(r, rows, stride=n_res), :] = obuf[0, r] / lbuf[0, r]
        return carry

    lax.fori_loop(0, n_res, merge, 0)
    out_ref[0] = (xo_ref[...] * g_ref[0].astype(F32)).astype(BF16)


def _dilated_attention(qa, ka, va, bias, ga):
    b, n_pair, n_res, rows, _ = qa.shape
    s = n_res * rows
    a_spec = pl.BlockSpec((1, 1, n_res, rows, PAIR), lambda i, p: (i, p, 0, 0, 0))
    lane_spec = pl.BlockSpec((1, s, PAIR), lambda i, p: (i, 0, p))
    bias_spec = pl.BlockSpec((len(DILATIONS), 2, 1, 2 * BLOCK, 2 * BLOCK),
                             lambda i, p: (0, 0, p, 0, 0))
    buf = pltpu.VMEM((n_res, rows, PAIR), F32)
    pat_buf = pltpu.VMEM((len(DILATIONS), n_res, rows, PAIR), F32)
    return pl.pallas_call(
        _dilated_kernel,
        grid=(b, n_pair),
        in_specs=[a_spec, a_spec, a_spec, bias_spec, lane_spec],
        out_specs=lane_spec,
        out_shape=jax.ShapeDtypeStruct((b, s, n_pair * PAIR), BF16),
        scratch_shapes=[pat_buf, pat_buf, buf, pltpu.VMEM((s, PAIR), F32)],
        compiler_params=pltpu.CompilerParams(
            dimension_semantics=("arbitrary", "arbitrary"), vmem_limit_bytes=VMEM_LIMIT),
        name="dilated_attention",
    )(qa, ka, va, bias, ga)


def _mla_kernel(qn_ref, qr_ref, k_ref, vt_ref, g_ref, out_ref, qt_ref, acc_ref):
    tq, tk, qs = MLA_Q_TILE, MLA_K_TILE, MLA_Q_SUB
    n_half = tq // qs
    n_chain = 2 * n_half
    s_len = k_ref.shape[2]
    lo_q = _lo_mask((qs, PAIR))

    def scores(c, kb):
        return jnp.dot(kb, qt_ref[c], preferred_element_type=F32)

    def chain_update(c, kb, vtb, m, l, key_offset):
        st = scores(c, kb)
        if key_offset is not None:
            key = lax.broadcasted_iota(jnp.int32, st.shape, 0) + key_offset
            qry = lax.broadcasted_iota(jnp.int32, st.shape, 1)
            st = jnp.where(key <= qry, st, NEG)
        m_new = jnp.maximum(m, jnp.max(st, axis=0, keepdims=True))
        alpha = jnp.exp(m - m_new)
        p = jnp.exp(st - m_new)
        l_new = alpha * l + jnp.sum(p, axis=0, keepdims=True)
        pv = jnp.dot(vtb, p.astype(BF16), preferred_element_type=F32)
        acc_ref[c] = alpha * acc_ref[c] + pv
        return m_new, l_new

    def q_tile(qi, carry):
        row0 = pl.multiple_of(qi * tq, tq)
        for hf in range(n_half):
            r = pl.multiple_of(row0 + hf * qs, qs)
            qn = qn_ref[0, 0, pl.ds(r, qs), :].astype(F32)
            qr = qr_ref[0, 0, pl.ds(r, qs), :].astype(F32)
            q_even = jnp.concatenate([jnp.where(lo_q, qn, 0.0), qr[:, :LANES]], axis=1)
            q_odd = jnp.concatenate([jnp.where(lo_q, 0.0, qn), qr[:, LANES:]], axis=1)
            qt_ref[2 * hf] = q_even.T.astype(BF16)
            qt_ref[2 * hf + 1] = q_odd.T.astype(BF16)
        n_full = qi * (tq // tk)

        def kv_block(j):
            col0 = pl.multiple_of(j * tk, tk)
            return k_ref[0, 0, pl.ds(col0, tk), :], vt_ref[0, 0, :, pl.ds(col0, tk)]

        def diagonal(ms, ls):
            for hf in range(n_half):
                n_keys = (hf + 1) * qs
                kb = k_ref[0, 0, pl.ds(row0, n_keys), :]
                vtb = vt_ref[0, 0, :, pl.ds(row0, n_keys)]
                for c in (2 * hf, 2 * hf + 1):
                    ms[c], ls[c] = chain_update(c, kb, vtb, ms[c], ls[c], -hf * qs)
            return ms, ls

        def fresh_stats():
            acc_ref[...] = jnp.zeros(acc_ref.shape, F32)
            return ([jnp.full((1, qs), NEG, F32) for _ in range(n_chain)],
                    [jnp.zeros((1, qs), F32) for _ in range(n_chain)])

        ms, ls = diagonal(*fresh_stats())

        def anchored_step(j, st_in):
            kb, vtb = kv_block(j)
            tops, sums = list(st_in[:n_chain]), list(st_in[n_chain:])
            ahead = 2
            sts = [scores(c, kb) for c in range(ahead)]
            for c in range(n_chain):
                if c + ahead < n_chain:
                    sts.append(scores(c + ahead, kb))
                st = sts[c]
                tops[c] = jnp.maximum(tops[c], jnp.max(st, axis=0, keepdims=True))
                p = jnp.exp(st - ms[c])
                sums[c] = sums[c] + jnp.sum(p, axis=0, keepdims=True)
                acc_ref[c] += jnp.dot(vtb, p.astype(BF16), preferred_element_type=F32)
            return tuple(tops) + tuple(sums)

        st_out = lax.fori_loop(0, n_full, anchored_step, tuple(ms) + tuple(ls))
        tops, ls = list(st_out[:n_chain]), list(st_out[n_chain:])
        unsafe = functools.reduce(
            jnp.logical_or, [jnp.any(tops[c] > ms[c] + EXP_HEADROOM) for c in range(n_chain)])

        def running_max_pass():
            m2, l2 = fresh_stats()

            def full_step(j, st_in):
                kb, vtb = kv_block(j)
                mm, ll = list(st_in[:n_chain]), list(st_in[n_chain:])
                for c in range(n_chain):
                    mm[c], ll[c] = chain_update(c, kb, vtb, mm[c], ll[c], None)
                return tuple(mm) + tuple(ll)

            st2 = lax.fori_loop(0, n_full, full_step, tuple(m2) + tuple(l2))
            _, l2 = diagonal(list(st2[:n_chain]), list(st2[n_chain:]))
            return tuple(l2)

        ls = lax.cond(unsafe, running_max_pass, lambda: tuple(ls))
        even_rows = lax.broadcasted_iota(jnp.int32, (PAIR, qs), 0) < HEAD_DIM
        for hf in range(n_half):
            ot = jnp.where(even_rows, acc_ref[2 * hf] / ls[2 * hf],
                           acc_ref[2 * hf + 1] / ls[2 * hf + 1])
            r = pl.multiple_of(row0 + hf * qs, qs)
            gate = g_ref[0, pl.ds(r, qs), :].astype(F32)
            out_ref[0, pl.ds(r, qs), :] = (ot.T * gate).astype(BF16)
        return carry

    lax.fori_loop(0, s_len // tq, q_tile, 0)


def _latent_attention(qn, qr, kcat, vt, gb):
    b, n_pair, s, _ = qn.shape
    n_chain = 2 * (MLA_Q_TILE // MLA_Q_SUB)
    pair_spec = lambda w: pl.BlockSpec((1, 1, s, w), lambda i, p: (i, p, 0, 0))
    vt_spec = pl.BlockSpec((1, 1, PAIR, s), lambda i, p: (i, p, 0, 0))
    lane_spec = pl.BlockSpec((1, s, PAIR), lambda i, p: (i, 0, p))
    return pl.pallas_call(
        _mla_kernel,
        grid=(b, n_pair),
        in_specs=[pair_spec(PAIR), pair_spec(2 * LANES), pair_spec(2 * LANES), vt_spec, lane_spec],
        out_specs=lane_spec,
        out_shape=jax.ShapeDtypeStruct((b, s, n_pair * PAIR), BF16),
        scratch_shapes=[pltpu.VMEM((n_chain, 2 * LANES, MLA_Q_SUB), BF16),
                        pltpu.VMEM((n_chain, PAIR, MLA_Q_SUB), F32)],
        compiler_params=pltpu.CompilerParams(
            dimension_semantics=("arbitrary", "arbitrary"), vmem_limit_bytes=VMEM_LIMIT),
        name="latent_attention",
    )(qn, qr, kcat, vt, gb)


def _out_kernel(ma_ref, mb_ref, x_ref, wa_ref, wb_ref, out_ref):
    acc = jnp.dot(ma_ref[0], wa_ref[...], preferred_element_type=F32)
    acc = acc + jnp.dot(mb_ref[0], wb_ref[...], preferred_element_type=F32)
    out_ref[0] = x_ref[0] + acc


def _output_projection(ma, mb, x, wa, wb):
    b, s, d = x.shape
    tm = ROW_TILE
    row_spec = lambda w: pl.BlockSpec((1, tm, w), lambda i, j: (i, j, 0))
    full = lambda a: pl.BlockSpec(a.shape, lambda i, j: (0,) * a.ndim)
    return pl.pallas_call(
        _out_kernel,
        grid=(b, s // tm),
        in_specs=[row_spec(ma.shape[-1]), row_spec(mb.shape[-1]), row_spec(d), full(wa), full(wb)],
        out_specs=row_spec(d),
        out_shape=jax.ShapeDtypeStruct((b, s, d), x.dtype),
        compiler_params=pltpu.CompilerParams(
            dimension_semantics=("arbitrary", "arbitrary"), vmem_limit_bytes=VMEM_LIMIT),
        name="output_projection",
    )(ma, mb, x, wa, wb)


def _t5_bucket(dist):
    max_exact = REL_BUCKETS // 2
    d = np.maximum(dist.astype(np.float32), np.float32(1.0))
    large = max_exact + (np.log(d / np.float32(max_exact)) / np.float32(
        math.log(REL_MAX_DISTANCE / max_exact)) * np.float32(REL_BUCKETS - max_exact)
                         ).astype(np.int32)
    large = np.minimum(large, REL_BUCKETS - 1)
    return np.where(dist < max_exact, dist, large)


def _dilated_bias(rel_bias, n_heads):
    tiles = []
    for dil in DILATIONS:
        n_chunk = MAX_DIL // dil
        csize = BLOCK // n_chunk
        idx = np.arange(BLOCK)
        true_idx = n_chunk * (idx % csize) + idx // csize
        qi = true_idx[:, None]
        ki = np.concatenate([true_idx, true_idx + BLOCK])[None, :]
        j = qi + BLOCK - ki
        valid = (j >= 0) & (j <= BLOCK)
        bucket = _t5_bucket(np.maximum(j, 0) * dil)
        bias = jnp.zeros((n_heads,) + bucket.shape, F32)
        for bkt in np.unique(bucket[valid]):
            bias = jnp.where((bucket == bkt)[None],
                             rel_bias[int(bkt)].astype(F32)[:, None, None], bias)
        bias = bias * LOG2E
        normal = jnp.where(valid[None], bias, NEG)
        first = jnp.where((valid & (ki >= BLOCK))[None], bias, NEG)
        both = jnp.stack([normal, first], axis=0)
        tiles.append(both.reshape(2, n_heads // 2, 2 * BLOCK, 2 * BLOCK))
    return jnp.stack(tiles, axis=0)


def _layer(x, pos3, rel_bias, norm_gain, w_in, a_q_gain, a_k_gain, q_c_gain, w_uq, kv_c_gain,
           w_ukv, qn_gain, qr_gain, kn_gain, kr_gain, w_out):
    d = x.shape[-1]
    n_heads = 8
    a_width = n_heads * HEAD_DIM
    row = lambda v: v.reshape(1, -1).astype(F32)
    pad_rope = lambda v: jnp.pad(v, (0, LANES - ROPE_DIM))

    splits = (a_width, a_width, a_width, a_width, Q_RANK, KV_RANK, ROPE_DIM, a_width)
    starts = np.concatenate([[0], np.cumsum(splits)])
    kr0, kr1 = int(starts[6]), int(starts[7])
    w_in_p = jnp.concatenate(
        [w_in[:, :kr1], jnp.zeros((d, LANES - ROPE_DIM), w_in.dtype), w_in[:, kr1:]],
        axis=1).astype(BF16)
    off = dict(qa=int(starts[0]), ka=int(starts[1]), va=int(starts[2]), za=int(starts[3]),
               cq=int(starts[4]), ckv=int(starts[5]), kr=kr0, zb=kr0 + LANES)

    qk_dim = HEAD_DIM + ROPE_DIM
    wq = w_uq.reshape(Q_RANK, n_heads, qk_dim)
    wq_rope = jnp.pad(wq[:, :, HEAD_DIM:], ((0, 0), (0, 0), (0, LANES - ROPE_DIM)))
    wuq_p = jnp.concatenate([wq[:, :, :HEAD_DIM].reshape(Q_RANK, -1),
                             wq_rope.reshape(Q_RANK, -1)], axis=1).astype(BF16)
    wkv = w_ukv.reshape(KV_RANK, n_heads, 2 * HEAD_DIM)
    wukv_p = jnp.concatenate([wkv[:, :, :HEAD_DIM].reshape(KV_RANK, -1),
                              wkv[:, :, HEAD_DIM:].reshape(KV_RANK, -1)], axis=1).astype(BF16)

    scale_a = HEAD_DIM ** -0.5
    scale_b = qk_dim ** -0.5
    inv_freq = ROPE_THETA ** (-jnp.arange(0, ROPE_DIM, 2, dtype=F32) / ROPE_DIM)
    freq = row(pad_rope(jnp.concatenate([inv_freq, inv_freq])))
    sgn = row(pad_rope(jnp.concatenate([-jnp.ones(ROPE_DIM // 2, F32),
                                        jnp.ones(ROPE_DIM // 2, F32)])))

    qa, ka, va, ga, qn, qr, kcat, vt, gb = _input_projection(
        x, pos3, row(norm_gain), w_in_p,
        row(jnp.tile(a_q_gain, 2) * (scale_a * LOG2E)), row(jnp.tile(a_k_gain, 2)),
        row(q_c_gain), wuq_p, row(kv_c_gain), wukv_p,
        row(jnp.tile(qn_gain, 2) * scale_b), row(pad_rope(qr_gain) * scale_b),
        row(jnp.tile(kn_gain, 2)), row(pad_rope(kr_gain)), freq, sgn, off)

    mixed_a = _dilated_attention(qa, ka, va, _dilated_bias(rel_bias, n_heads), ga)
    mixed_b = _latent_attention(qn, qr, kcat, vt, gb)
    w_out_b = w_out.astype(BF16)
    return _output_projection(mixed_a, mixed_b, x, w_out_b[:a_width], w_out_b[a_width:])


def kernel(x, positions, rel_bias, norm_gain, w_in, a_q_gain, a_k_gain, q_c_gain, w_uq, kv_c_gain,
           w_ukv, qn_gain, qr_gain, kn_gain, kr_gain, w_out):
    pos3 = positions[..., None]
    for l in range(norm_gain.shape[0]):
        x = _layer(x, pos3, rel_bias, norm_gain[l], w_in[l], a_q_gain[l], a_k_gain[l],
                   q_c_gain[l], w_uq[l], kv_c_gain[l], w_ukv[l], qn_gain[l], qr_gain[l],
                   kn_gain[l], kr_gain[l], w_out[l])
    return x
```

```python
import functools
import math

import jax
import jax.numpy as jnp
import numpy as np
from jax import lax
from jax.experimental import pallas as pl
from jax.experimental.pallas import tpu as pltpu

F32 = jnp.float32
BF16 = jnp.bfloat16

LANES = 128
HEAD_DIM = 64
PAIR = 2 * HEAD_DIM
ROPE_DIM = 32
KV_RANK = 128
Q_RANK = 256
BLOCK = 128
DILATIONS = (1, 4, 16)
MAX_DIL = 16
ROPE_THETA = 10000.0
REL_BUCKETS = 32
REL_MAX_DISTANCE = 2048
EPS = 1e-6
NEG = -1e30
SAFE_MIN, SAFE_MAX = 1e-30, 1e30
LOG2E = math.log2(math.e)

ROW_TILE = 512
MLA_Q_TILE = 1024
MLA_K_TILE = 512
MLA_Q_SUB = 256

VMEM_LIMIT = 56 * 1024 * 1024


def _nt_dot(a, b):
    return lax.dot_general(a, b, (((1,), (1,)), ((), ())), preferred_element_type=F32)


def _lo_mask(shape):
    return lax.broadcasted_iota(jnp.int32, shape, len(shape) - 1) < HEAD_DIM


def _pair_rms(t, gain_row):
    lo = _lo_mask(t.shape)
    sq = t * t
    s_lo = jnp.sum(jnp.where(lo, sq, 0.0), axis=1, keepdims=True)
    s_hi = jnp.sum(jnp.where(lo, 0.0, sq), axis=1, keepdims=True)
    r = jnp.where(lo, lax.rsqrt(s_lo * (1.0 / HEAD_DIM) + EPS),
                  lax.rsqrt(s_hi * (1.0 / HEAD_DIM) + EPS))
    return t * r * gain_row


def _silu(z):
    return z * (1.0 / (1.0 + jnp.exp(-z)))


def _proj_kernel(x_ref, pos_ref, ng_ref, w_in_ref, gq_ref, gk_ref, gcq_ref, wuq_ref,
                 gckv_ref, wukv_ref, gqn_ref, gqr_ref, gkn_ref, gkr_ref, freq_ref, sgn_ref,
                 qa_ref, ka_ref, va_ref, ga_ref, qn_ref, qr_ref, kcat_ref, vt_ref, gb_ref,
                 sq_ref, sk_ref, sv_ref, *, off):
    tm = x_ref.shape[1]
    sub = tm // MAX_DIL
    x = x_ref[0]
    h = x * lax.rsqrt(jnp.mean(x * x, axis=-1, keepdims=True) + EPS) * ng_ref[...]
    hb = h.astype(BF16)

    def seg(name, width):
        a = off[name]
        return jnp.dot(hb, w_in_ref[:, a:a + width], preferred_element_type=F32)

    n_pair = sq_ref.shape[0]

    def emit_a(t, gain_ref, scr_ref, out_ref):
        for p in range(n_pair):
            c = t[:, p * PAIR:(p + 1) * PAIR]
            if gain_ref is not None:
                c = _pair_rms(c, gain_ref[...])
            scr_ref[p] = c
        for p in range(n_pair):
            for r in range(MAX_DIL):
                out_ref[0, p, r] = scr_ref[p, pl.ds(r, sub, stride=MAX_DIL), :]

    emit_a(seg("qa", n_pair * PAIR), gq_ref, sq_ref, qa_ref)
    emit_a(seg("ka", n_pair * PAIR), gk_ref, sk_ref, ka_ref)
    emit_a(seg("va", n_pair * PAIR), None, sv_ref, va_ref)
    ga_ref[0] = _silu(seg("za", n_pair * PAIR)).astype(BF16)
    gb_ref[0] = _silu(seg("zb", n_pair * PAIR)).astype(BF16)

    ang = pos_ref[0].astype(F32) * freq_ref[...]
    cos_t = jnp.cos(ang)
    sin_t = jnp.sin(ang) * sgn_ref[...]
    lane = lax.broadcasted_iota(jnp.int32, (tm, LANES), 1)
    first_half = lane < (ROPE_DIM // 2)

    def rope(t):
        partner = jnp.where(first_half, pltpu.roll(t, LANES - ROPE_DIM // 2, 1),
                            pltpu.roll(t, ROPE_DIM // 2, 1))
        return t * cos_t + partner * sin_t

    def rope_rms(t, gain_row):
        ss = jnp.sum(t * t, axis=1, keepdims=True) * (1.0 / ROPE_DIM)
        return rope(t * lax.rsqrt(ss + EPS) * gain_row)

    cq = seg("cq", Q_RANK)
    cq = cq * lax.rsqrt(jnp.mean(cq * cq, axis=-1, keepdims=True) + EPS) * gcq_ref[...]
    q = jnp.dot(cq.astype(BF16), wuq_ref[...], preferred_element_type=F32)
    nope_w = n_pair * PAIR
    for p in range(n_pair):
        qn_ref[0, p] = _pair_rms(q[:, p * PAIR:(p + 1) * PAIR], gqn_ref[...]).astype(BF16)
    for hd in range(2 * n_pair):
        t = q[:, nope_w + hd * LANES:nope_w + (hd + 1) * LANES]
        qr_ref[0, hd // 2, :, (hd % 2) * LANES:(hd % 2 + 1) * LANES] = (
            rope_rms(t, gqr_ref[...]).astype(BF16))

    ckv = seg("ckv", KV_RANK)
    ckv = ckv * lax.rsqrt(jnp.mean(ckv * ckv, axis=-1, keepdims=True) + EPS) * gckv_ref[...]
    kv = jnp.dot(ckv.astype(BF16), wukv_ref[...], preferred_element_type=F32)
    kr = rope_rms(seg("kr", LANES), gkr_ref[...]).astype(BF16)
    for p in range(n_pair):
        kn = _pair_rms(kv[:, p * PAIR:(p + 1) * PAIR], gkn_ref[...])
        kcat_ref[0, p, :, 0:LANES] = kn.astype(BF16)
        kcat_ref[0, p, :, LANES:2 * LANES] = kr
        vt_ref[0, p] = kv[:, nope_w + p * PAIR:nope_w + (p + 1) * PAIR].T.astype(BF16)


def _input_projection(x, pos3, ng, w_in_p, gq, gk, gcq, wuq_p, gckv, wukv_p, gqn, gqr, gkn, gkr,
                      freq, sgn, off):
    b, s, d = x.shape
    tm = ROW_TILE
    n_pair = 4
    sub = tm // MAX_DIL
    grid = (b, s // tm)

    def full(a):
        return pl.BlockSpec(a.shape, lambda i, j: (0,) * a.ndim)

    a_shape = jax.ShapeDtypeStruct((b, n_pair, MAX_DIL, s // MAX_DIL, PAIR), F32)
    a_spec = pl.BlockSpec((1, n_pair, MAX_DIL, sub, PAIR), lambda i, j: (i, 0, 0, j, 0))
    row_spec = lambda w: pl.BlockSpec((1, tm, w), lambda i, j: (i, j, 0))
    pair_spec = lambda w: pl.BlockSpec((1, n_pair, tm, w), lambda i, j: (i, 0, j, 0))

    out_shape = (
        a_shape, a_shape, a_shape,
        jax.ShapeDtypeStruct((b, s, n_pair * PAIR), BF16),
        jax.ShapeDtypeStruct((b, n_pair, s, PAIR), BF16),
        jax.ShapeDtypeStruct((b, n_pair, s, 2 * LANES), BF16),
        jax.ShapeDtypeStruct((b, n_pair, s, 2 * LANES), BF16),
        jax.ShapeDtypeStruct((b, n_pair, PAIR, s), BF16),
        jax.ShapeDtypeStruct((b, s, n_pair * PAIR), BF16),
    )
    out_specs = (
        a_spec, a_spec, a_spec,
        row_spec(n_pair * PAIR),
        pair_spec(PAIR), pair_spec(2 * LANES), pair_spec(2 * LANES),
        pl.BlockSpec((1, n_pair, PAIR, tm), lambda i, j: (i, 0, 0, j)),
        row_spec(n_pair * PAIR),
    )
    in_arrays = (x, pos3, ng, w_in_p, gq, gk, gcq, wuq_p, gckv, wukv_p, gqn, gqr, gkn, gkr,
                 freq, sgn)
    in_specs = [row_spec(d), row_spec(1)] + [full(a) for a in in_arrays[2:]]
    scr = pltpu.VMEM((n_pair, tm, PAIR), F32)
    return pl.pallas_call(
        functools.partial(_proj_kernel, off=off),
        grid=grid,
        in_specs=in_specs,
        out_specs=out_specs,
        out_shape=out_shape,
        scratch_shapes=[scr, scr, scr],
        compiler_params=pltpu.CompilerParams(
            dimension_semantics=("arbitrary", "arbitrary"), vmem_limit_bytes=VMEM_LIMIT),
        name="input_projection",
    )(*in_arrays)


DILATED_GROUP = 4


def _dilated_schedule(pat, i, g):
    if pat == 0:
        return 0, DILATED_GROUP * i + g
    if pat == 1:
        return g, i
    return 2 * i + g // 2, g % 2


def _dilated_kernel(q_ref, k_ref, v_ref, bias_ref, g_ref, out_ref, obuf, lbuf, mbuf, xo_ref):
    n_res = q_ref.shape[2]
    rows = q_ref.shape[3]
    lo = _lo_mask((BLOCK, PAIR))
    ones = jnp.ones((2 * BLOCK, LANES), BF16)

    def pattern_geometry(pat):
        dil = DILATIONS[pat]
        n_chunk = MAX_DIL // dil
        csize = BLOCK // n_chunk
        return dil, n_chunk, csize

    def load_block(pat, stream, n):
        dil, n_chunk, csize = pattern_geometry(pat)
        chunks = [(stream + dil * c, n * csize) for c in range(n_chunk)]
        first = n == 0

        def gather(ref, shift):
            parts = [ref[0, 0, r, pl.ds(pl.multiple_of(st - shift, 8), csize), :]
                     for r, st in chunks]
            return parts[0] if len(parts) == 1 else jnp.concatenate(parts, axis=0)

        prev_shift = jnp.where(first, 0, csize)
        qc = gather(q_ref, 0)
        lhs = jnp.concatenate([jnp.where(lo, qc, 0.0), jnp.where(lo, 0.0, qc)],
                              axis=0).astype(BF16)
        kk = jnp.concatenate([gather(k_ref, prev_shift), gather(k_ref, 0)], axis=0).astype(BF16)
        vv = jnp.concatenate([gather(v_ref, prev_shift), gather(v_ref, 0)], axis=0).astype(BF16)
        vv1 = jnp.concatenate([vv, ones], axis=1)
        bias = bias_ref[pat, jnp.where(first, 1, 0), 0]
        return chunks, csize, lhs, kk, vv1, bias

    def split_heads(r):
        return jnp.where(lo, r[:BLOCK], r[BLOCK:])

    def store_chunks(buf, lead, chunks, csize, val):
        for c, (r, st) in enumerate(chunks):
            buf[lead + (r, pl.ds(pl.multiple_of(st, 8), csize), slice(None))] = (
                val[c * csize:(c + 1) * csize])

    def fast_body(i, carry):
        slots = [(pat, g) for g in range(DILATED_GROUP) for pat in range(len(DILATIONS))]
        blocks = [load_block(pat, *_dilated_schedule(pat, i, g)) for pat, g in slots]
        ahead = 3
        scores = [_nt_dot(b[2], b[3]) for b in blocks[:ahead]]
        for k, (chunks, csize, _, _, vv1, bias) in enumerate(blocks):
            if k + ahead < len(blocks):
                nb = blocks[k + ahead]
                scores.append(_nt_dot(nb[2], nb[3]))
            p = jnp.exp2(scores[k] + bias).astype(BF16)
            r = jnp.dot(p, vv1, preferred_element_type=F32)
            lead = (slots[k][0],)
            store_chunks(obuf, lead, chunks, csize, split_heads(r[:, :LANES]))
            store_chunks(lbuf, lead, chunks, csize, split_heads(r[:, LANES:]))
        return carry

    lax.fori_loop(0, 32 // DILATED_GROUP, fast_body, 0)

    def pattern_sum(r, stats):
        num = obuf[0, r] + obuf[1, r] + obuf[2, r]
        den = lbuf[0, r] + lbuf[1, r] + lbuf[2, r]
        obuf[0, r] = num
        lbuf[0, r] = den
        den_lo, den_hi, num_hi = stats
        return (jnp.minimum(den_lo, jnp.min(den, axis=0, keepdims=True)),
                jnp.maximum(den_hi, jnp.max(den, axis=0, keepdims=True)),
                jnp.maximum(num_hi, jnp.max(jnp.abs(num), axis=0, keepdims=True)))

    row = lambda v: jnp.full((1, PAIR), v, F32)
    den_lo, den_hi, num_hi = lax.fori_loop(0, n_res, pattern_sum, (row(1.0), row(1.0), row(0.0)))
    unsafe = jnp.logical_not((jnp.min(den_lo) >= SAFE_MIN) & (jnp.max(den_hi) <= SAFE_MAX)
                             & (jnp.max(num_hi) <= SAFE_MAX))

    @pl.when(unsafe)
    def _():
        for pat in range(len(DILATIONS)):
            def body(i, carry, pat=pat):
                n_blocks = 32 // DILATIONS[pat]
                chunks, csize, lhs, kk, vv1, bias = load_block(pat, i // n_blocks, i % n_blocks)
                t = _nt_dot(lhs, kk) + bias
                m = jnp.max(t, axis=1, keepdims=True)
                p = jnp.exp2(t - m).astype(BF16)
                r = jnp.dot(p, vv1, preferred_element_type=F32)
                num_blk, den_blk = split_heads(r[:, :LANES]), split_heads(r[:, LANES:])
                m2 = jnp.where(lo, m[:BLOCK], m[BLOCK:])
                if pat == 0:
                    store_chunks(obuf, (0,), chunks, csize, num_blk)
                    store_chunks(lbuf, (0,), chunks, csize, den_blk)
                    store_chunks(mbuf, (), chunks, csize, m2)
                else:
                    for c, (r16, st) in enumerate(chunks):
                        rows_c = pl.ds(pl.multiple_of(st, 8), csize)
                        piece = slice(c * csize, (c + 1) * csize)
                        m_old = mbuf[r16, rows_c, :]
                        m_new = jnp.maximum(m_old, m2[piece])
                        a, b = jnp.exp2(m_old - m_new), jnp.exp2(m2[piece] - m_new)
                        obuf[0, r16, rows_c, :] = a * obuf[0, r16, rows_c, :] + b * num_blk[piece]
                        lbuf[0, r16, rows_c, :] = a * lbuf[0, r16, rows_c, :] + b * den_blk[piece]
                        mbuf[r16, rows_c, :] = m_new
                return carry

            lax.fori_loop(0, 32, body, 0)

    def merge(r, carry):
        xo_ref[pl.ds(r, rows, stride=n_res), :] = obuf[0, r] / lbuf[0, r]
        return carry

    lax.fori_loop(0, n_res, merge, 0)
    out_ref[0] = (xo_ref[...] * g_ref[0].astype(F32)).astype(BF16)


def _dilated_attention(qa, ka, va, bias, ga):
    b, n_pair, n_res, rows, _ = qa.shape
    s = n_res * rows
    a_spec = pl.BlockSpec((1, 1, n_res, rows, PAIR), lambda i, p: (i, p, 0, 0, 0))
    lane_spec = pl.BlockSpec((1, s, PAIR), lambda i, p: (i, 0, p))
    bias_spec = pl.BlockSpec((len(DILATIONS), 2, 1, 2 * BLOCK, 2 * BLOCK),
                             lambda i, p: (0, 0, p, 0, 0))
    buf = pltpu.VMEM((n_res, rows, PAIR), F32)
    pat_buf = pltpu.VMEM((len(DILATIONS), n_res, rows, PAIR), F32)
    return pl.pallas_call(
        _dilated_kernel,
        grid=(b, n_pair),
        in_specs=[a_spec, a_spec, a_spec, bias_spec, lane_spec],
        out_specs=lane_spec,
        out_shape=jax.ShapeDtypeStruct((b, s, n_pair * PAIR), BF16),
        scratch_shapes=[pat_buf, pat_buf, buf, pltpu.VMEM((s, PAIR), F32)],
        compiler_params=pltpu.CompilerParams(
            dimension_semantics=("arbitrary", "arbitrary"), vmem_limit_bytes=VMEM_LIMIT),
        name="dilated_attention",
    )(qa, ka, va, bias, ga)


def _mla_kernel(qn_ref, qr_ref, k_ref, vt_ref, g_ref, out_ref, qt_ref, acc_ref, l_ref):
    tq, tk, qs = MLA_Q_TILE, MLA_K_TILE, MLA_Q_SUB
    n_sub = tq // qs
    n_chain = 2 * n_sub
    s_len = k_ref.shape[2]
    lo_q = _lo_mask((qs, PAIR))

    def scores(c, kb):
        return jnp.dot(kb, qt_ref[c], preferred_element_type=F32)

    def causal(st, key_offset):
        key = lax.broadcasted_iota(jnp.int32, st.shape, 0) + key_offset
        qry = lax.broadcasted_iota(jnp.int32, st.shape, 1)
        return jnp.where(key <= qry, st, NEG)

    def run_jobs(jobs, ls, assign):
        ahead = 2
        sts = [scores(c, kb) for c, kb, _, _ in jobs[:ahead]]
        for n, (c, kb, vtb, key_offset) in enumerate(jobs):
            if n + ahead < len(jobs):
                nc, nkb, _, _ = jobs[n + ahead]
                sts.append(scores(nc, nkb))
            st = sts[n] if key_offset is None else causal(sts[n], key_offset)
            p = jnp.exp2(st)
            col_sum = jnp.sum(p, axis=0, keepdims=True)
            pv = jnp.dot(vtb, p.astype(BF16), preferred_element_type=F32)
            if c in assign:
                assign.remove(c)
                ls[c] = col_sum
                acc_ref[c] = pv
            else:
                ls[c] = ls[c] + col_sum
                acc_ref[c] += pv
        return ls

    def q_tile(qi, carry):
        row0 = pl.multiple_of(qi * tq, tq)
        for h in range(n_sub):
            r = pl.multiple_of(row0 + h * qs, qs)
            qn = qn_ref[0, 0, pl.ds(r, qs), :].astype(F32)
            qr = qr_ref[0, 0, pl.ds(r, qs), :].astype(F32)
            q_even = jnp.concatenate([jnp.where(lo_q, qn, 0.0), qr[:, :LANES]], axis=1)
            q_odd = jnp.concatenate([jnp.where(lo_q, 0.0, qn), qr[:, LANES:]], axis=1)
            qt_ref[2 * h] = q_even.T.astype(BF16)
            qt_ref[2 * h + 1] = q_odd.T.astype(BF16)

        def kv_block(start, size):
            start = pl.multiple_of(start, qs)
            return k_ref[0, 0, pl.ds(start, size), :], vt_ref[0, 0, :, pl.ds(start, size)]

        jobs = []
        for h in range(n_sub):
            for k0 in range(0, (h + 1) * qs, tk):
                size = min(tk, (h + 1) * qs - k0)
                kb, vtb = kv_block(row0 + k0, size)
                offset = k0 - h * qs if k0 + size > h * qs else None
                jobs += [(2 * h, kb, vtb, offset), (2 * h + 1, kb, vtb, offset)]
        ls = run_jobs(jobs, [None] * n_chain, set(range(n_chain)))

        tiles_per_step = tq // tk

        def full_step(j, ls_in):
            jobs = []
            for t in range(tiles_per_step):
                kb, vtb = kv_block((j * tiles_per_step + t) * tk, tk)
                jobs += [(c, kb, vtb, None) for c in range(n_chain)]
            return tuple(run_jobs(jobs, list(ls_in), set()))

        n_full = qi * tiles_per_step
        ls = lax.fori_loop(0, qi, full_step, tuple(ls))
        for c in range(n_chain):
            l_ref[c] = jnp.broadcast_to(ls[c], (8, qs))
        den_lo = functools.reduce(jnp.minimum, [jnp.min(l) for l in ls])
        den_hi = functools.reduce(jnp.maximum, [jnp.max(l) for l in ls])
        num_hi = jnp.max(jnp.abs(acc_ref[...]))
        unsafe = jnp.logical_not((den_lo >= SAFE_MIN) & (den_hi <= SAFE_MAX) & (num_hi <= SAFE_MAX))

        @pl.when(unsafe)
        def _():
            def chain(c, carry_c):
                q_start = (c // 2) * qs

                def kv_step(j, st_in):
                    m, l = st_in
                    kb, vtb = kv_block(j * tk, tk)
                    st = causal(scores(c, kb), j * tk - row0 - q_start)
                    m_new = jnp.maximum(m, jnp.max(st, axis=0, keepdims=True))
                    alpha = jnp.exp2(m - m_new)
                    p = jnp.exp2(st - m_new)
                    pv = jnp.dot(vtb, p.astype(BF16), preferred_element_type=F32)
                    acc_ref[c] = alpha * acc_ref[c] + pv
                    return m_new, alpha * l + jnp.sum(p, axis=0, keepdims=True)

                acc_ref[c] = jnp.zeros((PAIR, qs), F32)
                init = (jnp.full((1, qs), NEG, F32), jnp.zeros((1, qs), F32))
                _, l = lax.fori_loop(0, n_full + tq // tk, kv_step, init)
                l_ref[c] = jnp.broadcast_to(l, (8, qs))
                return carry_c

            lax.fori_loop(0, n_chain, chain, 0)

        even_rows = lax.broadcasted_iota(jnp.int32, (PAIR, qs), 0) < HEAD_DIM
        for h in range(n_sub):
            ot = jnp.where(even_rows, acc_ref[2 * h] / l_ref[2 * h, 0:1, :],
                           acc_ref[2 * h + 1] / l_ref[2 * h + 1, 0:1, :])
            r = pl.multiple_of(row0 + h * qs, qs)
            gate = g_ref[0, pl.ds(r, qs), :].astype(F32)
            out_ref[0, pl.ds(r, qs), :] = (ot.T * gate).astype(BF16)
        return carry

    lax.fori_loop(0, s_len // tq, q_tile, 0)


def _latent_attention(qn, qr, kcat, vt, gb):
    b, n_pair, s, _ = qn.shape
    n_chain = 2 * (MLA_Q_TILE // MLA_Q_SUB)
    pair_spec = lambda w: pl.BlockSpec((1, 1, s, w), lambda i, p: (i, p, 0, 0))
    vt_spec = pl.BlockSpec((1, 1, PAIR, s), lambda i, p: (i, p, 0, 0))
    lane_spec = pl.BlockSpec((1, s, PAIR), lambda i, p: (i, 0, p))
    return pl.pallas_call(
        _mla_kernel,
        grid=(b, n_pair),
        in_specs=[pair_spec(PAIR), pair_spec(2 * LANES), pair_spec(2 * LANES), vt_spec, lane_spec],
        out_specs=lane_spec,
        out_shape=jax.ShapeDtypeStruct((b, s, n_pair * PAIR), BF16),
        scratch_shapes=[pltpu.VMEM((n_chain, 2 * LANES, MLA_Q_SUB), BF16),
                        pltpu.VMEM((n_chain, PAIR, MLA_Q_SUB), F32),
                        pltpu.VMEM((n_chain, 8, MLA_Q_SUB), F32)],
        compiler_params=pltpu.CompilerParams(
            dimension_semantics=("arbitrary", "arbitrary"), vmem_limit_bytes=VMEM_LIMIT),
        name="latent_attention",
    )(qn, qr, kcat, vt, gb)


def _out_kernel(ma_ref, mb_ref, x_ref, wa_ref, wb_ref, out_ref):
    acc = jnp.dot(ma_ref[0], wa_ref[...], preferred_element_type=F32)
    acc = acc + jnp.dot(mb_ref[0], wb_ref[...], preferred_element_type=F32)
    out_ref[0] = x_ref[0] + acc


def _output_projection(ma, mb, x, wa, wb):
    b, s, d = x.shape
    tm = ROW_TILE
    row_spec = lambda w: pl.BlockSpec((1, tm, w), lambda i, j: (i, j, 0))
    full = lambda a: pl.BlockSpec(a.shape, lambda i, j: (0,) * a.ndim)
    return pl.pallas_call(
        _out_kernel,
        grid=(b, s // tm),
        in_specs=[row_spec(ma.shape[-1]), row_spec(mb.shape[-1]), row_spec(d), full(wa), full(wb)],
        out_specs=row_spec(d),
        out_shape=jax.ShapeDtypeStruct((b, s, d), x.dtype),
        compiler_params=pltpu.CompilerParams(
            dimension_semantics=("arbitrary", "arbitrary"), vmem_limit_bytes=VMEM_LIMIT),
        name="output_projection",
    )(ma, mb, x, wa, wb)


def _t5_bucket(dist):
    max_exact = REL_BUCKETS // 2
    d = np.maximum(dist.astype(np.float32), np.float32(1.0))
    large = max_exact + (np.log(d / np.float32(max_exact)) / np.float32(
        math.log(REL_MAX_DISTANCE / max_exact)) * np.float32(REL_BUCKETS - max_exact)
                         ).astype(np.int32)
    large = np.minimum(large, REL_BUCKETS - 1)
    return np.where(dist < max_exact, dist, large)


def _dilated_bias(rel_bias, n_heads):
    tiles = []
    for dil in DILATIONS:
        n_chunk = MAX_DIL // dil
        csize = BLOCK // n_chunk
        idx = np.arange(BLOCK)
        true_idx = n_chunk * (idx % csize) + idx // csize
        qi = true_idx[:, None]
        ki = np.concatenate([true_idx, true_idx + BLOCK])[None, :]
        j = qi + BLOCK - ki
        valid = (j >= 0) & (j <= BLOCK)
        bucket = _t5_bucket(np.maximum(j, 0) * dil)
        bias = jnp.zeros((n_heads,) + bucket.shape, F32)
        for bkt in np.unique(bucket[valid]):
            bias = jnp.where((bucket == bkt)[None],
                             rel_bias[int(bkt)].astype(F32)[:, None, None], bias)
        bias = bias * LOG2E
        normal = jnp.where(valid[None], bias, NEG)
        first = jnp.where((valid & (ki >= BLOCK))[None], bias, NEG)
        both = jnp.stack([normal, first], axis=0)
        tiles.append(both.reshape(2, n_heads // 2, 2 * BLOCK, 2 * BLOCK))
    return jnp.stack(tiles, axis=0)


def _layer(x, pos3, rel_bias, norm_gain, w_in, a_q_gain, a_k_gain, q_c_gain, w_uq, kv_c_gain,
           w_ukv, qn_gain, qr_gain, kn_gain, kr_gain, w_out):
    d = x.shape[-1]
    n_heads = 8
    a_width = n_heads * HEAD_DIM
    row = lambda v: v.reshape(1, -1).astype(F32)
    pad_rope = lambda v: jnp.pad(v, (0, LANES - ROPE_DIM))

    splits = (a_width, a_width, a_width, a_width, Q_RANK, KV_RANK, ROPE_DIM, a_width)
    starts = np.concatenate([[0], np.cumsum(splits)])
    kr0, kr1 = int(starts[6]), int(starts[7])
    w_in_p = jnp.concatenate(
        [w_in[:, :kr1], jnp.zeros((d, LANES - ROPE_DIM), w_in.dtype), w_in[:, kr1:]],
        axis=1).astype(BF16)
    off = dict(qa=int(starts[0]), ka=int(starts[1]), va=int(starts[2]), za=int(starts[3]),
               cq=int(starts[4]), ckv=int(starts[5]), kr=kr0, zb=kr0 + LANES)

    qk_dim = HEAD_DIM + ROPE_DIM
    wq = w_uq.reshape(Q_RANK, n_heads, qk_dim)
    wq_rope = jnp.pad(wq[:, :, HEAD_DIM:], ((0, 0), (0, 0), (0, LANES - ROPE_DIM)))
    wuq_p = jnp.concatenate([wq[:, :, :HEAD_DIM].reshape(Q_RANK, -1),
                             wq_rope.reshape(Q_RANK, -1)], axis=1).astype(BF16)
    wkv = w_ukv.reshape(KV_RANK, n_heads, 2 * HEAD_DIM)
    wukv_p = jnp.concatenate([wkv[:, :, :HEAD_DIM].reshape(KV_RANK, -1),
                              wkv[:, :, HEAD_DIM:].reshape(KV_RANK, -1)], axis=1).astype(BF16)

    scale_a = HEAD_DIM ** -0.5
    scale_b = qk_dim ** -0.5
    inv_freq = ROPE_THETA ** (-jnp.arange(0, ROPE_DIM, 2, dtype=F32) / ROPE_DIM)
    freq = row(pad_rope(jnp.concatenate([inv_freq, inv_freq])))
    sgn = row(pad_rope(jnp.concatenate([-jnp.ones(ROPE_DIM // 2, F32),
                                        jnp.ones(ROPE_DIM // 2, F32)])))

    qa, ka, va, ga, qn, qr, kcat, vt, gb = _input_projection(
        x, pos3, row(norm_gain), w_in_p,
        row(jnp.tile(a_q_gain, 2) * (scale_a * LOG2E)), row(jnp.tile(a_k_gain, 2)),
        row(q_c_gain), wuq_p, row(kv_c_gain), wukv_p,
        row(jnp.tile(qn_gain, 2) * (scale_b * LOG2E)), row(pad_rope(qr_gain) * (scale_b * LOG2E)),
        row(jnp.tile(kn_gain, 2)), row(pad_rope(kr_gain)), freq, sgn, off)

    mixed_a = _dilated_attention(qa, ka, va, _dilated_bias(rel_bias, n_heads), ga)
    mixed_b = _latent_attention(qn, qr, kcat, vt, gb)
    w_out_b = w_out.astype(BF16)
    return _output_projection(mixed_a, mixed_b, x, w_out_b[:a_width], w_out_b[a_width:])


def kernel(x, positions, rel_bias, norm_gain, w_in, a_q_gain, a_k_gain, q_c_gain, w_uq, kv_c_gain,
           w_ukv, qn_gain, qr_gain, kn_gain, kr_gain, w_out):
    pos3 = positions[..., None]
    for l in range(norm_gain.shape[0]):
        x = _layer(x, pos3, rel_bias, norm_gain[l], w_in[l], a_q_gain[l], a_k_gain[l],
                   q_c_gain[l], w_uq[l], kv_c_gain[l], w_ukv[l], qn_gain[l], qr_gain[l],
                   kn_gain[l], kr_gain[l], w_out[l])
    return x
```

```python
import functools
import math

import jax
import jax.numpy as jnp
import numpy as np
from jax import lax
from jax.experimental import pallas as pl
from jax.experimental.pallas import tpu as pltpu

F32 = jnp.float32
BF16 = jnp.bfloat16

LANES = 128
HEAD_DIM = 64
PAIR = 2 * HEAD_DIM
ROPE_DIM = 32
KV_RANK = 128
Q_RANK = 256
BLOCK = 128
DILATIONS = (1, 4, 16)
MAX_DIL = 16
ROPE_THETA = 10000.0
REL_BUCKETS = 32
REL_MAX_DISTANCE = 2048
EPS = 1e-6
NEG = -1e30
SAFE_MIN, SAFE_MAX = 1e-30, 1e30
LOG2E = math.log2(math.e)

ROW_TILE = 512
MLA_Q_TILE = 1024
MLA_K_TILE = 512
MLA_Q_SUB = 256

VMEM_LIMIT = 56 * 1024 * 1024


def _nt_dot(a, b):
    return lax.dot_general(a, b, (((1,), (1,)), ((), ())), preferred_element_type=F32)


def _lo_mask(shape):
    return lax.broadcasted_iota(jnp.int32, shape, len(shape) - 1) < HEAD_DIM


def _silu(z):
    return z * (1.0 / (1.0 + jnp.exp(-z)))


def _group_rms(t, gain_row, group):
    lane_group = lax.broadcasted_iota(jnp.int32, t.shape, 1) // group
    sq = t * t
    ss = None
    for g in range(LANES // group):
        in_g = lane_group == g
        s = jnp.sum(jnp.where(in_g, sq, 0.0), axis=1, keepdims=True)
        ss = s if ss is None else jnp.where(in_g, s, ss)
    return t * lax.rsqrt(ss + group * EPS) * gain_row


def _proj_kernel(x_ref, *rest, off, n_res):
    xres_refs = rest[:n_res]
    (pos_ref, ng_ref, w_in_ref, gq_ref, gk_ref, gcq_ref, wuq_ref, gckv_ref, wukv_ref, gqn_ref,
     gqr_ref, gkn_ref, gkr_ref, freq_ref, spread_ref, sgn_ref,
     qa_ref, ka_ref, va_ref, ga_ref, qn_ref, qr_ref, kcat_ref, vt_ref, gb_ref) = rest[n_res:]
    tm = x_ref.shape[1]
    sub = tm // n_res
    n_pair = qa_ref.shape[1]
    a_width = n_pair * PAIR

    def normed(x):
        return (x * lax.rsqrt(jnp.mean(x * x, axis=-1, keepdims=True) + EPS)
                * ng_ref[...]).astype(BF16)

    h_res = normed(jnp.concatenate([r[0] for r in xres_refs], axis=0))

    def seg(h, name, width):
        a = off[name]
        return jnp.dot(h, w_in_ref[:, a:a + width], preferred_element_type=F32)

    def emit_a(t, gain_ref, out_ref):
        for p in range(n_pair):
            c = t[:, p * PAIR:(p + 1) * PAIR]
            if gain_ref is not None:
                c = _group_rms(c, gain_ref[...], HEAD_DIM)
            for r in range(n_res):
                out_ref[0, p, r] = c[r * sub:(r + 1) * sub]

    ang = freq_ref[...] * pos_ref[0].astype(F32)

    def spread(table):
        hi = table.astype(BF16)
        lo = (table - hi.astype(F32)).astype(BF16)
        tn = lambda a: lax.dot_general(a, spread_ref[...], (((0,), (0,)), ((), ())),
                                       preferred_element_type=F32)
        return tn(hi) + tn(lo)

    cos_t = spread(jnp.cos(ang))
    sin_t = spread(jnp.sin(ang)) * sgn_ref[...]
    lane = lax.broadcasted_iota(jnp.int32, (tm, LANES), 1)
    first_half = (lane & (ROPE_DIM - 1)) < (ROPE_DIM // 2)

    def rope(t):
        partner = jnp.where(first_half, pltpu.roll(t, LANES - ROPE_DIM // 2, 1),
                            pltpu.roll(t, ROPE_DIM // 2, 1))
        return t * cos_t + partner * sin_t

    def rms(t, gain_ref):
        return t * lax.rsqrt(jnp.mean(t * t, axis=-1, keepdims=True) + EPS) * gain_ref[...]

    s_qa = seg(h_res, "qa", a_width)
    h_nat = normed(x_ref[0])
    s_ka = seg(h_res, "ka", a_width)
    emit_a(s_qa, gq_ref, qa_ref)
    s_cq = seg(h_nat, "cq", Q_RANK)
    s_ckv = seg(h_nat, "ckv", KV_RANK)
    s_kr = seg(h_nat, "kr", LANES)
    emit_a(s_ka, gk_ref, ka_ref)
    q = jnp.dot(rms(s_cq, gcq_ref).astype(BF16), wuq_ref[...], preferred_element_type=F32)
    kv = jnp.dot(rms(s_ckv, gckv_ref).astype(BF16), wukv_ref[...], preferred_element_type=F32)
    s_za = seg(h_nat, "za", a_width)

    for p in range(n_pair):
        qn_ref[0, p] = _group_rms(q[:, p * PAIR:(p + 1) * PAIR], gqn_ref[...],
                                  HEAD_DIM).astype(BF16)
    for t in range(qr_ref.shape[2] // LANES):
        tile = q[:, a_width + t * LANES:a_width + (t + 1) * LANES]
        qr_ref[0, :, t * LANES:(t + 1) * LANES] = rope(
            _group_rms(tile, gqr_ref[...], ROPE_DIM)).astype(BF16)
    s_zb = seg(h_nat, "zb", a_width)

    kr = rope(rms(s_kr, gkr_ref)).astype(BF16)
    for p in range(n_pair):
        kn = _group_rms(kv[:, p * PAIR:(p + 1) * PAIR], gkn_ref[...], HEAD_DIM)
        kcat_ref[0, p, :, 0:LANES] = kn.astype(BF16)
        kcat_ref[0, p, :, LANES:2 * LANES] = kr
        vt_ref[0, p] = kv[:, a_width + p * PAIR:a_width + (p + 1) * PAIR].T.astype(BF16)

    ga_ref[0] = _silu(s_za).astype(BF16)
    s_va = seg(h_res, "va", a_width)
    gb_ref[0] = _silu(s_zb).astype(BF16)
    emit_a(s_va, None, va_ref)


def _input_projection(x, pos_row, ng, w_in_p, gq, gk, gcq, wuq_p, gckv, wukv_p, gqn, gqr, gkn, gkr,
                      freq, spread, sgn, off):
    b, s, d = x.shape
    tm = ROW_TILE
    n_pair = 4
    n_res = MAX_DIL
    sub = tm // n_res
    grid = (b, s // tm)

    def full(a):
        return pl.BlockSpec(a.shape, lambda i, j: (0,) * a.ndim)

    a_shape = jax.ShapeDtypeStruct((b, n_pair, n_res, s // n_res, PAIR), F32)
    a_spec = pl.BlockSpec((1, n_pair, n_res, sub, PAIR), lambda i, j: (i, 0, 0, j, 0))
    row_spec = lambda w: pl.BlockSpec((1, tm, w), lambda i, j: (i, j, 0))
    pair_spec = lambda w: pl.BlockSpec((1, n_pair, tm, w), lambda i, j: (i, 0, j, 0))
    x_res = x.reshape(b, s // n_res, n_res * d)
    res_specs = [pl.BlockSpec((1, sub, d), functools.partial(lambda i, j, r: (i, j, r), r=r))
                 for r in range(n_res)]

    out_shape = (
        a_shape, a_shape, a_shape,
        jax.ShapeDtypeStruct((b, s, n_pair * PAIR), BF16),
        jax.ShapeDtypeStruct((b, n_pair, s, PAIR), BF16),
        jax.ShapeDtypeStruct((b, s, 2 * LANES), BF16),
        jax.ShapeDtypeStruct((b, n_pair, s, 2 * LANES), BF16),
        jax.ShapeDtypeStruct((b, n_pair, PAIR, s), BF16),
        jax.ShapeDtypeStruct((b, s, n_pair * PAIR), BF16),
    )
    out_specs = (
        a_spec, a_spec, a_spec,
        row_spec(n_pair * PAIR),
        pair_spec(PAIR), row_spec(2 * LANES), pair_spec(2 * LANES),
        pl.BlockSpec((1, n_pair, PAIR, tm), lambda i, j: (i, 0, 0, j)),
        row_spec(n_pair * PAIR),
    )
    params = (ng, w_in_p, gq, gk, gcq, wuq_p, gckv, wukv_p, gqn, gqr, gkn, gkr, freq, spread, sgn)
    pos_spec = pl.BlockSpec((1, 1, tm), lambda i, j: (i, 0, j))
    in_specs = [row_spec(d)] + res_specs + [pos_spec] + [full(a) for a in params]
    return pl.pallas_call(
        functools.partial(_proj_kernel, off=off, n_res=n_res),
        grid=grid,
        in_specs=in_specs,
        out_specs=out_specs,
        out_shape=out_shape,
        compiler_params=pltpu.CompilerParams(
            dimension_semantics=("arbitrary", "arbitrary"), vmem_limit_bytes=VMEM_LIMIT),
        name="input_projection",
    )(x, *([x_res] * n_res), pos_row, *params)


DILATED_GROUP = 4


def _dilated_schedule(pat, i, g):
    if pat == 0:
        return 0, DILATED_GROUP * i + g
    if pat == 1:
        return g, i
    return 2 * i + g // 2, g % 2


def _dilated_kernel(q_ref, k_ref, v_ref, bias_ref, g_ref, out_ref, obuf, lbuf, mbuf, xo_ref):
    n_res = q_ref.shape[2]
    rows = q_ref.shape[3]
    lo = _lo_mask((BLOCK, PAIR))
    ones = jnp.ones((2 * BLOCK, LANES), BF16)

    def pattern_geometry(pat):
        dil = DILATIONS[pat]
        n_chunk = MAX_DIL // dil
        csize = BLOCK // n_chunk
        return dil, n_chunk, csize

    def load_block(pat, stream, n):
        dil, n_chunk, csize = pattern_geometry(pat)
        chunks = [(stream + dil * c, n * csize) for c in range(n_chunk)]
        first = n == 0

        def gather(ref, shift):
            parts = [ref[0, 0, r, pl.ds(pl.multiple_of(st - shift, 8), csize), :]
                     for r, st in chunks]
            return parts[0] if len(parts) == 1 else jnp.concatenate(parts, axis=0)

        prev_shift = jnp.where(first, 0, csize)
        qc = gather(q_ref, 0)
        lhs = jnp.concatenate([jnp.where(lo, qc, 0.0), jnp.where(lo, 0.0, qc)],
                              axis=0).astype(BF16)
        kk = jnp.concatenate([gather(k_ref, prev_shift), gather(k_ref, 0)], axis=0).astype(BF16)
        vv = jnp.concatenate([gather(v_ref, prev_shift), gather(v_ref, 0)], axis=0).astype(BF16)
        vv1 = jnp.concatenate([vv, ones], axis=1)
        bias = bias_ref[pat, jnp.where(first, 1, 0), 0]
        return chunks, csize, lhs, kk, vv1, bias

    def split_heads(r):
        return jnp.where(lo, r[:BLOCK], r[BLOCK:])

    def store_chunks(buf, lead, chunks, csize, val):
        for c, (r, st) in enumerate(chunks):
            buf[lead + (r, pl.ds(pl.multiple_of(st, 8), csize), slice(None))] = (
                val[c * csize:(c + 1) * csize])

    def fast_body(i, carry):
        slots = [(pat, g) for g in range(DILATED_GROUP) for pat in range(len(DILATIONS))]
        blocks = [load_block(pat, *_dilated_schedule(pat, i, g)) for pat, g in slots]
        ahead = 3
        scores = [_nt_dot(b[2], b[3]) for b in blocks[:ahead]]
        for k, (chunks, csize, _, _, vv1, bias) in enumerate(blocks):
            if k + ahead < len(blocks):
                nb = blocks[k + ahead]
                scores.append(_nt_dot(nb[2], nb[3]))
            p = jnp.exp2(scores[k] + bias).astype(BF16)
            r = jnp.dot(p, vv1, preferred_element_type=F32)
            lead = (slots[k][0],)
            store_chunks(obuf, lead, chunks, csize, split_heads(r[:, :LANES]))
            store_chunks(lbuf, lead, chunks, csize, split_heads(r[:, LANES:]))
        return carry

    lax.fori_loop(0, 32 // DILATED_GROUP, fast_body, 0)

    def pattern_sum(r, stats):
        num = obuf[0, r] + obuf[1, r] + obuf[2, r]
        den = lbuf[0, r] + lbuf[1, r] + lbuf[2, r]
        obuf[0, r] = num
        lbuf[0, r] = den
        den_lo, den_hi, num_hi = stats
        return (jnp.minimum(den_lo, jnp.min(den, axis=0, keepdims=True)),
                jnp.maximum(den_hi, jnp.max(den, axis=0, keepdims=True)),
                jnp.maximum(num_hi, jnp.max(jnp.abs(num), axis=0, keepdims=True)))

    row = lambda v: jnp.full((1, PAIR), v, F32)
    den_lo, den_hi, num_hi = lax.fori_loop(0, n_res, pattern_sum, (row(1.0), row(1.0), row(0.0)))
    unsafe = jnp.logical_not((jnp.min(den_lo) >= SAFE_MIN) & (jnp.max(den_hi) <= SAFE_MAX)
                             & (jnp.max(num_hi) <= SAFE_MAX))

    @pl.when(unsafe)
    def _():
        for pat in range(len(DILATIONS)):
            def body(i, carry, pat=pat):
                n_blocks = 32 // DILATIONS[pat]
                chunks, csize, lhs, kk, vv1, bias = load_block(pat, i // n_blocks, i % n_blocks)
                t = _nt_dot(lhs, kk) + bias
                m = jnp.max(t, axis=1, keepdims=True)
                p = jnp.exp2(t - m).astype(BF16)
                r = jnp.dot(p, vv1, preferred_element_type=F32)
                num_blk, den_blk = split_heads(r[:, :LANES]), split_heads(r[:, LANES:])
                m2 = jnp.where(lo, m[:BLOCK], m[BLOCK:])
                if pat == 0:
                    store_chunks(obuf, (0,), chunks, csize, num_blk)
                    store_chunks(lbuf, (0,), chunks, csize, den_blk)
                    store_chunks(mbuf, (), chunks, csize, m2)
                else:
                    for c, (r16, st) in enumerate(chunks):
                        rows_c = pl.ds(pl.multiple_of(st, 8), csize)
                        piece = slice(c * csize, (c + 1) * csize)
                        m_old = mbuf[r16, rows_c, :]
                        m_new = jnp.maximum(m_old, m2[piece])
                        a, b = jnp.exp2(m_old - m_new), jnp.exp2(m2[piece] - m_new)
                        obuf[0, r16, rows_c, :] = a * obuf[0, r16, rows_c, :] + b * num_blk[piece]
                        lbuf[0, r16, rows_c, :] = a * lbuf[0, r16, rows_c, :] + b * den_blk[piece]
                        mbuf[r16, rows_c, :] = m_new
                return carry

            lax.fori_loop(0, 32, body, 0)

    def merge(r, carry):
        xo_ref[pl.ds(r, rows, stride=n_res), :] = obuf[0, r] / lbuf[0, r]
        return carry

    lax.fori_loop(0, n_res, merge, 0)
    out_ref[0] = (xo_ref[...] * g_ref[0].astype(F32)).astype(BF16)


def _dilated_attention(qa, ka, va, bias, ga):
    b, n_pair, n_res, rows, _ = qa.shape
    s = n_res * rows
    a_spec = pl.BlockSpec((1, 1, n_res, rows, PAIR), lambda i, p: (i, p, 0, 0, 0))
    lane_spec = pl.BlockSpec((1, s, PAIR), lambda i, p: (i, 0, p))
    bias_spec = pl.BlockSpec((len(DILATIONS), 2, 1, 2 * BLOCK, 2 * BLOCK),
                             lambda i, p: (0, 0, p, 0, 0))
    buf = pltpu.VMEM((n_res, rows, PAIR), F32)
    pat_buf = pltpu.VMEM((len(DILATIONS), n_res, rows, PAIR), F32)
    return pl.pallas_call(
        _dilated_kernel,
        grid=(b, n_pair),
        in_specs=[a_spec, a_spec, a_spec, bias_spec, lane_spec],
        out_specs=lane_spec,
        out_shape=jax.ShapeDtypeStruct((b, s, n_pair * PAIR), BF16),
        scratch_shapes=[pat_buf, pat_buf, buf, pltpu.VMEM((s, PAIR), F32)],
        compiler_params=pltpu.CompilerParams(
            dimension_semantics=("arbitrary", "arbitrary"), vmem_limit_bytes=VMEM_LIMIT),
        name="dilated_attention",
    )(qa, ka, va, bias, ga)


def _mla_kernel(qn_ref, qr_ref, k_ref, vt_ref, g_ref, out_ref, qt_ref, acc_ref, l_ref):
    tq, tk, qs = MLA_Q_TILE, MLA_K_TILE, MLA_Q_SUB
    n_sub = tq // qs
    n_chain = 2 * n_sub
    s_len = k_ref.shape[2]
    lo_q = _lo_mask((qs, PAIR))
    rope_head = lax.broadcasted_iota(jnp.int32, (qs, LANES), 1) // ROPE_DIM
    even_head = (pl.program_id(1) % 2) * 2

    def scores(c, kb):
        return jnp.dot(kb, qt_ref[c], preferred_element_type=F32)

    def causal(st, key_offset):
        key = lax.broadcasted_iota(jnp.int32, st.shape, 0) + key_offset
        qry = lax.broadcasted_iota(jnp.int32, st.shape, 1)
        return jnp.where(key <= qry, st, NEG)

    def run_jobs(jobs, ls, assign):
        ahead = 2
        sts = [scores(c, kb) for c, kb, _, _ in jobs[:ahead]]
        for n, (c, kb, vtb, key_offset) in enumerate(jobs):
            if n + ahead < len(jobs):
                nc, nkb, _, _ = jobs[n + ahead]
                sts.append(scores(nc, nkb))
            st = sts[n] if key_offset is None else causal(sts[n], key_offset)
            p = jnp.exp2(st)
            col_sum = jnp.sum(p, axis=0, keepdims=True)
            pv = jnp.dot(vtb, p.astype(BF16), preferred_element_type=F32)
            if c in assign:
                assign.remove(c)
                ls[c] = col_sum
                acc_ref[c] = pv
            else:
                ls[c] = ls[c] + col_sum
                acc_ref[c] += pv
        return ls

    def q_tile(qi, carry):
        row0 = pl.multiple_of(qi * tq, tq)
        for h in range(n_sub):
            r = pl.multiple_of(row0 + h * qs, qs)
            qn = qn_ref[0, 0, pl.ds(r, qs), :].astype(F32)
            qr = qr_ref[0, pl.ds(r, qs), :].astype(F32)
            q_even = jnp.concatenate(
                [jnp.where(lo_q, qn, 0.0), jnp.where(rope_head == even_head, qr, 0.0)], axis=1)
            q_odd = jnp.concatenate(
                [jnp.where(lo_q, 0.0, qn), jnp.where(rope_head == even_head + 1, qr, 0.0)], axis=1)
            qt_ref[2 * h] = q_even.T.astype(BF16)
            qt_ref[2 * h + 1] = q_odd.T.astype(BF16)

        def kv_block(start, size):
            start = pl.multiple_of(start, qs)
            return k_ref[0, 0, pl.ds(start, size), :], vt_ref[0, 0, :, pl.ds(start, size)]

        jobs = []
        for h in range(n_sub):
            for k0 in range(0, (h + 1) * qs, tk):
                size = min(tk, (h + 1) * qs - k0)
                kb, vtb = kv_block(row0 + k0, size)
                offset = k0 - h * qs if k0 + size > h * qs else None
                jobs += [(2 * h, kb, vtb, offset), (2 * h + 1, kb, vtb, offset)]
        ls = run_jobs(jobs, [None] * n_chain, set(range(n_chain)))

        tiles_per_step = tq // tk

        def full_step(j, ls_in):
            jobs = []
            for t in range(tiles_per_step):
                kb, vtb = kv_block((j * tiles_per_step + t) * tk, tk)
                jobs += [(c, kb, vtb, None) for c in range(n_chain)]
            return tuple(run_jobs(jobs, list(ls_in), set()))

        n_full = qi * tiles_per_step
        ls = lax.fori_loop(0, qi, full_step, tuple(ls))
        for c in range(n_chain):
            l_ref[c] = jnp.broadcast_to(ls[c], (8, qs))
        den_lo = functools.reduce(jnp.minimum, [jnp.min(l) for l in ls])
        den_hi = functools.reduce(jnp.maximum, [jnp.max(l) for l in ls])
        num_hi = jnp.max(jnp.abs(acc_ref[...]))
        unsafe = jnp.logical_not((den_lo >= SAFE_MIN) & (den_hi <= SAFE_MAX) & (num_hi <= SAFE_MAX))

        @pl.when(unsafe)
        def _():
            def chain(c, carry_c):
                q_start = (c // 2) * qs

                def kv_step(j, st_in):
                    m, l = st_in
                    kb, vtb = kv_block(j * tk, tk)
                    st = causal(scores(c, kb), j * tk - row0 - q_start)
                    m_new = jnp.maximum(m, jnp.max(st, axis=0, keepdims=True))
                    alpha = jnp.exp2(m - m_new)
                    p = jnp.exp2(st - m_new)
                    pv = jnp.dot(vtb, p.astype(BF16), preferred_element_type=F32)
                    acc_ref[c] = alpha * acc_ref[c] + pv
                    return m_new, alpha * l + jnp.sum(p, axis=0, keepdims=True)

                acc_ref[c] = jnp.zeros((PAIR, qs), F32)
                init = (jnp.full((1, qs), NEG, F32), jnp.zeros((1, qs), F32))
                _, l = lax.fori_loop(0, n_full + tq // tk, kv_step, init)
                l_ref[c] = jnp.broadcast_to(l, (8, qs))
                return carry_c

            lax.fori_loop(0, n_chain, chain, 0)

        even_rows = lax.broadcasted_iota(jnp.int32, (PAIR, qs), 0) < HEAD_DIM
        for h in range(n_sub):
            ot = jnp.where(even_rows, acc_ref[2 * h] / l_ref[2 * h, 0:1, :],
                           acc_ref[2 * h + 1] / l_ref[2 * h + 1, 0:1, :])
            r = pl.multiple_of(row0 + h * qs, qs)
            gate = g_ref[0, pl.ds(r, qs), :].astype(F32)
            out_ref[0, pl.ds(r, qs), :] = (ot.T * gate).astype(BF16)
        return carry

    lax.fori_loop(0, s_len // tq, q_tile, 0)


def _latent_attention(qn, qr, kcat, vt, gb):
    b, n_pair, s, _ = qn.shape
    n_chain = 2 * (MLA_Q_TILE // MLA_Q_SUB)
    pair_spec = lambda w: pl.BlockSpec((1, 1, s, w), lambda i, p: (i, p, 0, 0))
    vt_spec = pl.BlockSpec((1, 1, PAIR, s), lambda i, p: (i, p, 0, 0))
    lane_spec = pl.BlockSpec((1, s, PAIR), lambda i, p: (i, 0, p))
    return pl.pallas_call(
        _mla_kernel,
        grid=(b, n_pair),
        in_specs=[pair_spec(PAIR), pl.BlockSpec((1, s, LANES), lambda i, p: (i, 0, p // 2)),
                  pair_spec(2 * LANES), vt_spec, lane_spec],
        out_specs=lane_spec,
        out_shape=jax.ShapeDtypeStruct((b, s, n_pair * PAIR), BF16),
        scratch_shapes=[pltpu.VMEM((n_chain, 2 * LANES, MLA_Q_SUB), BF16),
                        pltpu.VMEM((n_chain, PAIR, MLA_Q_SUB), F32),
                        pltpu.VMEM((n_chain, 8, MLA_Q_SUB), F32)],
        compiler_params=pltpu.CompilerParams(
            dimension_semantics=("arbitrary", "arbitrary"), vmem_limit_bytes=VMEM_LIMIT),
        name="latent_attention",
    )(qn, qr, kcat, vt, gb)


def _out_kernel(ma_ref, mb_ref, x_ref, wa_ref, wb_ref, out_ref):
    acc = jnp.dot(ma_ref[0], wa_ref[...], preferred_element_type=F32)
    acc = acc + jnp.dot(mb_ref[0], wb_ref[...], preferred_element_type=F32)
    out_ref[0] = x_ref[0] + acc


def _output_projection(ma, mb, x, wa, wb):
    b, s, d = x.shape
    tm = ROW_TILE
    row_spec = lambda w: pl.BlockSpec((1, tm, w), lambda i, j: (i, j, 0))
    full = lambda a: pl.BlockSpec(a.shape, lambda i, j: (0,) * a.ndim)
    return pl.pallas_call(
        _out_kernel,
        grid=(b, s // tm),
        in_specs=[row_spec(ma.shape[-1]), row_spec(mb.shape[-1]), row_spec(d), full(wa), full(wb)],
        out_specs=row_spec(d),
        out_shape=jax.ShapeDtypeStruct((b, s, d), x.dtype),
        compiler_params=pltpu.CompilerParams(
            dimension_semantics=("arbitrary", "arbitrary"), vmem_limit_bytes=VMEM_LIMIT),
        name="output_projection",
    )(ma, mb, x, wa, wb)


def _t5_bucket(dist):
    max_exact = REL_BUCKETS // 2
    d = np.maximum(dist.astype(np.float32), np.float32(1.0))
    large = max_exact + (np.log(d / np.float32(max_exact)) / np.float32(
        math.log(REL_MAX_DISTANCE / max_exact)) * np.float32(REL_BUCKETS - max_exact)
                         ).astype(np.int32)
    large = np.minimum(large, REL_BUCKETS - 1)
    return np.where(dist < max_exact, dist, large)


def _dilated_bias(rel_bias, n_heads):
    tiles = []
    for dil in DILATIONS:
        n_chunk = MAX_DIL // dil
        csize = BLOCK // n_chunk
        idx = np.arange(BLOCK)
        true_idx = n_chunk * (idx % csize) + idx // csize
        qi = true_idx[:, None]
        ki = np.concatenate([true_idx, true_idx + BLOCK])[None, :]
        j = qi + BLOCK - ki
        valid = (j >= 0) & (j <= BLOCK)
        bucket = _t5_bucket(np.maximum(j, 0) * dil)
        bias = jnp.zeros((n_heads,) + bucket.shape, F32)
        for bkt in np.unique(bucket[valid]):
            bias = jnp.where((bucket == bkt)[None],
                             rel_bias[int(bkt)].astype(F32)[:, None, None], bias)
        bias = bias * LOG2E
        normal = jnp.where(valid[None], bias, NEG)
        first = jnp.where((valid & (ki >= BLOCK))[None], bias, NEG)
        both = jnp.stack([normal, first], axis=0)
        tiles.append(both.reshape(2, n_heads // 2, 2 * BLOCK, 2 * BLOCK))
    return jnp.stack(tiles, axis=0)


def _layer(x, pos_row, rel_bias, norm_gain, w_in, a_q_gain, a_k_gain, q_c_gain, w_uq, kv_c_gain,
           w_ukv, qn_gain, qr_gain, kn_gain, kr_gain, w_out):
    d = x.shape[-1]
    n_heads = 8
    a_width = n_heads * HEAD_DIM
    row = lambda v: v.reshape(1, -1).astype(F32)

    splits = (a_width, a_width, a_width, a_width, Q_RANK, KV_RANK, ROPE_DIM, a_width)
    starts = np.concatenate([[0], np.cumsum(splits)])
    kr0, kr1 = int(starts[6]), int(starts[7])
    rope_rep = LANES // ROPE_DIM
    w_in_p = jnp.concatenate(
        [w_in[:, :kr0], jnp.tile(w_in[:, kr0:kr1], (1, rope_rep)), w_in[:, kr1:]],
        axis=1).astype(BF16)
    off = dict(qa=int(starts[0]), ka=int(starts[1]), va=int(starts[2]), za=int(starts[3]),
               cq=int(starts[4]), ckv=int(starts[5]), kr=kr0, zb=kr0 + LANES)

    qk_dim = HEAD_DIM + ROPE_DIM
    wq = w_uq.reshape(Q_RANK, n_heads, qk_dim)
    wuq_p = jnp.concatenate([wq[:, :, :HEAD_DIM].reshape(Q_RANK, -1),
                             wq[:, :, HEAD_DIM:].reshape(Q_RANK, -1)], axis=1).astype(BF16)
    wkv = w_ukv.reshape(KV_RANK, n_heads, 2 * HEAD_DIM)
    wukv_p = jnp.concatenate([wkv[:, :, :HEAD_DIM].reshape(KV_RANK, -1),
                              wkv[:, :, HEAD_DIM:].reshape(KV_RANK, -1)], axis=1).astype(BF16)

    scale_a = HEAD_DIM ** -0.5
    scale_b = qk_dim ** -0.5
    inv_freq = ROPE_THETA ** (-jnp.arange(0, ROPE_DIM, 2, dtype=F32) / ROPE_DIM)
    rope_row = lambda v: row(jnp.tile(v, rope_rep))
    half = ROPE_DIM // 2
    freq = inv_freq.reshape(half, 1)
    spread = (np.arange(LANES)[None, :] % half == np.arange(half)[:, None]).astype(np.float32)
    spread = jnp.asarray(spread, BF16)
    sgn = rope_row(jnp.concatenate([-jnp.ones(half, F32), jnp.ones(half, F32)]))
    head_norm, rope_norm = HEAD_DIM ** 0.5, ROPE_DIM ** 0.5

    qa, ka, va, ga, qn, qr, kcat, vt, gb = _input_projection(
        x, pos_row, row(norm_gain), w_in_p,
        row(jnp.tile(a_q_gain, 2) * (head_norm * scale_a * LOG2E)),
        row(jnp.tile(a_k_gain, 2) * head_norm),
        row(q_c_gain), wuq_p, row(kv_c_gain), wukv_p,
        row(jnp.tile(qn_gain, 2) * (head_norm * scale_b * LOG2E)),
        rope_row(qr_gain * (rope_norm * scale_b * LOG2E)),
        row(jnp.tile(kn_gain, 2) * head_norm), rope_row(kr_gain), freq, spread, sgn, off)

    mixed_a = _dilated_attention(qa, ka, va, _dilated_bias(rel_bias, n_heads), ga)
    mixed_b = _latent_attention(qn, qr, kcat, vt, gb)
    w_out_b = w_out.astype(BF16)
    return _output_projection(mixed_a, mixed_b, x, w_out_b[:a_width], w_out_b[a_width:])


def kernel(x, positions, rel_bias, norm_gain, w_in, a_q_gain, a_k_gain, q_c_gain, w_uq, kv_c_gain,
           w_ukv, qn_gain, qr_gain, kn_gain, kr_gain, w_out):
    pos_row = positions[:, None, :]
    for l in range(norm_gain.shape[0]):
        x = _layer(x, pos_row, rel_bias, norm_gain[l], w_in[l], a_q_gain[l], a_k_gain[l],
                   q_c_gain[l], w_uq[l], kv_c_gain[l], w_ukv[l], qn_gain[l], qr_gain[l],
                   kn_gain[l], kr_gain[l], w_out[l])
    return x
```

```python
import functools
import math

import jax
import jax.numpy as jnp
import numpy as np
from jax import lax
from jax.experimental import pallas as pl
from jax.experimental.pallas import tpu as pltpu

F32 = jnp.float32
BF16 = jnp.bfloat16

LANES = 128
HEAD_DIM = 64
PAIR = 2 * HEAD_DIM
ROPE_DIM = 32
KV_RANK = 128
Q_RANK = 256
BLOCK = 128
DILATIONS = (1, 4, 16)
MAX_DIL = 16
DEINTERLEAVE = 4
ROPE_THETA = 10000.0
REL_BUCKETS = 32
REL_MAX_DISTANCE = 2048
EPS = 1e-6
NEG = -1e30
SAFE_MIN, SAFE_MAX = 1e-30, 1e30
LOG2E = math.log2(math.e)

ROW_TILE = 512
MLA_Q_TILE = 1024
MLA_K_TILE = 512
MLA_Q_SUB = 256

VMEM_LIMIT = 56 * 1024 * 1024


def _nt_dot(a, b):
    return lax.dot_general(a, b, (((1,), (1,)), ((), ())), preferred_element_type=F32)


def _lo_mask(shape):
    return lax.broadcasted_iota(jnp.int32, shape, len(shape) - 1) < HEAD_DIM


def _silu(z):
    return z * (1.0 / (1.0 + jnp.exp(-z)))


def _group_rms(t, gain_row, group):
    lane_group = lax.broadcasted_iota(jnp.int32, t.shape, 1) // group
    sq = t * t
    ss = None
    for g in range(LANES // group):
        in_g = lane_group == g
        s = jnp.sum(jnp.where(in_g, sq, 0.0), axis=1, keepdims=True)
        ss = s if ss is None else jnp.where(in_g, s, ss)
    return t * lax.rsqrt(ss + group * EPS) * gain_row


def _proj_kernel(x_ref, pos_ref, ng_ref, w_in_ref, gq_ref, gk_ref, gcq_ref, wuq_ref, gckv_ref,
                 wukv_ref, gqn_ref, gqr_ref, gkn_ref, gkr_ref, freq_ref, spread_ref, sgn_ref,
                 qa_ref, ka_ref, va_ref, ga_ref, qn_ref, qr_ref, kcat_ref, vt_ref, gb_ref,
                 *stage_refs, off):
    n_res = qa_ref.shape[2]
    tm = x_ref.shape[1]
    sub = tm // n_res
    n_pair = qa_ref.shape[1]
    a_width = n_pair * PAIR

    x = x_ref[0]
    h = (x * lax.rsqrt(jnp.mean(x * x, axis=-1, keepdims=True) + EPS) * ng_ref[...]).astype(BF16)

    def seg(name, width):
        a = off[name]
        return jnp.dot(h, w_in_ref[:, a:a + width], preferred_element_type=F32)

    def emit_a(t, gain_ref, out_ref, rows_ref, quad_ref):
        quarter = tm // DEINTERLEAVE
        for p in range(n_pair):
            c = t[:, p * PAIR:(p + 1) * PAIR]
            if gain_ref is not None:
                c = _group_rms(c, gain_ref[...], HEAD_DIM)
            rows_ref[p] = c
            for g in range(DEINTERLEAVE):
                quad_ref[p, g * quarter:(g + 1) * quarter] = (
                    rows_ref[p, pl.ds(g, quarter, stride=DEINTERLEAVE), :])
            for r in range(n_res):
                g, h = r % DEINTERLEAVE, r // DEINTERLEAVE
                out_ref[0, p, r] = quad_ref[p, pl.ds(g * quarter + h, sub, stride=DEINTERLEAVE), :]

    ang = freq_ref[...] * pos_ref[0].astype(F32)

    def spread(table):
        hi = table.astype(BF16)
        lo = (table - hi.astype(F32)).astype(BF16)
        tn = lambda a: lax.dot_general(a, spread_ref[...], (((0,), (0,)), ((), ())),
                                       preferred_element_type=F32)
        return tn(hi) + tn(lo)

    cos_t = spread(jnp.cos(ang))
    sin_t = spread(jnp.sin(ang)) * sgn_ref[...]
    lane = lax.broadcasted_iota(jnp.int32, (tm, LANES), 1)
    first_half = (lane & (ROPE_DIM - 1)) < (ROPE_DIM // 2)

    def rope(t):
        partner = jnp.where(first_half, pltpu.roll(t, LANES - ROPE_DIM // 2, 1),
                            pltpu.roll(t, ROPE_DIM // 2, 1))
        return t * cos_t + partner * sin_t

    def rms(t, gain_ref):
        return t * lax.rsqrt(jnp.mean(t * t, axis=-1, keepdims=True) + EPS) * gain_ref[...]

    s_qa = seg("qa", a_width)
    s_ka = seg("ka", a_width)
    emit_a(s_qa, gq_ref, qa_ref, *stage_refs[0:2])
    s_cq = seg("cq", Q_RANK)
    s_ckv = seg("ckv", KV_RANK)
    s_kr = seg("kr", LANES)
    emit_a(s_ka, gk_ref, ka_ref, *stage_refs[2:4])
    q = jnp.dot(rms(s_cq, gcq_ref).astype(BF16), wuq_ref[...], preferred_element_type=F32)
    kv = jnp.dot(rms(s_ckv, gckv_ref).astype(BF16), wukv_ref[...], preferred_element_type=F32)
    s_za = seg("za", a_width)

    for p in range(n_pair):
        qn_ref[0, p] = _group_rms(q[:, p * PAIR:(p + 1) * PAIR], gqn_ref[...],
                                  HEAD_DIM).astype(BF16)
    for t in range(qr_ref.shape[2] // LANES):
        tile = q[:, a_width + t * LANES:a_width + (t + 1) * LANES]
        qr_ref[0, :, t * LANES:(t + 1) * LANES] = rope(
            _group_rms(tile, gqr_ref[...], ROPE_DIM)).astype(BF16)
    s_zb = seg("zb", a_width)

    kr = rope(rms(s_kr, gkr_ref)).astype(BF16)
    for p in range(n_pair):
        kn = _group_rms(kv[:, p * PAIR:(p + 1) * PAIR], gkn_ref[...], HEAD_DIM)
        kcat_ref[0, p, :, 0:LANES] = kn.astype(BF16)
        kcat_ref[0, p, :, LANES:2 * LANES] = kr
        vt_ref[0, p] = kv[:, a_width + p * PAIR:a_width + (p + 1) * PAIR].T.astype(BF16)

    ga_ref[0] = _silu(s_za).astype(BF16)
    s_va = seg("va", a_width)
    gb_ref[0] = _silu(s_zb).astype(BF16)
    emit_a(s_va, None, va_ref, *stage_refs[4:6])


def _input_projection(x, pos_row, ng, w_in_p, gq, gk, gcq, wuq_p, gckv, wukv_p, gqn, gqr, gkn, gkr,
                      freq, spread, sgn, off):
    b, s, d = x.shape
    tm = ROW_TILE
    n_pair = 4
    n_res = MAX_DIL
    sub = tm // n_res
    grid = (b, s // tm)

    def full(a):
        return pl.BlockSpec(a.shape, lambda i, j: (0,) * a.ndim)

    a_shape = jax.ShapeDtypeStruct((b, n_pair, n_res, s // n_res, PAIR), F32)
    a_spec = pl.BlockSpec((1, n_pair, n_res, sub, PAIR), lambda i, j: (i, 0, 0, j, 0))
    row_spec = lambda w: pl.BlockSpec((1, tm, w), lambda i, j: (i, j, 0))
    pair_spec = lambda w: pl.BlockSpec((1, n_pair, tm, w), lambda i, j: (i, 0, j, 0))

    out_shape = (
        a_shape, a_shape, a_shape,
        jax.ShapeDtypeStruct((b, s, n_pair * PAIR), BF16),
        jax.ShapeDtypeStruct((b, n_pair, s, PAIR), BF16),
        jax.ShapeDtypeStruct((b, s, 2 * LANES), BF16),
        jax.ShapeDtypeStruct((b, n_pair, s, 2 * LANES), BF16),
        jax.ShapeDtypeStruct((b, n_pair, PAIR, s), BF16),
        jax.ShapeDtypeStruct((b, s, n_pair * PAIR), BF16),
    )
    out_specs = (
        a_spec, a_spec, a_spec,
        row_spec(n_pair * PAIR),
        pair_spec(PAIR), row_spec(2 * LANES), pair_spec(2 * LANES),
        pl.BlockSpec((1, n_pair, PAIR, tm), lambda i, j: (i, 0, 0, j)),
        row_spec(n_pair * PAIR),
    )
    params = (ng, w_in_p, gq, gk, gcq, wuq_p, gckv, wukv_p, gqn, gqr, gkn, gkr, freq, spread, sgn)
    pos_spec = pl.BlockSpec((1, 1, tm), lambda i, j: (i, 0, j))
    in_specs = [row_spec(d), pos_spec] + [full(a) for a in params]
    stage = pltpu.VMEM((n_pair, tm, PAIR), F32)
    return pl.pallas_call(
        functools.partial(_proj_kernel, off=off),
        grid=grid,
        in_specs=in_specs,
        out_specs=out_specs,
        out_shape=out_shape,
        scratch_shapes=[stage] * 6,
        compiler_params=pltpu.CompilerParams(
            dimension_semantics=("arbitrary", "arbitrary"), vmem_limit_bytes=VMEM_LIMIT),
        name="input_projection",
    )(x, pos_row, *params)


DILATED_GROUP = 4


def _dilated_schedule(pat, i, g):
    if pat == 0:
        return 0, DILATED_GROUP * i + g
    if pat == 1:
        return g, i
    return 2 * i + g // 2, g % 2


def _dilated_kernel(q_ref, k_ref, v_ref, bias_ref, g_ref, out_ref, obuf, lbuf, mbuf, xo_ref):
    n_res = q_ref.shape[2]
    rows = q_ref.shape[3]
    lo = _lo_mask((BLOCK, PAIR))
    ones = jnp.ones((2 * BLOCK, LANES), BF16)

    def pattern_geometry(pat):
        dil = DILATIONS[pat]
        n_chunk = MAX_DIL // dil
        csize = BLOCK // n_chunk
        return dil, n_chunk, csize

    def load_block(pat, stream, n):
        dil, n_chunk, csize = pattern_geometry(pat)
        chunks = [(stream + dil * c, n * csize) for c in range(n_chunk)]
        first = n == 0

        def gather(ref, shift):
            parts = [ref[0, 0, r, pl.ds(pl.multiple_of(st - shift, 8), csize), :]
                     for r, st in chunks]
            return parts[0] if len(parts) == 1 else jnp.concatenate(parts, axis=0)

        prev_shift = jnp.where(first, 0, csize)
        qc = gather(q_ref, 0)
        lhs = jnp.concatenate([jnp.where(lo, qc, 0.0), jnp.where(lo, 0.0, qc)],
                              axis=0).astype(BF16)
        kk = jnp.concatenate([gather(k_ref, prev_shift), gather(k_ref, 0)], axis=0).astype(BF16)
        vv = jnp.concatenate([gather(v_ref, prev_shift), gather(v_ref, 0)], axis=0).astype(BF16)
        vv1 = jnp.concatenate([vv, ones], axis=1)
        bias = bias_ref[pat, jnp.where(first, 1, 0), 0]
        return chunks, csize, lhs, kk, vv1, bias

    def split_heads(r):
        return jnp.where(lo, r[:BLOCK], r[BLOCK:])

    def store_chunks(buf, lead, chunks, csize, val):
        for c, (r, st) in enumerate(chunks):
            buf[lead + (r, pl.ds(pl.multiple_of(st, 8), csize), slice(None))] = (
                val[c * csize:(c + 1) * csize])

    def fast_body(i, carry):
        slots = [(pat, g) for g in range(DILATED_GROUP) for pat in range(len(DILATIONS))]
        blocks = [load_block(pat, *_dilated_schedule(pat, i, g)) for pat, g in slots]
        ahead = 3
        scores = [_nt_dot(b[2], b[3]) for b in blocks[:ahead]]
        for k, (chunks, csize, _, _, vv1, bias) in enumerate(blocks):
            if k + ahead < len(blocks):
                nb = blocks[k + ahead]
                scores.append(_nt_dot(nb[2], nb[3]))
            p = jnp.exp2(scores[k] + bias).astype(BF16)
            r = jnp.dot(p, vv1, preferred_element_type=F32)
            lead = (slots[k][0],)
            store_chunks(obuf, lead, chunks, csize, split_heads(r[:, :LANES]))
            store_chunks(lbuf, lead, chunks, csize, split_heads(r[:, LANES:]))
        return carry

    lax.fori_loop(0, 32 // DILATED_GROUP, fast_body, 0)

    def pattern_sum(r, stats):
        num = obuf[0, r] + obuf[1, r] + obuf[2, r]
        den = lbuf[0, r] + lbuf[1, r] + lbuf[2, r]
        obuf[0, r] = num
        lbuf[0, r] = den
        den_lo, den_hi, num_hi = stats
        return (jnp.minimum(den_lo, jnp.min(den, axis=0, keepdims=True)),
                jnp.maximum(den_hi, jnp.max(den, axis=0, keepdims=True)),
                jnp.maximum(num_hi, jnp.max(jnp.abs(num), axis=0, keepdims=True)))

    row = lambda v: jnp.full((1, PAIR), v, F32)
    den_lo, den_hi, num_hi = lax.fori_loop(0, n_res, pattern_sum, (row(1.0), row(1.0), row(0.0)))
    unsafe = jnp.logical_not((jnp.min(den_lo) >= SAFE_MIN) & (jnp.max(den_hi) <= SAFE_MAX)
                             & (jnp.max(num_hi) <= SAFE_MAX))

    @pl.when(unsafe)
    def _():
        for pat in range(len(DILATIONS)):
            def body(i, carry, pat=pat):
                n_blocks = 32 // DILATIONS[pat]
                chunks, csize, lhs, kk, vv1, bias = load_block(pat, i // n_blocks, i % n_blocks)
                t = _nt_dot(lhs, kk) + bias
                m = jnp.max(t, axis=1, keepdims=True)
                p = jnp.exp2(t - m).astype(BF16)
                r = jnp.dot(p, vv1, preferred_element_type=F32)
                num_blk, den_blk = split_heads(r[:, :LANES]), split_heads(r[:, LANES:])
                m2 = jnp.where(lo, m[:BLOCK], m[BLOCK:])
                if pat == 0:
                    store_chunks(obuf, (0,), chunks, csize, num_blk)
                    store_chunks(lbuf, (0,), chunks, csize, den_blk)
                    store_chunks(mbuf, (), chunks, csize, m2)
                else:
                    for c, (r16, st) in enumerate(chunks):
                        rows_c = pl.ds(pl.multiple_of(st, 8), csize)
                        piece = slice(c * csize, (c + 1) * csize)
                        m_old = mbuf[r16, rows_c, :]
                        m_new = jnp.maximum(m_old, m2[piece])
                        a, b = jnp.exp2(m_old - m_new), jnp.exp2(m2[piece] - m_new)
                        obuf[0, r16, rows_c, :] = a * obuf[0, r16, rows_c, :] + b * num_blk[piece]
                        lbuf[0, r16, rows_c, :] = a * lbuf[0, r16, rows_c, :] + b * den_blk[piece]
                        mbuf[r16, rows_c, :] = m_new
                return carry

            lax.fori_loop(0, 32, body, 0)

    def merge(r, carry):
        xo_ref[pl.ds(r, rows, stride=n_res), :] = obuf[0, r] / lbuf[0, r]
        return carry

    lax.fori_loop(0, n_res, merge, 0)
    out_ref[0] = (xo_ref[...] * g_ref[0].astype(F32)).astype(BF16)


def _dilated_attention(qa, ka, va, bias, ga):
    b, n_pair, n_res, rows, _ = qa.shape
    s = n_res * rows
    a_spec = pl.BlockSpec((1, 1, n_res, rows, PAIR), lambda i, p: (i, p, 0, 0, 0))
    lane_spec = pl.BlockSpec((1, s, PAIR), lambda i, p: (i, 0, p))
    bias_spec = pl.BlockSpec((len(DILATIONS), 2, 1, 2 * BLOCK, 2 * BLOCK),
                             lambda i, p: (0, 0, p, 0, 0))
    buf = pltpu.VMEM((n_res, rows, PAIR), F32)
    pat_buf = pltpu.VMEM((len(DILATIONS), n_res, rows, PAIR), F32)
    return pl.pallas_call(
        _dilated_kernel,
        grid=(b, n_pair),
        in_specs=[a_spec, a_spec, a_spec, bias_spec, lane_spec],
        out_specs=lane_spec,
        out_shape=jax.ShapeDtypeStruct((b, s, n_pair * PAIR), BF16),
        scratch_shapes=[pat_buf, pat_buf, buf, pltpu.VMEM((s, PAIR), F32)],
        compiler_params=pltpu.CompilerParams(
            dimension_semantics=("arbitrary", "arbitrary"), vmem_limit_bytes=VMEM_LIMIT),
        name="dilated_attention",
    )(qa, ka, va, bias, ga)


def _mla_kernel(qn_ref, qr_ref, k_ref, vt_ref, g_ref, out_ref, qt_ref, acc_ref, l_ref):
    tq, tk, qs = MLA_Q_TILE, MLA_K_TILE, MLA_Q_SUB
    n_sub = tq // qs
    n_chain = 2 * n_sub
    s_len = k_ref.shape[2]
    lo_q = _lo_mask((qs, PAIR))
    rope_head = lax.broadcasted_iota(jnp.int32, (qs, LANES), 1) // ROPE_DIM
    even_head = (pl.program_id(1) % 2) * 2

    def scores(c, kb):
        return jnp.dot(kb, qt_ref[c], preferred_element_type=F32)

    def causal(st, key_offset):
        key = lax.broadcasted_iota(jnp.int32, st.shape, 0) + key_offset
        qry = lax.broadcasted_iota(jnp.int32, st.shape, 1)
        return jnp.where(key <= qry, st, NEG)

    def run_jobs(jobs, ls, assign):
        ahead = 2
        sts = [scores(c, kb) for c, kb, _, _ in jobs[:ahead]]
        for n, (c, kb, vtb, key_offset) in enumerate(jobs):
            if n + ahead < len(jobs):
                nc, nkb, _, _ = jobs[n + ahead]
                sts.append(scores(nc, nkb))
            st = sts[n] if key_offset is None else causal(sts[n], key_offset)
            p = jnp.exp2(st)
            col_sum = jnp.sum(p, axis=0, keepdims=True)
            pv = jnp.dot(vtb, p.astype(BF16), preferred_element_type=F32)
            if c in assign:
                assign.remove(c)
                ls[c] = col_sum
                acc_ref[c] = pv
            else:
                ls[c] = ls[c] + col_sum
                acc_ref[c] += pv
        return ls

    def q_tile(qi, carry):
        row0 = pl.multiple_of(qi * tq, tq)
        for h in range(n_sub):
            r = pl.multiple_of(row0 + h * qs, qs)
            qn = qn_ref[0, 0, pl.ds(r, qs), :].astype(F32)
            qr = qr_ref[0, pl.ds(r, qs), :].astype(F32)
            q_even = jnp.concatenate(
                [jnp.where(lo_q, qn, 0.0), jnp.where(rope_head == even_head, qr, 0.0)], axis=1)
            q_odd = jnp.concatenate(
                [jnp.where(lo_q, 0.0, qn), jnp.where(rope_head == even_head + 1, qr, 0.0)], axis=1)
            qt_ref[2 * h] = q_even.T.astype(BF16)
            qt_ref[2 * h + 1] = q_odd.T.astype(BF16)

        def kv_block(start, size):
            start = pl.multiple_of(start, qs)
            return k_ref[0, 0, pl.ds(start, size), :], vt_ref[0, 0, :, pl.ds(start, size)]

        jobs = []
        for h in range(n_sub):
            for k0 in range(0, (h + 1) * qs, tk):
                size = min(tk, (h + 1) * qs - k0)
                kb, vtb = kv_block(row0 + k0, size)
                offset = k0 - h * qs if k0 + size > h * qs else None
                jobs += [(2 * h, kb, vtb, offset), (2 * h + 1, kb, vtb, offset)]
        ls = run_jobs(jobs, [None] * n_chain, set(range(n_chain)))

        tiles_per_step = tq // tk

        def full_step(j, ls_in):
            jobs = []
            for t in range(tiles_per_step):
                kb, vtb = kv_block((j * tiles_per_step + t) * tk, tk)
                jobs += [(c, kb, vtb, None) for c in range(n_chain)]
            return tuple(run_jobs(jobs, list(ls_in), set()))

        n_full = qi * tiles_per_step
        ls = lax.fori_loop(0, qi, full_step, tuple(ls))
        for c in range(n_chain):
            l_ref[c] = jnp.broadcast_to(ls[c], (8, qs))
        den_lo = functools.reduce(jnp.minimum, [jnp.min(l) for l in ls])
        den_hi = functools.reduce(jnp.maximum, [jnp.max(l) for l in ls])
        num_hi = jnp.max(jnp.abs(acc_ref[...]))
        unsafe = jnp.logical_not((den_lo >= SAFE_MIN) & (den_hi <= SAFE_MAX) & (num_hi <= SAFE_MAX))

        @pl.when(unsafe)
        def _():
            def chain(c, carry_c):
                q_start = (c // 2) * qs

                def kv_step(j, st_in):
                    m, l = st_in
                    kb, vtb = kv_block(j * tk, tk)
                    st = causal(scores(c, kb), j * tk - row0 - q_start)
                    m_new = jnp.maximum(m, jnp.max(st, axis=0, keepdims=True))
                    alpha = jnp.exp2(m - m_new)
                    p = jnp.exp2(st - m_new)
                    pv = jnp.dot(vtb, p.astype(BF16), preferred_element_type=F32)
                    acc_ref[c] = alpha * acc_ref[c] + pv
                    return m_new, alpha * l + jnp.sum(p, axis=0, keepdims=True)

                acc_ref[c] = jnp.zeros((PAIR, qs), F32)
                init = (jnp.full((1, qs), NEG, F32), jnp.zeros((1, qs), F32))
                _, l = lax.fori_loop(0, n_full + tq // tk, kv_step, init)
                l_ref[c] = jnp.broadcast_to(l, (8, qs))
                return carry_c

            lax.fori_loop(0, n_chain, chain, 0)

        even_rows = lax.broadcasted_iota(jnp.int32, (PAIR, qs), 0) < HEAD_DIM
        for h in range(n_sub):
            ot = jnp.where(even_rows, acc_ref[2 * h] / l_ref[2 * h, 0:1, :],
                           acc_ref[2 * h + 1] / l_ref[2 * h + 1, 0:1, :])
            r = pl.multiple_of(row0 + h * qs, qs)
            gate = g_ref[0, pl.ds(r, qs), :].astype(F32)
            out_ref[0, pl.ds(r, qs), :] = (ot.T * gate).astype(BF16)
        return carry

    lax.fori_loop(0, s_len // tq, q_tile, 0)


def _latent_attention(qn, qr, kcat, vt, gb):
    b, n_pair, s, _ = qn.shape
    n_chain = 2 * (MLA_Q_TILE // MLA_Q_SUB)
    pair_spec = lambda w: pl.BlockSpec((1, 1, s, w), lambda i, p: (i, p, 0, 0))
    vt_spec = pl.BlockSpec((1, 1, PAIR, s), lambda i, p: (i, p, 0, 0))
    lane_spec = pl.BlockSpec((1, s, PAIR), lambda i, p: (i, 0, p))
    return pl.pallas_call(
        _mla_kernel,
        grid=(b, n_pair),
        in_specs=[pair_spec(PAIR), pl.BlockSpec((1, s, LANES), lambda i, p: (i, 0, p // 2)),
                  pair_spec(2 * LANES), vt_spec, lane_spec],
        out_specs=lane_spec,
        out_shape=jax.ShapeDtypeStruct((b, s, n_pair * PAIR), BF16),
        scratch_shapes=[pltpu.VMEM((n_chain, 2 * LANES, MLA_Q_SUB), BF16),
                        pltpu.VMEM((n_chain, PAIR, MLA_Q_SUB), F32),
                        pltpu.VMEM((n_chain, 8, MLA_Q_SUB), F32)],
        compiler_params=pltpu.CompilerParams(
            dimension_semantics=("arbitrary", "arbitrary"), vmem_limit_bytes=VMEM_LIMIT),
        name="latent_attention",
    )(qn, qr, kcat, vt, gb)


def _out_kernel(ma_ref, mb_ref, x_ref, wa_ref, wb_ref, out_ref):
    acc = jnp.dot(ma_ref[0], wa_ref[...], preferred_element_type=F32)
    acc = acc + jnp.dot(mb_ref[0], wb_ref[...], preferred_element_type=F32)
    out_ref[0] = x_ref[0] + acc


def _output_projection(ma, mb, x, wa, wb):
    b, s, d = x.shape
    tm = ROW_TILE
    row_spec = lambda w: pl.BlockSpec((1, tm, w), lambda i, j: (i, j, 0))
    full = lambda a: pl.BlockSpec(a.shape, lambda i, j: (0,) * a.ndim)
    return pl.pallas_call(
        _out_kernel,
        grid=(b, s // tm),
        in_specs=[row_spec(ma.shape[-1]), row_spec(mb.shape[-1]), row_spec(d), full(wa), full(wb)],
        out_specs=row_spec(d),
        out_shape=jax.ShapeDtypeStruct((b, s, d), x.dtype),
        compiler_params=pltpu.CompilerParams(
            dimension_semantics=("arbitrary", "arbitrary"), vmem_limit_bytes=VMEM_LIMIT),
        name="output_projection",
    )(ma, mb, x, wa, wb)


def _t5_bucket(dist):
    max_exact = REL_BUCKETS // 2
    d = np.maximum(dist.astype(np.float32), np.float32(1.0))
    large = max_exact + (np.log(d / np.float32(max_exact)) / np.float32(
        math.log(REL_MAX_DISTANCE / max_exact)) * np.float32(REL_BUCKETS - max_exact)
                         ).astype(np.int32)
    large = np.minimum(large, REL_BUCKETS - 1)
    return np.where(dist < max_exact, dist, large)


def _dilated_bias(rel_bias, n_heads):
    tiles = []
    for dil in DILATIONS:
        n_chunk = MAX_DIL // dil
        csize = BLOCK // n_chunk
        idx = np.arange(BLOCK)
        true_idx = n_chunk * (idx % csize) + idx // csize
        qi = true_idx[:, None]
        ki = np.concatenate([true_idx, true_idx + BLOCK])[None, :]
        j = qi + BLOCK - ki
        valid = (j >= 0) & (j <= BLOCK)
        bucket = _t5_bucket(np.maximum(j, 0) * dil)
        bias = jnp.zeros((n_heads,) + bucket.shape, F32)
        for bkt in np.unique(bucket[valid]):
            bias = jnp.where((bucket == bkt)[None],
                             rel_bias[int(bkt)].astype(F32)[:, None, None], bias)
        bias = bias * LOG2E
        normal = jnp.where(valid[None], bias, NEG)
        first = jnp.where((valid & (ki >= BLOCK))[None], bias, NEG)
        both = jnp.stack([normal, first], axis=0)
        tiles.append(both.reshape(2, n_heads // 2, 2 * BLOCK, 2 * BLOCK))
    return jnp.stack(tiles, axis=0)


def _layer(x, pos_row, rel_bias, norm_gain, w_in, a_q_gain, a_k_gain, q_c_gain, w_uq, kv_c_gain,
           w_ukv, qn_gain, qr_gain, kn_gain, kr_gain, w_out):
    d = x.shape[-1]
    n_heads = 8
    a_width = n_heads * HEAD_DIM
    row = lambda v: v.reshape(1, -1).astype(F32)

    splits = (a_width, a_width, a_width, a_width, Q_RANK, KV_RANK, ROPE_DIM, a_width)
    starts = np.concatenate([[0], np.cumsum(splits)])
    kr0, kr1 = int(starts[6]), int(starts[7])
    rope_rep = LANES // ROPE_DIM
    w_in_p = jnp.concatenate(
        [w_in[:, :kr0], jnp.tile(w_in[:, kr0:kr1], (1, rope_rep)), w_in[:, kr1:]],
        axis=1).astype(BF16)
    off = dict(qa=int(starts[0]), ka=int(starts[1]), va=int(starts[2]), za=int(starts[3]),
               cq=int(starts[4]), ckv=int(starts[5]), kr=kr0, zb=kr0 + LANES)

    qk_dim = HEAD_DIM + ROPE_DIM
    wq = w_uq.reshape(Q_RANK, n_heads, qk_dim)
    wuq_p = jnp.concatenate([wq[:, :, :HEAD_DIM].reshape(Q_RANK, -1),
                             wq[:, :, HEAD_DIM:].reshape(Q_RANK, -1)], axis=1).astype(BF16)
    wkv = w_ukv.reshape(KV_RANK, n_heads, 2 * HEAD_DIM)
    wukv_p = jnp.concatenate([wkv[:, :, :HEAD_DIM].reshape(KV_RANK, -1),
                              wkv[:, :, HEAD_DIM:].reshape(KV_RANK, -1)], axis=1).astype(BF16)

    scale_a = HEAD_DIM ** -0.5
    scale_b = qk_dim ** -0.5
    inv_freq = ROPE_THETA ** (-jnp.arange(0, ROPE_DIM, 2, dtype=F32) / ROPE_DIM)
    rope_row = lambda v: row(jnp.tile(v, rope_rep))
    half = ROPE_DIM // 2
    freq = inv_freq.reshape(half, 1)
    spread = (np.arange(LANES)[None, :] % half == np.arange(half)[:, None]).astype(np.float32)
    spread = jnp.asarray(spread, BF16)
    sgn = rope_row(jnp.concatenate([-jnp.ones(half, F32), jnp.ones(half, F32)]))
    head_norm, rope_norm = HEAD_DIM ** 0.5, ROPE_DIM ** 0.5

    qa, ka, va, ga, qn, qr, kcat, vt, gb = _input_projection(
        x, pos_row, row(norm_gain), w_in_p,
        row(jnp.tile(a_q_gain, 2) * (head_norm * scale_a * LOG2E)),
        row(jnp.tile(a_k_gain, 2) * head_norm),
        row(q_c_gain), wuq_p, row(kv_c_gain), wukv_p,
        row(jnp.tile(qn_gain, 2) * (head_norm * scale_b * LOG2E)),
        rope_row(qr_gain * (rope_norm * scale_b * LOG2E)),
        row(jnp.tile(kn_gain, 2) * head_norm), rope_row(kr_gain), freq, spread, sgn, off)

    mixed_a = _dilated_attention(qa, ka, va, _dilated_bias(rel_bias, n_heads), ga)
    mixed_b = _latent_attention(qn, qr, kcat, vt, gb)
    w_out_b = w_out.astype(BF16)
    return _output_projection(mixed_a, mixed_b, x, w_out_b[:a_width], w_out_b[a_width:])


def kernel(x, positions, rel_bias, norm_gain, w_in, a_q_gain, a_k_gain, q_c_gain, w_uq, kv_c_gain,
           w_ukv, qn_gain, qr_gain, kn_gain, kr_gain, w_out):
    pos_row = positions[:, None, :]
    for l in range(norm_gain.shape[0]):
        x = _layer(x, pos_row, rel_bias, norm_gain[l], w_in[l], a_q_gain[l], a_k_gain[l],
                   q_c_gain[l], w_uq[l], kv_c_gain[l], w_ukv[l], qn_gain[l], qr_gain[l],
                   kn_gain[l], kr_gain[l], w_out[l])
    return x
```

```python
import functools
import math

import jax
import jax.numpy as jnp
import numpy as np
from jax import lax
from jax.experimental import pallas as pl
from jax.experimental.pallas import tpu as pltpu

F32 = jnp.float32
BF16 = jnp.bfloat16

LANES = 128
HEAD_DIM = 64
PAIR = 2 * HEAD_DIM
ROPE_DIM = 32
KV_RANK = 128
Q_RANK = 256
BLOCK = 128
DILATIONS = (1, 4, 16)
MAX_DIL = 16
DEINTERLEAVE = 4
ROPE_THETA = 10000.0
REL_BUCKETS = 32
REL_MAX_DISTANCE = 2048
EPS = 1e-6
NEG = -1e30
SAFE_MIN, SAFE_MAX = 1e-30, 1e30
LOG2E = math.log2(math.e)

ROW_TILE = 1024
OUT_ROW_TILE = 1024
MLA_Q_TILE = 1024
MLA_K_TILE = 512
MLA_Q_SUB = 256

VMEM_LIMIT = 56 * 1024 * 1024


def _nt_dot(a, b):
    return lax.dot_general(a, b, (((1,), (1,)), ((), ())), preferred_element_type=F32)


def _lo_mask(shape):
    return lax.broadcasted_iota(jnp.int32, shape, len(shape) - 1) < HEAD_DIM


def _silu(z):
    return z * (1.0 / (1.0 + jnp.exp(-z)))


def _group_rms(t, gain_row, group):
    lane_group = lax.broadcasted_iota(jnp.int32, t.shape, 1) // group
    sq = t * t
    ss = None
    for g in range(LANES // group):
        in_g = lane_group == g
        s = jnp.sum(jnp.where(in_g, sq, 0.0), axis=1, keepdims=True)
        ss = s if ss is None else jnp.where(in_g, s, ss)
    return t * lax.rsqrt(ss + group * EPS) * gain_row


def _proj_kernel(x_ref, pos_ref, ng_ref, w_in_ref, gq_ref, gk_ref, gcq_ref, wuq_ref, gckv_ref,
                 wukv_ref, gqn_ref, gqr_ref, gkn_ref, gkr_ref, freq_ref, spread_ref, sgn_ref,
                 qa_ref, ka_ref, va_ref, ga_ref, qn_ref, qr_ref, kcat_ref, vt_ref, gb_ref,
                 *stage_refs, off):
    n_res = qa_ref.shape[2]
    tm = x_ref.shape[1]
    sub = tm // n_res
    n_pair = qa_ref.shape[1]
    a_width = n_pair * PAIR

    x = x_ref[0]
    h = (x * lax.rsqrt(jnp.mean(x * x, axis=-1, keepdims=True) + EPS) * ng_ref[...]).astype(BF16)

    def seg(name, width):
        a = off[name]
        return jnp.dot(h, w_in_ref[:, a:a + width], preferred_element_type=F32)

    def emit_a(t, gain_ref, out_ref, rows_ref, quad_ref):
        quarter = tm // DEINTERLEAVE
        for p in range(n_pair):
            c = t[:, p * PAIR:(p + 1) * PAIR]
            if gain_ref is not None:
                c = _group_rms(c, gain_ref[...], HEAD_DIM)
            rows_ref[p] = c
            for g in range(DEINTERLEAVE):
                quad_ref[p, g * quarter:(g + 1) * quarter] = (
                    rows_ref[p, pl.ds(g, quarter, stride=DEINTERLEAVE), :])
            for r in range(n_res):
                g, h = r % DEINTERLEAVE, r // DEINTERLEAVE
                out_ref[0, p, r] = quad_ref[p, pl.ds(g * quarter + h, sub, stride=DEINTERLEAVE), :]

    ang = freq_ref[...] * pos_ref[0].astype(F32)

    def spread(table):
        hi = table.astype(BF16)
        lo = (table - hi.astype(F32)).astype(BF16)
        tn = lambda a: lax.dot_general(a, spread_ref[...], (((0,), (0,)), ((), ())),
                                       preferred_element_type=F32)
        return tn(hi) + tn(lo)

    cos_t = spread(jnp.cos(ang))
    sin_t = spread(jnp.sin(ang)) * sgn_ref[...]
    lane = lax.broadcasted_iota(jnp.int32, (tm, LANES), 1)
    first_half = (lane & (ROPE_DIM - 1)) < (ROPE_DIM // 2)

    def rope(t):
        partner = jnp.where(first_half, pltpu.roll(t, LANES - ROPE_DIM // 2, 1),
                            pltpu.roll(t, ROPE_DIM // 2, 1))
        return t * cos_t + partner * sin_t

    def rms(t, gain_ref):
        return t * lax.rsqrt(jnp.mean(t * t, axis=-1, keepdims=True) + EPS) * gain_ref[...]

    s_qa = seg("qa", a_width)
    s_ka = seg("ka", a_width)
    emit_a(s_qa, gq_ref, qa_ref, *stage_refs)
    s_cq = seg("cq", Q_RANK)
    s_ckv = seg("ckv", KV_RANK)
    s_kr = seg("kr", LANES)
    emit_a(s_ka, gk_ref, ka_ref, *stage_refs)
    q = jnp.dot(rms(s_cq, gcq_ref).astype(BF16), wuq_ref[...], preferred_element_type=F32)
    kv = jnp.dot(rms(s_ckv, gckv_ref).astype(BF16), wukv_ref[...], preferred_element_type=F32)
    s_za = seg("za", a_width)

    for p in range(n_pair):
        qn_ref[0, p] = _group_rms(q[:, p * PAIR:(p + 1) * PAIR], gqn_ref[...],
                                  HEAD_DIM).astype(BF16)
    for t in range(qr_ref.shape[2] // LANES):
        tile = q[:, a_width + t * LANES:a_width + (t + 1) * LANES]
        qr_ref[0, :, t * LANES:(t + 1) * LANES] = rope(
            _group_rms(tile, gqr_ref[...], ROPE_DIM)).astype(BF16)
    s_zb = seg("zb", a_width)

    kr = rope(rms(s_kr, gkr_ref)).astype(BF16)
    for p in range(n_pair):
        kn = _group_rms(kv[:, p * PAIR:(p + 1) * PAIR], gkn_ref[...], HEAD_DIM)
        kcat_ref[0, p, :, 0:LANES] = kn.astype(BF16)
        kcat_ref[0, p, :, LANES:2 * LANES] = kr
        vt_ref[0, p] = kv[:, a_width + p * PAIR:a_width + (p + 1) * PAIR].T.astype(BF16)

    ga_ref[0] = _silu(s_za).astype(BF16)
    s_va = seg("va", a_width)
    gb_ref[0] = _silu(s_zb).astype(BF16)
    emit_a(s_va, None, va_ref, *stage_refs)


def _input_projection(x, pos_row, ng, w_in_p, gq, gk, gcq, wuq_p, gckv, wukv_p, gqn, gqr, gkn, gkr,
                      freq, spread, sgn, off):
    b, s, d = x.shape
    tm = ROW_TILE
    n_pair = 4
    n_res = MAX_DIL
    sub = tm // n_res
    grid = (b, s // tm)

    def full(a):
        return pl.BlockSpec(a.shape, lambda i, j: (0,) * a.ndim)

    a_shape = jax.ShapeDtypeStruct((b, n_pair, n_res, s // n_res, PAIR), F32)
    a_spec = pl.BlockSpec((1, n_pair, n_res, sub, PAIR), lambda i, j: (i, 0, 0, j, 0))
    row_spec = lambda w: pl.BlockSpec((1, tm, w), lambda i, j: (i, j, 0))
    pair_spec = lambda w: pl.BlockSpec((1, n_pair, tm, w), lambda i, j: (i, 0, j, 0))

    out_shape = (
        a_shape, a_shape, a_shape,
        jax.ShapeDtypeStruct((b, s, n_pair * PAIR), BF16),
        jax.ShapeDtypeStruct((b, n_pair, s, PAIR), BF16),
        jax.ShapeDtypeStruct((b, s, 2 * LANES), BF16),
        jax.ShapeDtypeStruct((b, n_pair, s, 2 * LANES), BF16),
        jax.ShapeDtypeStruct((b, n_pair, PAIR, s), BF16),
        jax.ShapeDtypeStruct((b, s, n_pair * PAIR), BF16),
    )
    out_specs = (
        a_spec, a_spec, a_spec,
        row_spec(n_pair * PAIR),
        pair_spec(PAIR), row_spec(2 * LANES), pair_spec(2 * LANES),
        pl.BlockSpec((1, n_pair, PAIR, tm), lambda i, j: (i, 0, 0, j)),
        row_spec(n_pair * PAIR),
    )
    params = (ng, w_in_p, gq, gk, gcq, wuq_p, gckv, wukv_p, gqn, gqr, gkn, gkr, freq, spread, sgn)
    pos_spec = pl.BlockSpec((1, 1, tm), lambda i, j: (i, 0, j))
    in_specs = [row_spec(d), pos_spec] + [full(a) for a in params]
    stage = pltpu.VMEM((n_pair, tm, PAIR), F32)
    return pl.pallas_call(
        functools.partial(_proj_kernel, off=off),
        grid=grid,
        in_specs=in_specs,
        out_specs=out_specs,
        out_shape=out_shape,
        scratch_shapes=[stage] * 2,
        compiler_params=pltpu.CompilerParams(
            dimension_semantics=("arbitrary", "arbitrary"), vmem_limit_bytes=VMEM_LIMIT),
        name="input_projection",
    )(x, pos_row, *params)


DILATED_GROUP = 8


def _dilated_schedule(pat, i, g):
    if pat == 0:
        return 0, DILATED_GROUP * i + g
    if pat == 1:
        return g % 4, (DILATED_GROUP // 4) * i + g // 4
    return (DILATED_GROUP // 2) * i + g // 2, g % 2


def _dilated_kernel(q_ref, k_ref, v_ref, bias_ref, g_ref, out_ref, obuf, lbuf, mbuf, xo_ref):
    n_res = q_ref.shape[2]
    rows = q_ref.shape[3]
    lo = _lo_mask((BLOCK, PAIR))
    ones = jnp.ones((2 * BLOCK, LANES), BF16)

    def pattern_geometry(pat):
        dil = DILATIONS[pat]
        n_chunk = MAX_DIL // dil
        csize = BLOCK // n_chunk
        return dil, n_chunk, csize

    def load_block(pat, stream, n):
        dil, n_chunk, csize = pattern_geometry(pat)
        chunks = [(stream + dil * c, n * csize) for c in range(n_chunk)]
        first = n == 0

        def gather(ref, shift):
            parts = [ref[0, 0, r, pl.ds(pl.multiple_of(st - shift, 8), csize), :]
                     for r, st in chunks]
            return parts[0] if len(parts) == 1 else jnp.concatenate(parts, axis=0)

        prev_shift = jnp.where(first, 0, csize)
        qc = gather(q_ref, 0)
        lhs = jnp.concatenate([jnp.where(lo, qc, 0.0), jnp.where(lo, 0.0, qc)],
                              axis=0).astype(BF16)
        kk = jnp.concatenate([gather(k_ref, prev_shift), gather(k_ref, 0)], axis=0).astype(BF16)
        vv = jnp.concatenate([gather(v_ref, prev_shift), gather(v_ref, 0)], axis=0).astype(BF16)
        vv1 = jnp.concatenate([vv, ones], axis=1)
        bias = bias_ref[pat, jnp.where(first, 1, 0), 0]
        return chunks, csize, lhs, kk, vv1, bias

    def split_heads(r):
        return jnp.where(lo, r[:BLOCK], r[BLOCK:])

    def store_chunks(buf, lead, chunks, csize, val):
        for c, (r, st) in enumerate(chunks):
            buf[lead + (r, pl.ds(pl.multiple_of(st, 8), csize), slice(None))] = (
                val[c * csize:(c + 1) * csize])

    def fast_body(i, carry):
        slots = [(pat, g) for g in range(DILATED_GROUP) for pat in range(len(DILATIONS))]
        blocks = [load_block(pat, *_dilated_schedule(pat, i, g)) for pat, g in slots]
        ahead = 3
        scores = [_nt_dot(b[2], b[3]) for b in blocks[:ahead]]
        for k, (chunks, csize, _, _, vv1, bias) in enumerate(blocks):
            if k + ahead < len(blocks):
                nb = blocks[k + ahead]
                scores.append(_nt_dot(nb[2], nb[3]))
            p = jnp.exp2(scores[k] + bias).astype(BF16)
            r = jnp.dot(p, vv1, preferred_element_type=F32)
            lead = (slots[k][0],)
            store_chunks(obuf, lead, chunks, csize, split_heads(r[:, :LANES]))
            store_chunks(lbuf, lead, chunks, csize, split_heads(r[:, LANES:]))
        return carry

    lax.fori_loop(0, 32 // DILATED_GROUP, fast_body, 0)

    def pattern_sum(r, stats):
        num = obuf[0, r] + obuf[1, r] + obuf[2, r]
        den = lbuf[0, r] + lbuf[1, r] + lbuf[2, r]
        obuf[0, r] = num
        lbuf[0, r] = den
        den_lo, den_hi, num_hi = stats
        return (jnp.minimum(den_lo, jnp.min(den, axis=0, keepdims=True)),
                jnp.maximum(den_hi, jnp.max(den, axis=0, keepdims=True)),
                jnp.maximum(num_hi, jnp.max(jnp.abs(num), axis=0, keepdims=True)))

    row = lambda v: jnp.full((1, PAIR), v, F32)
    den_lo, den_hi, num_hi = lax.fori_loop(0, n_res, pattern_sum, (row(1.0), row(1.0), row(0.0)))
    unsafe = jnp.logical_not((jnp.min(den_lo) >= SAFE_MIN) & (jnp.max(den_hi) <= SAFE_MAX)
                             & (jnp.max(num_hi) <= SAFE_MAX))

    @pl.when(unsafe)
    def _():
        for pat in range(len(DILATIONS)):
            def body(i, carry, pat=pat):
                n_blocks = 32 // DILATIONS[pat]
                chunks, csize, lhs, kk, vv1, bias = load_block(pat, i // n_blocks, i % n_blocks)
                t = _nt_dot(lhs, kk) + bias
                m = jnp.max(t, axis=1, keepdims=True)
                p = jnp.exp2(t - m).astype(BF16)
                r = jnp.dot(p, vv1, preferred_element_type=F32)
                num_blk, den_blk = split_heads(r[:, :LANES]), split_heads(r[:, LANES:])
                m2 = jnp.where(lo, m[:BLOCK], m[BLOCK:])
                if pat == 0:
                    store_chunks(obuf, (0,), chunks, csize, num_blk)
                    store_chunks(lbuf, (0,), chunks, csize, den_blk)
                    store_chunks(mbuf, (), chunks, csize, m2)
                else:
                    for c, (r16, st) in enumerate(chunks):
                        rows_c = pl.ds(pl.multiple_of(st, 8), csize)
                        piece = slice(c * csize, (c + 1) * csize)
                        m_old = mbuf[r16, rows_c, :]
                        m_new = jnp.maximum(m_old, m2[piece])
                        a, b = jnp.exp2(m_old - m_new), jnp.exp2(m2[piece] - m_new)
                        obuf[0, r16, rows_c, :] = a * obuf[0, r16, rows_c, :] + b * num_blk[piece]
                        lbuf[0, r16, rows_c, :] = a * lbuf[0, r16, rows_c, :] + b * den_blk[piece]
                        mbuf[r16, rows_c, :] = m_new
                return carry

            lax.fori_loop(0, 32, body, 0)

    def merge(r, carry):
        xo_ref[pl.ds(r, rows, stride=n_res), :] = obuf[0, r] / lbuf[0, r]
        return carry

    lax.fori_loop(0, n_res, merge, 0)
    out_ref[0] = (xo_ref[...] * g_ref[0].astype(F32)).astype(BF16)


def _dilated_attention(qa, ka, va, bias, ga):
    b, n_pair, n_res, rows, _ = qa.shape
    s = n_res * rows
    a_spec = pl.BlockSpec((1, 1, n_res, rows, PAIR), lambda i, p: (i, p, 0, 0, 0))
    lane_spec = pl.BlockSpec((1, s, PAIR), lambda i, p: (i, 0, p))
    bias_spec = pl.BlockSpec((len(DILATIONS), 2, 1, 2 * BLOCK, 2 * BLOCK),
                             lambda i, p: (0, 0, p, 0, 0))
    buf = pltpu.VMEM((n_res, rows, PAIR), F32)
    pat_buf = pltpu.VMEM((len(DILATIONS), n_res, rows, PAIR), F32)
    return pl.pallas_call(
        _dilated_kernel,
        grid=(b, n_pair),
        in_specs=[a_spec, a_spec, a_spec, bias_spec, lane_spec],
        out_specs=lane_spec,
        out_shape=jax.ShapeDtypeStruct((b, s, n_pair * PAIR), BF16),
        scratch_shapes=[pat_buf, pat_buf, buf, pltpu.VMEM((s, PAIR), F32)],
        compiler_params=pltpu.CompilerParams(
            dimension_semantics=("arbitrary", "arbitrary"), vmem_limit_bytes=VMEM_LIMIT),
        name="dilated_attention",
    )(qa, ka, va, bias, ga)


def _mla_kernel(qn_ref, qr_ref, k_ref, vt_ref, g_ref, out_ref, qt_ref, acc_ref, l_ref):
    tq, tk, qs = MLA_Q_TILE, MLA_K_TILE, MLA_Q_SUB
    n_sub = tq // qs
    n_chain = 2 * n_sub
    s_len = k_ref.shape[2]
    lo_q = _lo_mask((qs, PAIR))
    rope_head = lax.broadcasted_iota(jnp.int32, (qs, LANES), 1) // ROPE_DIM
    even_head = (pl.program_id(1) % 2) * 2

    def scores(c, kb):
        return jnp.dot(kb, qt_ref[c], preferred_element_type=F32)

    def causal(st, key_offset):
        key = lax.broadcasted_iota(jnp.int32, st.shape, 0) + key_offset
        qry = lax.broadcasted_iota(jnp.int32, st.shape, 1)
        return jnp.where(key <= qry, st, NEG)

    def run_jobs(jobs, ls, assign):
        ahead = 2
        sts = [scores(c, kb) for c, kb, _, _ in jobs[:ahead]]
        for n, (c, kb, vtb, key_offset) in enumerate(jobs):
            if n + ahead < len(jobs):
                nc, nkb, _, _ = jobs[n + ahead]
                sts.append(scores(nc, nkb))
            st = sts[n] if key_offset is None else causal(sts[n], key_offset)
            p = jnp.exp2(st)
            col_sum = jnp.sum(p, axis=0, keepdims=True)
            pv = jnp.dot(vtb, p.astype(BF16), preferred_element_type=F32)
            if c in assign:
                assign.remove(c)
                ls[c] = col_sum
                acc_ref[c] = pv
            else:
                ls[c] = ls[c] + col_sum
                acc_ref[c] += pv
        return ls

    def q_tile(qi, carry):
        row0 = pl.multiple_of(qi * tq, tq)
        for h in range(n_sub):
            r = pl.multiple_of(row0 + h * qs, qs)
            qn = qn_ref[0, 0, pl.ds(r, qs), :].astype(F32)
            qr = qr_ref[0, pl.ds(r, qs), :].astype(F32)
            q_even = jnp.concatenate(
                [jnp.where(lo_q, qn, 0.0), jnp.where(rope_head == even_head, qr, 0.0)], axis=1)
            q_odd = jnp.concatenate(
                [jnp.where(lo_q, 0.0, qn), jnp.where(rope_head == even_head + 1, qr, 0.0)], axis=1)
            qt_ref[2 * h] = q_even.T.astype(BF16)
            qt_ref[2 * h + 1] = q_odd.T.astype(BF16)

        def kv_block(start, size):
            start = pl.multiple_of(start, qs)
            return k_ref[0, 0, pl.ds(start, size), :], vt_ref[0, 0, :, pl.ds(start, size)]

        jobs = []
        for h in range(n_sub):
            for k0 in range(0, (h + 1) * qs, tk):
                size = min(tk, (h + 1) * qs - k0)
                kb, vtb = kv_block(row0 + k0, size)
                offset = k0 - h * qs if k0 + size > h * qs else None
                jobs += [(2 * h, kb, vtb, offset), (2 * h + 1, kb, vtb, offset)]
        ls = run_jobs(jobs, [None] * n_chain, set(range(n_chain)))

        tiles_per_step = tq // tk

        def full_step(j, ls_in):
            jobs = []
            for t in range(tiles_per_step):
                kb, vtb = kv_block((j * tiles_per_step + t) * tk, tk)
                jobs += [(c, kb, vtb, None) for c in range(n_chain)]
            return tuple(run_jobs(jobs, list(ls_in), set()))

        n_full = qi * tiles_per_step
        ls = lax.fori_loop(0, qi, full_step, tuple(ls))
        for c in range(n_chain):
            l_ref[c] = jnp.broadcast_to(ls[c], (8, qs))
        den_lo = functools.reduce(jnp.minimum, [jnp.min(l) for l in ls])
        den_hi = functools.reduce(jnp.maximum, [jnp.max(l) for l in ls])
        num_hi = jnp.max(jnp.abs(acc_ref[...]))
        unsafe = jnp.logical_not((den_lo >= SAFE_MIN) & (den_hi <= SAFE_MAX) & (num_hi <= SAFE_MAX))

        @pl.when(unsafe)
        def _():
            def chain(c, carry_c):
                q_start = (c // 2) * qs

                def kv_step(j, st_in):
                    m, l = st_in
                    kb, vtb = kv_block(j * tk, tk)
                    st = causal(scores(c, kb), j * tk - row0 - q_start)
                    m_new = jnp.maximum(m, jnp.max(st, axis=0, keepdims=True))
                    alpha = jnp.exp2(m - m_new)
                    p = jnp.exp2(st - m_new)
                    pv = jnp.dot(vtb, p.astype(BF16), preferred_element_type=F32)
                    acc_ref[c] = alpha * acc_ref[c] + pv
                    return m_new, alpha * l + jnp.sum(p, axis=0, keepdims=True)

                acc_ref[c] = jnp.zeros((PAIR, qs), F32)
                init = (jnp.full((1, qs), NEG, F32), jnp.zeros((1, qs), F32))
                _, l = lax.fori_loop(0, n_full + tq // tk, kv_step, init)
                l_ref[c] = jnp.broadcast_to(l, (8, qs))
                return carry_c

            lax.fori_loop(0, n_chain, chain, 0)

        even_rows = lax.broadcasted_iota(jnp.int32, (PAIR, qs), 0) < HEAD_DIM
        for h in range(n_sub):
            ot = jnp.where(even_rows, acc_ref[2 * h] / l_ref[2 * h, 0:1, :],
                           acc_ref[2 * h + 1] / l_ref[2 * h + 1, 0:1, :])
            r = pl.multiple_of(row0 + h * qs, qs)
            gate = g_ref[0, pl.ds(r, qs), :].astype(F32)
            out_ref[0, pl.ds(r, qs), :] = (ot.T * gate).astype(BF16)
        return carry

    lax.fori_loop(0, s_len // tq, q_tile, 0)


def _latent_attention(qn, qr, kcat, vt, gb):
    b, n_pair, s, _ = qn.shape
    n_chain = 2 * (MLA_Q_TILE // MLA_Q_SUB)
    pair_spec = lambda w: pl.BlockSpec((1, 1, s, w), lambda i, p: (i, p, 0, 0))
    vt_spec = pl.BlockSpec((1, 1, PAIR, s), lambda i, p: (i, p, 0, 0))
    lane_spec = pl.BlockSpec((1, s, PAIR), lambda i, p: (i, 0, p))
    return pl.pallas_call(
        _mla_kernel,
        grid=(b, n_pair),
        in_specs=[pair_spec(PAIR), pl.BlockSpec((1, s, LANES), lambda i, p: (i, 0, p // 2)),
                  pair_spec(2 * LANES), vt_spec, lane_spec],
        out_specs=lane_spec,
        out_shape=jax.ShapeDtypeStruct((b, s, n_pair * PAIR), BF16),
        scratch_shapes=[pltpu.VMEM((n_chain, 2 * LANES, MLA_Q_SUB), BF16),
                        pltpu.VMEM((n_chain, PAIR, MLA_Q_SUB), F32),
                        pltpu.VMEM((n_chain, 8, MLA_Q_SUB), F32)],
        compiler_params=pltpu.CompilerParams(
            dimension_semantics=("arbitrary", "arbitrary"), vmem_limit_bytes=VMEM_LIMIT),
        name="latent_attention",
    )(qn, qr, kcat, vt, gb)


def _out_kernel(ma_ref, mb_ref, x_ref, wa_ref, wb_ref, out_ref):
    acc = jnp.dot(ma_ref[0], wa_ref[...], preferred_element_type=F32)
    acc = acc + jnp.dot(mb_ref[0], wb_ref[...], preferred_element_type=F32)
    out_ref[0] = x_ref[0] + acc


def _output_projection(ma, mb, x, wa, wb):
    b, s, d = x.shape
    tm = OUT_ROW_TILE
    row_spec = lambda w: pl.BlockSpec((1, tm, w), lambda i, j: (i, j, 0))
    full = lambda a: pl.BlockSpec(a.shape, lambda i, j: (0,) * a.ndim)
    return pl.pallas_call(
        _out_kernel,
        grid=(b, s // tm),
        in_specs=[row_spec(ma.shape[-1]), row_spec(mb.shape[-1]), row_spec(d), full(wa), full(wb)],
        out_specs=row_spec(d),
        out_shape=jax.ShapeDtypeStruct((b, s, d), x.dtype),
        compiler_params=pltpu.CompilerParams(
            dimension_semantics=("arbitrary", "arbitrary"), vmem_limit_bytes=VMEM_LIMIT),
        name="output_projection",
    )(ma, mb, x, wa, wb)


def _t5_bucket(dist):
    max_exact = REL_BUCKETS // 2
    d = np.maximum(dist.astype(np.float32), np.float32(1.0))
    large = max_exact + (np.log(d / np.float32(max_exact)) / np.float32(
        math.log(REL_MAX_DISTANCE / max_exact)) * np.float32(REL_BUCKETS - max_exact)
                         ).astype(np.int32)
    large = np.minimum(large, REL_BUCKETS - 1)
    return np.where(dist < max_exact, dist, large)


def _dilated_bias(rel_bias, n_heads):
    tiles = []
    for dil in DILATIONS:
        n_chunk = MAX_DIL // dil
        csize = BLOCK // n_chunk
        idx = np.arange(BLOCK)
        true_idx = n_chunk * (idx % csize) + idx // csize
        qi = true_idx[:, None]
        ki = np.concatenate([true_idx, true_idx + BLOCK])[None, :]
        j = qi + BLOCK - ki
        valid = (j >= 0) & (j <= BLOCK)
        bucket = _t5_bucket(np.maximum(j, 0) * dil)
        bias = jnp.zeros((n_heads,) + bucket.shape, F32)
        for bkt in np.unique(bucket[valid]):
            bias = jnp.where((bucket == bkt)[None],
                             rel_bias[int(bkt)].astype(F32)[:, None, None], bias)
        bias = bias * LOG2E
        normal = jnp.where(valid[None], bias, NEG)
        first = jnp.where((valid & (ki >= BLOCK))[None], bias, NEG)
        both = jnp.stack([normal, first], axis=0)
        tiles.append(both.reshape(2, n_heads // 2, 2 * BLOCK, 2 * BLOCK))
    return jnp.stack(tiles, axis=0)


def _layer(x, pos_row, rel_bias, norm_gain, w_in, a_q_gain, a_k_gain, q_c_gain, w_uq, kv_c_gain,
           w_ukv, qn_gain, qr_gain, kn_gain, kr_gain, w_out):
    d = x.shape[-1]
    n_heads = 8
    a_width = n_heads * HEAD_DIM
    row = lambda v: v.reshape(1, -1).astype(F32)

    splits = (a_width, a_width, a_width, a_width, Q_RANK, KV_RANK, ROPE_DIM, a_width)
    starts = np.concatenate([[0], np.cumsum(splits)])
    kr0, kr1 = int(starts[6]), int(starts[7])
    rope_rep = LANES // ROPE_DIM
    w_in_p = jnp.concatenate(
        [w_in[:, :kr0], jnp.tile(w_in[:, kr0:kr1], (1, rope_rep)), w_in[:, kr1:]],
        axis=1).astype(BF16)
    off = dict(qa=int(starts[0]), ka=int(starts[1]), va=int(starts[2]), za=int(starts[3]),
               cq=int(starts[4]), ckv=int(starts[5]), kr=kr0, zb=kr0 + LANES)

    qk_dim = HEAD_DIM + ROPE_DIM
    wq = w_uq.reshape(Q_RANK, n_heads, qk_dim)
    wuq_p = jnp.concatenate([wq[:, :, :HEAD_DIM].reshape(Q_RANK, -1),
                             wq[:, :, HEAD_DIM:].reshape(Q_RANK, -1)], axis=1).astype(BF16)
    wkv = w_ukv.reshape(KV_RANK, n_heads, 2 * HEAD_DIM)
    wukv_p = jnp.concatenate([wkv[:, :, :HEAD_DIM].reshape(KV_RANK, -1),
                              wkv[:, :, HEAD_DIM:].reshape(KV_RANK, -1)], axis=1).astype(BF16)

    scale_a = HEAD_DIM ** -0.5
    scale_b = qk_dim ** -0.5
    inv_freq = ROPE_THETA ** (-jnp.arange(0, ROPE_DIM, 2, dtype=F32) / ROPE_DIM)
    rope_row = lambda v: row(jnp.tile(v, rope_rep))
    half = ROPE_DIM // 2
    freq = inv_freq.reshape(half, 1)
    spread = (np.arange(LANES)[None, :] % half == np.arange(half)[:, None]).astype(np.float32)
    spread = jnp.asarray(spread, BF16)
    sgn = rope_row(jnp.concatenate([-jnp.ones(half, F32), jnp.ones(half, F32)]))
    head_norm, rope_norm = HEAD_DIM ** 0.5, ROPE_DIM ** 0.5

    qa, ka, va, ga, qn, qr, kcat, vt, gb = _input_projection(
        x, pos_row, row(norm_gain), w_in_p,
        row(jnp.tile(a_q_gain, 2) * (head_norm * scale_a * LOG2E)),
        row(jnp.tile(a_k_gain, 2) * head_norm),
        row(q_c_gain), wuq_p, row(kv_c_gain), wukv_p,
        row(jnp.tile(qn_gain, 2) * (head_norm * scale_b * LOG2E)),
        rope_row(qr_gain * (rope_norm * scale_b * LOG2E)),
        row(jnp.tile(kn_gain, 2) * head_norm), rope_row(kr_gain), freq, spread, sgn, off)

    mixed_a = _dilated_attention(qa, ka, va, _dilated_bias(rel_bias, n_heads), ga)
    mixed_b = _latent_attention(qn, qr, kcat, vt, gb)
    w_out_b = w_out.astype(BF16)
    return _output_projection(mixed_a, mixed_b, x, w_out_b[:a_width], w_out_b[a_width:])


def kernel(x, positions, rel_bias, norm_gain, w_in, a_q_gain, a_k_gain, q_c_gain, w_uq, kv_c_gain,
           w_ukv, qn_gain, qr_gain, kn_gain, kr_gain, w_out):
    pos_row = positions[:, None, :]
    for l in range(norm_gain.shape[0]):
        x = _layer(x, pos_row, rel_bias, norm_gain[l], w_in[l], a_q_gain[l], a_k_gain[l],
                   q_c_gain[l], w_uq[l], kv_c_gain[l], w_ukv[l], qn_gain[l], qr_gain[l],
                   kn_gain[l], kr_gain[l], w_out[l])
    return x
```

```python
import functools
import math

import jax
import jax.numpy as jnp
import numpy as np
from jax import lax
from jax.experimental import pallas as pl
from jax.experimental.pallas import tpu as pltpu

F32 = jnp.float32
BF16 = jnp.bfloat16

LANES = 128
HEAD_DIM = 64
PAIR = 2 * HEAD_DIM
ROPE_DIM = 32
KV_RANK = 128
Q_RANK = 256
BLOCK = 128
DILATIONS = (1, 4, 16)
MAX_DIL = 16
DEINTERLEAVE = 4
ROPE_THETA = 10000.0
REL_BUCKETS = 32
REL_MAX_DISTANCE = 2048
EPS = 1e-6
NEG = -1e30
SAFE_MIN, SAFE_MAX = 1e-30, 1e30
LOG2E = math.log2(math.e)

ROW_TILE = 1024
OUT_ROW_TILE = 1024
MLA_Q_TILE = 2048
MLA_TILES_PER_STEP = 2
MLA_K_TILE = 512
MLA_Q_SUB = 256

VMEM_LIMIT = 56 * 1024 * 1024


def _nt_dot(a, b):
    return lax.dot_general(a, b, (((1,), (1,)), ((), ())), preferred_element_type=F32)


def _lo_mask(shape):
    return lax.broadcasted_iota(jnp.int32, shape, len(shape) - 1) < HEAD_DIM


def _silu(z):
    return z * (1.0 / (1.0 + jnp.exp(-z)))


def _group_rms(t, gain_row, group):
    lane_group = lax.broadcasted_iota(jnp.int32, t.shape, 1) // group
    sq = t * t
    ss = None
    for g in range(LANES // group):
        in_g = lane_group == g
        s = jnp.sum(jnp.where(in_g, sq, 0.0), axis=1, keepdims=True)
        ss = s if ss is None else jnp.where(in_g, s, ss)
    return t * lax.rsqrt(ss + group * EPS) * gain_row


def _proj_kernel(x_ref, pos_ref, ng_ref, w_in_ref, gq_ref, gk_ref, gcq_ref, wuq_ref, gckv_ref,
                 wukv_ref, gqn_ref, gqr_ref, gkn_ref, gkr_ref, freq_ref, spread_ref, sgn_ref,
                 qa_ref, ka_ref, va_ref, ga_ref, qn_ref, qr_ref, kcat_ref, vt_ref, gb_ref,
                 *stage_refs, off):
    n_res = qa_ref.shape[2]
    tm = x_ref.shape[1]
    sub = tm // n_res
    n_pair = qa_ref.shape[1]
    a_width = n_pair * PAIR

    x = x_ref[0]
    h = (x * lax.rsqrt(jnp.mean(x * x, axis=-1, keepdims=True) + EPS) * ng_ref[...]).astype(BF16)

    def seg(name, width):
        a = off[name]
        return jnp.dot(h, w_in_ref[:, a:a + width], preferred_element_type=F32)

    def emit_a(t, gain_ref, out_ref, rows_ref, quad_ref):
        quarter = tm // DEINTERLEAVE
        for p in range(n_pair):
            c = t[:, p * PAIR:(p + 1) * PAIR]
            if gain_ref is not None:
                c = _group_rms(c, gain_ref[...], HEAD_DIM)
            rows_ref[p] = c
            for g in range(DEINTERLEAVE):
                quad_ref[p, g * quarter:(g + 1) * quarter] = (
                    rows_ref[p, pl.ds(g, quarter, stride=DEINTERLEAVE), :])
            for r in range(n_res):
                g, h = r % DEINTERLEAVE, r // DEINTERLEAVE
                out_ref[0, p, r] = quad_ref[p, pl.ds(g * quarter + h, sub, stride=DEINTERLEAVE), :]

    ang = freq_ref[...] * pos_ref[0].astype(F32)

    def spread(table):
        hi = table.astype(BF16)
        lo = (table - hi.astype(F32)).astype(BF16)
        tn = lambda a: lax.dot_general(a, spread_ref[...], (((0,), (0,)), ((), ())),
                                       preferred_element_type=F32)
        return tn(hi) + tn(lo)

    cos_t = spread(jnp.cos(ang))
    sin_t = spread(jnp.sin(ang)) * sgn_ref[...]
    lane = lax.broadcasted_iota(jnp.int32, (tm, LANES), 1)
    first_half = (lane & (ROPE_DIM - 1)) < (ROPE_DIM // 2)

    def rope(t):
        partner = jnp.where(first_half, pltpu.roll(t, LANES - ROPE_DIM // 2, 1),
                            pltpu.roll(t, ROPE_DIM // 2, 1))
        return t * cos_t + partner * sin_t

    def rms(t, gain_ref):
        return t * lax.rsqrt(jnp.mean(t * t, axis=-1, keepdims=True) + EPS) * gain_ref[...]

    s_qa = seg("qa", a_width)
    s_ka = seg("ka", a_width)
    emit_a(s_qa, gq_ref, qa_ref, *stage_refs)
    s_cq = seg("cq", Q_RANK)
    s_ckv = seg("ckv", KV_RANK)
    s_kr = seg("kr", LANES)
    emit_a(s_ka, gk_ref, ka_ref, *stage_refs)
    q = jnp.dot(rms(s_cq, gcq_ref).astype(BF16), wuq_ref[...], preferred_element_type=F32)
    kv = jnp.dot(rms(s_ckv, gckv_ref).astype(BF16), wukv_ref[...], preferred_element_type=F32)
    s_za = seg("za", a_width)

    for p in range(n_pair):
        qn_ref[0, p] = _group_rms(q[:, p * PAIR:(p + 1) * PAIR], gqn_ref[...],
                                  HEAD_DIM).astype(BF16)
    for t in range(qr_ref.shape[2] // LANES):
        tile = q[:, a_width + t * LANES:a_width + (t + 1) * LANES]
        qr_ref[0, :, t * LANES:(t + 1) * LANES] = rope(
            _group_rms(tile, gqr_ref[...], ROPE_DIM)).astype(BF16)
    s_zb = seg("zb", a_width)

    kr = rope(rms(s_kr, gkr_ref)).astype(BF16)
    for p in range(n_pair):
        kn = _group_rms(kv[:, p * PAIR:(p + 1) * PAIR], gkn_ref[...], HEAD_DIM)
        kcat_ref[0, p, :, 0:LANES] = kn.astype(BF16)
        kcat_ref[0, p, :, LANES:2 * LANES] = kr
        vt_ref[0, p] = kv[:, a_width + p * PAIR:a_width + (p + 1) * PAIR].T.astype(BF16)

    ga_ref[0] = _silu(s_za).astype(BF16)
    s_va = seg("va", a_width)
    gb_ref[0] = _silu(s_zb).astype(BF16)
    emit_a(s_va, None, va_ref, *stage_refs)


def _input_projection(x, pos_row, ng, w_in_p, gq, gk, gcq, wuq_p, gckv, wukv_p, gqn, gqr, gkn, gkr,
                      freq, spread, sgn, off):
    b, s, d = x.shape
    tm = ROW_TILE
    n_pair = 4
    n_res = MAX_DIL
    sub = tm // n_res
    grid = (b, s // tm)

    def full(a):
        return pl.BlockSpec(a.shape, lambda i, j: (0,) * a.ndim)

    a_shape = jax.ShapeDtypeStruct((b, n_pair, n_res, s // n_res, PAIR), F32)
    a_spec = pl.BlockSpec((1, n_pair, n_res, sub, PAIR), lambda i, j: (i, 0, 0, j, 0))
    row_spec = lambda w: pl.BlockSpec((1, tm, w), lambda i, j: (i, j, 0))
    pair_spec = lambda w: pl.BlockSpec((1, n_pair, tm, w), lambda i, j: (i, 0, j, 0))

    out_shape = (
        a_shape, a_shape, a_shape,
        jax.ShapeDtypeStruct((b, s, n_pair * PAIR), BF16),
        jax.ShapeDtypeStruct((b, n_pair, s, PAIR), BF16),
        jax.ShapeDtypeStruct((b, s, 2 * LANES), BF16),
        jax.ShapeDtypeStruct((b, n_pair, s, 2 * LANES), BF16),
        jax.ShapeDtypeStruct((b, n_pair, PAIR, s), BF16),
        jax.ShapeDtypeStruct((b, s, n_pair * PAIR), BF16),
    )
    out_specs = (
        a_spec, a_spec, a_spec,
        row_spec(n_pair * PAIR),
        pair_spec(PAIR), row_spec(2 * LANES), pair_spec(2 * LANES),
        pl.BlockSpec((1, n_pair, PAIR, tm), lambda i, j: (i, 0, 0, j)),
        row_spec(n_pair * PAIR),
    )
    params = (ng, w_in_p, gq, gk, gcq, wuq_p, gckv, wukv_p, gqn, gqr, gkn, gkr, freq, spread, sgn)
    pos_spec = pl.BlockSpec((1, 1, tm), lambda i, j: (i, 0, j))
    in_specs = [row_spec(d), pos_spec] + [full(a) for a in params]
    stage = pltpu.VMEM((n_pair, tm, PAIR), F32)
    return pl.pallas_call(
        functools.partial(_proj_kernel, off=off),
        grid=grid,
        in_specs=in_specs,
        out_specs=out_specs,
        out_shape=out_shape,
        scratch_shapes=[stage] * 2,
        compiler_params=pltpu.CompilerParams(
            dimension_semantics=("arbitrary", "arbitrary"), vmem_limit_bytes=VMEM_LIMIT),
        name="input_projection",
    )(x, pos_row, *params)


DILATED_GROUP = 8


def _dilated_schedule(pat, i, g):
    if pat == 0:
        return 0, DILATED_GROUP * i + g
    if pat == 1:
        return g % 4, (DILATED_GROUP // 4) * i + g // 4
    return (DILATED_GROUP // 2) * i + g // 2, g % 2


def _dilated_kernel(q_ref, k_ref, v_ref, bias_ref, g_ref, out_ref, obuf, lbuf, mbuf, xo_ref):
    n_res = q_ref.shape[2]
    rows = q_ref.shape[3]
    lo = _lo_mask((BLOCK, PAIR))
    ones = jnp.ones((2 * BLOCK, LANES), BF16)

    def pattern_geometry(pat):
        dil = DILATIONS[pat]
        n_chunk = MAX_DIL // dil
        csize = BLOCK // n_chunk
        return dil, n_chunk, csize

    def load_block(pat, stream, n):
        dil, n_chunk, csize = pattern_geometry(pat)
        chunks = [(stream + dil * c, n * csize) for c in range(n_chunk)]
        first = n == 0

        def gather(ref, shift):
            parts = [ref[0, 0, r, pl.ds(pl.multiple_of(st - shift, 8), csize), :]
                     for r, st in chunks]
            return parts[0] if len(parts) == 1 else jnp.concatenate(parts, axis=0)

        prev_shift = jnp.where(first, 0, csize)
        qc = gather(q_ref, 0)
        lhs = jnp.concatenate([jnp.where(lo, qc, 0.0), jnp.where(lo, 0.0, qc)],
                              axis=0).astype(BF16)
        kk = jnp.concatenate([gather(k_ref, prev_shift), gather(k_ref, 0)], axis=0).astype(BF16)
        vv = jnp.concatenate([gather(v_ref, prev_shift), gather(v_ref, 0)], axis=0).astype(BF16)
        vv1 = jnp.concatenate([vv, ones], axis=1)
        bias = bias_ref[pat, jnp.where(first, 1, 0), 0]
        return chunks, csize, lhs, kk, vv1, bias

    def split_heads(r):
        return jnp.where(lo, r[:BLOCK], r[BLOCK:])

    def store_chunks(buf, lead, chunks, csize, val):
        for c, (r, st) in enumerate(chunks):
            buf[lead + (r, pl.ds(pl.multiple_of(st, 8), csize), slice(None))] = (
                val[c * csize:(c + 1) * csize])

    def fast_body(i, carry):
        slots = [(pat, g) for g in range(DILATED_GROUP) for pat in range(len(DILATIONS))]
        blocks = [load_block(pat, *_dilated_schedule(pat, i, g)) for pat, g in slots]
        ahead = 3
        scores = [_nt_dot(b[2], b[3]) for b in blocks[:ahead]]
        for k, (chunks, csize, _, _, vv1, bias) in enumerate(blocks):
            if k + ahead < len(blocks):
                nb = blocks[k + ahead]
                scores.append(_nt_dot(nb[2], nb[3]))
            p = jnp.exp2(scores[k] + bias).astype(BF16)
            r = jnp.dot(p, vv1, preferred_element_type=F32)
            lead = (slots[k][0],)
            store_chunks(obuf, lead, chunks, csize, split_heads(r[:, :LANES]))
            store_chunks(lbuf, lead, chunks, csize, split_heads(r[:, LANES:]))
        return carry

    lax.fori_loop(0, 32 // DILATED_GROUP, fast_body, 0)

    def pattern_sum(r, stats):
        num = obuf[0, r] + obuf[1, r] + obuf[2, r]
        den = lbuf[0, r] + lbuf[1, r] + lbuf[2, r]
        obuf[0, r] = num
        lbuf[0, r] = den
        den_lo, den_hi, num_hi = stats
        return (jnp.minimum(den_lo, jnp.min(den, axis=0, keepdims=True)),
                jnp.maximum(den_hi, jnp.max(den, axis=0, keepdims=True)),
                jnp.maximum(num_hi, jnp.max(jnp.abs(num), axis=0, keepdims=True)))

    row = lambda v: jnp.full((1, PAIR), v, F32)
    den_lo, den_hi, num_hi = lax.fori_loop(0, n_res, pattern_sum, (row(1.0), row(1.0), row(0.0)))
    unsafe = jnp.logical_not((jnp.min(den_lo) >= SAFE_MIN) & (jnp.max(den_hi) <= SAFE_MAX)
                             & (jnp.max(num_hi) <= SAFE_MAX))

    @pl.when(unsafe)
    def _():
        for pat in range(len(DILATIONS)):
            def body(i, carry, pat=pat):
                n_blocks = 32 // DILATIONS[pat]
                chunks, csize, lhs, kk, vv1, bias = load_block(pat, i // n_blocks, i % n_blocks)
                t = _nt_dot(lhs, kk) + bias
                m = jnp.max(t, axis=1, keepdims=True)
                p = jnp.exp2(t - m).astype(BF16)
                r = jnp.dot(p, vv1, preferred_element_type=F32)
                num_blk, den_blk = split_heads(r[:, :LANES]), split_heads(r[:, LANES:])
                m2 = jnp.where(lo, m[:BLOCK], m[BLOCK:])
                if pat == 0:
                    store_chunks(obuf, (0,), chunks, csize, num_blk)
                    store_chunks(lbuf, (0,), chunks, csize, den_blk)
                    store_chunks(mbuf, (), chunks, csize, m2)
                else:
                    for c, (r16, st) in enumerate(chunks):
                        rows_c = pl.ds(pl.multiple_of(st, 8), csize)
                        piece = slice(c * csize, (c + 1) * csize)
                        m_old = mbuf[r16, rows_c, :]
                        m_new = jnp.maximum(m_old, m2[piece])
                        a, b = jnp.exp2(m_old - m_new), jnp.exp2(m2[piece] - m_new)
                        obuf[0, r16, rows_c, :] = a * obuf[0, r16, rows_c, :] + b * num_blk[piece]
                        lbuf[0, r16, rows_c, :] = a * lbuf[0, r16, rows_c, :] + b * den_blk[piece]
                        mbuf[r16, rows_c, :] = m_new
                return carry

            lax.fori_loop(0, 32, body, 0)

    def merge(r, carry):
        xo_ref[pl.ds(r, rows, stride=n_res), :] = obuf[0, r] / lbuf[0, r]
        return carry

    lax.fori_loop(0, n_res, merge, 0)
    out_ref[0] = (xo_ref[...] * g_ref[0].astype(F32)).astype(BF16)


def _dilated_attention(qa, ka, va, bias, ga):
    b, n_pair, n_res, rows, _ = qa.shape
    s = n_res * rows
    a_spec = pl.BlockSpec((1, 1, n_res, rows, PAIR), lambda i, p: (i, p, 0, 0, 0))
    lane_spec = pl.BlockSpec((1, s, PAIR), lambda i, p: (i, 0, p))
    bias_spec = pl.BlockSpec((len(DILATIONS), 2, 1, 2 * BLOCK, 2 * BLOCK),
                             lambda i, p: (0, 0, p, 0, 0))
    buf = pltpu.VMEM((n_res, rows, PAIR), F32)
    pat_buf = pltpu.VMEM((len(DILATIONS), n_res, rows, PAIR), F32)
    return pl.pallas_call(
        _dilated_kernel,
        grid=(b, n_pair),
        in_specs=[a_spec, a_spec, a_spec, bias_spec, lane_spec],
        out_specs=lane_spec,
        out_shape=jax.ShapeDtypeStruct((b, s, n_pair * PAIR), BF16),
        scratch_shapes=[pat_buf, pat_buf, buf, pltpu.VMEM((s, PAIR), F32)],
        compiler_params=pltpu.CompilerParams(
            dimension_semantics=("arbitrary", "arbitrary"), vmem_limit_bytes=VMEM_LIMIT),
        name="dilated_attention",
    )(qa, ka, va, bias, ga)


def _mla_kernel(qn_ref, qr_ref, k_ref, vt_ref, g_ref, out_ref, qt_ref, acc_ref, l_ref):
    tq, tk, qs = MLA_Q_TILE, MLA_K_TILE, MLA_Q_SUB
    n_sub = tq // qs
    n_chain = 2 * n_sub
    s_len = k_ref.shape[2]
    lo_q = _lo_mask((qs, PAIR))
    rope_head = lax.broadcasted_iota(jnp.int32, (qs, LANES), 1) // ROPE_DIM
    even_head = (pl.program_id(1) % 2) * 2

    def scores(c, kb):
        return jnp.dot(kb, qt_ref[c], preferred_element_type=F32)

    def causal(st, key_offset):
        key = lax.broadcasted_iota(jnp.int32, st.shape, 0) + key_offset
        qry = lax.broadcasted_iota(jnp.int32, st.shape, 1)
        return jnp.where(key <= qry, st, NEG)

    def run_jobs(jobs, ls, assign):
        ahead = 4
        sts = [scores(c, kb) for c, kb, _, _ in jobs[:ahead]]
        for n, (c, kb, vtb, key_offset) in enumerate(jobs):
            if n + ahead < len(jobs):
                nc, nkb, _, _ = jobs[n + ahead]
                sts.append(scores(nc, nkb))
            st = sts[n] if key_offset is None else causal(sts[n], key_offset)
            p = jnp.exp2(st)
            col_sum = jnp.sum(p, axis=0, keepdims=True)
            pv = jnp.dot(vtb[c % 2], p.astype(BF16), preferred_element_type=F32)
            if c in assign:
                assign.remove(c)
                ls[c] = col_sum
                acc_ref[c] = pv
            else:
                ls[c] = ls[c] + col_sum
                acc_ref[c] += pv
        return ls

    def q_tile(qi, carry):
        row0 = pl.multiple_of(qi * tq, tq)
        for h in range(n_sub):
            r = pl.multiple_of(row0 + h * qs, qs)
            qn = qn_ref[0, 0, pl.ds(r, qs), :].astype(F32)
            qr = qr_ref[0, pl.ds(r, qs), :].astype(F32)
            q_even = jnp.concatenate(
                [jnp.where(lo_q, qn, 0.0), jnp.where(rope_head == even_head, qr, 0.0)], axis=1)
            q_odd = jnp.concatenate(
                [jnp.where(lo_q, 0.0, qn), jnp.where(rope_head == even_head + 1, qr, 0.0)], axis=1)
            qt_ref[2 * h] = q_even.T.astype(BF16)
            qt_ref[2 * h + 1] = q_odd.T.astype(BF16)

        def kv_block(start, size):
            start = pl.multiple_of(start, qs)
            vt = tuple(vt_ref[0, 0, hd * HEAD_DIM:(hd + 1) * HEAD_DIM, pl.ds(start, size)]
                       for hd in range(2))
            return k_ref[0, 0, pl.ds(start, size), :], vt

        jobs = []
        for h in range(n_sub):
            for k0 in range(0, (h + 1) * qs, tk):
                size = min(tk, (h + 1) * qs - k0)
                kb, vtb = kv_block(row0 + k0, size)
                offset = k0 - h * qs if k0 + size > h * qs else None
                jobs += [(2 * h, kb, vtb, offset), (2 * h + 1, kb, vtb, offset)]
        ls = run_jobs(jobs, [None] * n_chain, set(range(n_chain)))

        tiles_per_step = MLA_TILES_PER_STEP
        n_full = qi * (tq // tk)

        def full_step(j, ls_in):
            jobs = []
            for t in range(tiles_per_step):
                kb, vtb = kv_block((j * tiles_per_step + t) * tk, tk)
                jobs += [(c, kb, vtb, None) for c in range(n_chain)]
            return tuple(run_jobs(jobs, list(ls_in), set()))

        ls = lax.fori_loop(0, n_full // tiles_per_step, full_step, tuple(ls))
        for c in range(n_chain):
            l_ref[c] = jnp.broadcast_to(ls[c], (8, qs))
        den_lo = functools.reduce(jnp.minimum, [jnp.min(l) for l in ls])
        den_hi = functools.reduce(jnp.maximum, [jnp.max(l) for l in ls])
        num_hi = jnp.max(jnp.abs(acc_ref[...]))
        unsafe = jnp.logical_not((den_lo >= SAFE_MIN) & (den_hi <= SAFE_MAX) & (num_hi <= SAFE_MAX))

        @pl.when(unsafe)
        def _():
            def chain(c, carry_c):
                q_start = (c // 2) * qs

                def kv_step(j, st_in):
                    m, l = st_in
                    col0 = pl.multiple_of(j * tk, tk)
                    kb = k_ref[0, 0, pl.ds(col0, tk), :]
                    vtb = vt_ref[0, 0, pl.ds(pl.multiple_of((c % 2) * HEAD_DIM, HEAD_DIM), HEAD_DIM),
                                 pl.ds(col0, tk)]
                    st = causal(scores(c, kb), j * tk - row0 - q_start)
                    m_new = jnp.maximum(m, jnp.max(st, axis=0, keepdims=True))
                    alpha = jnp.exp2(m - m_new)
                    p = jnp.exp2(st - m_new)
                    pv = jnp.dot(vtb, p.astype(BF16), preferred_element_type=F32)
                    acc_ref[c] = alpha * acc_ref[c] + pv
                    return m_new, alpha * l + jnp.sum(p, axis=0, keepdims=True)

                acc_ref[c] = jnp.zeros((HEAD_DIM, qs), F32)
                init = (jnp.full((1, qs), NEG, F32), jnp.zeros((1, qs), F32))
                _, l = lax.fori_loop(0, n_full + tq // tk, kv_step, init)
                l_ref[c] = jnp.broadcast_to(l, (8, qs))
                return carry_c

            lax.fori_loop(0, n_chain, chain, 0)

        for h in range(n_sub):
            ot = jnp.concatenate([acc_ref[2 * h] / l_ref[2 * h, 0:1, :],
                                  acc_ref[2 * h + 1] / l_ref[2 * h + 1, 0:1, :]], axis=0)
            r = pl.multiple_of(row0 + h * qs, qs)
            gate = g_ref[0, pl.ds(r, qs), :].astype(F32)
            out_ref[0, pl.ds(r, qs), :] = (ot.T * gate).astype(BF16)
        return carry

    lax.fori_loop(0, s_len // tq, q_tile, 0)


def _latent_attention(qn, qr, kcat, vt, gb):
    b, n_pair, s, _ = qn.shape
    n_chain = 2 * (MLA_Q_TILE // MLA_Q_SUB)
    pair_spec = lambda w: pl.BlockSpec((1, 1, s, w), lambda i, p: (i, p, 0, 0))
    vt_spec = pl.BlockSpec((1, 1, PAIR, s), lambda i, p: (i, p, 0, 0))
    lane_spec = pl.BlockSpec((1, s, PAIR), lambda i, p: (i, 0, p))
    return pl.pallas_call(
        _mla_kernel,
        grid=(b, n_pair),
        in_specs=[pair_spec(PAIR), pl.BlockSpec((1, s, LANES), lambda i, p: (i, 0, p // 2)),
                  pair_spec(2 * LANES), vt_spec, lane_spec],
        out_specs=lane_spec,
        out_shape=jax.ShapeDtypeStruct((b, s, n_pair * PAIR), BF16),
        scratch_shapes=[pltpu.VMEM((n_chain, 2 * LANES, MLA_Q_SUB), BF16),
                        pltpu.VMEM((n_chain, HEAD_DIM, MLA_Q_SUB), F32),
                        pltpu.VMEM((n_chain, 8, MLA_Q_SUB), F32)],
        compiler_params=pltpu.CompilerParams(
            dimension_semantics=("arbitrary", "arbitrary"), vmem_limit_bytes=VMEM_LIMIT),
        name="latent_attention",
    )(qn, qr, kcat, vt, gb)


def _out_kernel(ma_ref, mb_ref, x_ref, wa_ref, wb_ref, out_ref):
    acc = jnp.dot(ma_ref[0], wa_ref[...], preferred_element_type=F32)
    acc = acc + jnp.dot(mb_ref[0], wb_ref[...], preferred_element_type=F32)
    out_ref[0] = x_ref[0] + acc


def _output_projection(ma, mb, x, wa, wb):
    b, s, d = x.shape
    tm = OUT_ROW_TILE
    row_spec = lambda w: pl.BlockSpec((1, tm, w), lambda i, j: (i, j, 0))
    full = lambda a: pl.BlockSpec(a.shape, lambda i, j: (0,) * a.ndim)
    return pl.pallas_call(
        _out_kernel,
        grid=(b, s // tm),
        in_specs=[row_spec(ma.shape[-1]), row_spec(mb.shape[-1]), row_spec(d), full(wa), full(wb)],
        out_specs=row_spec(d),
        out_shape=jax.ShapeDtypeStruct((b, s, d), x.dtype),
        compiler_params=pltpu.CompilerParams(
            dimension_semantics=("arbitrary", "arbitrary"), vmem_limit_bytes=VMEM_LIMIT),
        name="output_projection",
    )(ma, mb, x, wa, wb)


def _t5_bucket(dist):
    max_exact = REL_BUCKETS // 2
    d = np.maximum(dist.astype(np.float32), np.float32(1.0))
    large = max_exact + (np.log(d / np.float32(max_exact)) / np.float32(
        math.log(REL_MAX_DISTANCE / max_exact)) * np.float32(REL_BUCKETS - max_exact)
                         ).astype(np.int32)
    large = np.minimum(large, REL_BUCKETS - 1)
    return np.where(dist < max_exact, dist, large)


def _dilated_bias(rel_bias, n_heads):
    tiles = []
    for dil in DILATIONS:
        n_chunk = MAX_DIL // dil
        csize = BLOCK // n_chunk
        idx = np.arange(BLOCK)
        true_idx = n_chunk * (idx % csize) + idx // csize
        qi = true_idx[:, None]
        ki = np.concatenate([true_idx, true_idx + BLOCK])[None, :]
        j = qi + BLOCK - ki
        valid = (j >= 0) & (j <= BLOCK)
        bucket = _t5_bucket(np.maximum(j, 0) * dil)
        bias = jnp.zeros((n_heads,) + bucket.shape, F32)
        for bkt in np.unique(bucket[valid]):
            bias = jnp.where((bucket == bkt)[None],
                             rel_bias[int(bkt)].astype(F32)[:, None, None], bias)
        bias = bias * LOG2E
        normal = jnp.where(valid[None], bias, NEG)
        first = jnp.where((valid & (ki >= BLOCK))[None], bias, NEG)
        both = jnp.stack([normal, first], axis=0)
        tiles.append(both.reshape(2, n_heads // 2, 2 * BLOCK, 2 * BLOCK))
    return jnp.stack(tiles, axis=0)


def _layer(x, pos_row, rel_bias, norm_gain, w_in, a_q_gain, a_k_gain, q_c_gain, w_uq, kv_c_gain,
           w_ukv, qn_gain, qr_gain, kn_gain, kr_gain, w_out):
    d = x.shape[-1]
    n_heads = 8
    a_width = n_heads * HEAD_DIM
    row = lambda v: v.reshape(1, -1).astype(F32)

    splits = (a_width, a_width, a_width, a_width, Q_RANK, KV_RANK, ROPE_DIM, a_width)
    starts = np.concatenate([[0], np.cumsum(splits)])
    kr0, kr1 = int(starts[6]), int(starts[7])
    rope_rep = LANES // ROPE_DIM
    w_in_p = jnp.concatenate(
        [w_in[:, :kr0], jnp.tile(w_in[:, kr0:kr1], (1, rope_rep)), w_in[:, kr1:]],
        axis=1).astype(BF16)
    off = dict(qa=int(starts[0]), ka=int(starts[1]), va=int(starts[2]), za=int(starts[3]),
               cq=int(starts[4]), ckv=int(starts[5]), kr=kr0, zb=kr0 + LANES)

    qk_dim = HEAD_DIM + ROPE_DIM
    wq = w_uq.reshape(Q_RANK, n_heads, qk_dim)
    wuq_p = jnp.concatenate([wq[:, :, :HEAD_DIM].reshape(Q_RANK, -1),
                             wq[:, :, HEAD_DIM:].reshape(Q_RANK, -1)], axis=1).astype(BF16)
    wkv = w_ukv.reshape(KV_RANK, n_heads, 2 * HEAD_DIM)
    wukv_p = jnp.concatenate([wkv[:, :, :HEAD_DIM].reshape(KV_RANK, -1),
                              wkv[:, :, HEAD_DIM:].reshape(KV_RANK, -1)], axis=1).astype(BF16)

    scale_a = HEAD_DIM ** -0.5
    scale_b = qk_dim ** -0.5
    inv_freq = ROPE_THETA ** (-jnp.arange(0, ROPE_DIM, 2, dtype=F32) / ROPE_DIM)
    rope_row = lambda v: row(jnp.tile(v, rope_rep))
    half = ROPE_DIM // 2
    freq = inv_freq.reshape(half, 1)
    spread = (np.arange(LANES)[None, :] % half == np.arange(half)[:, None]).astype(np.float32)
    spread = jnp.asarray(spread, BF16)
    sgn = rope_row(jnp.concatenate([-jnp.ones(half, F32), jnp.ones(half, F32)]))
    head_norm, rope_norm = HEAD_DIM ** 0.5, ROPE_DIM ** 0.5

    qa, ka, va, ga, qn, qr, kcat, vt, gb = _input_projection(
        x, pos_row, row(norm_gain), w_in_p,
        row(jnp.tile(a_q_gain, 2) * (head_norm * scale_a * LOG2E)),
        row(jnp.tile(a_k_gain, 2) * head_norm),
        row(q_c_gain), wuq_p, row(kv_c_gain), wukv_p,
        row(jnp.tile(qn_gain, 2) * (head_norm * scale_b * LOG2E)),
        rope_row(qr_gain * (rope_norm * scale_b * LOG2E)),
        row(jnp.tile(kn_gain, 2) * head_norm), rope_row(kr_gain), freq, spread, sgn, off)

    mixed_a = _dilated_attention(qa, ka, va, _dilated_bias(rel_bias, n_heads), ga)
    mixed_b = _latent_attention(qn, qr, kcat, vt, gb)
    w_out_b = w_out.astype(BF16)
    return _output_projection(mixed_a, mixed_b, x, w_out_b[:a_width], w_out_b[a_width:])


def kernel(x, positions, rel_bias, norm_gain, w_in, a_q_gain, a_k_gain, q_c_gain, w_uq, kv_c_gain,
           w_ukv, qn_gain, qr_gain, kn_gain, kr_gain, w_out):
    pos_row = positions[:, None, :]
    for l in range(norm_gain.shape[0]):
        x = _layer(x, pos_row, rel_bias, norm_gain[l], w_in[l], a_q_gain[l], a_k_gain[l],
                   q_c_gain[l], w_uq[l], kv_c_gain[l], w_ukv[l], qn_gain[l], qr_gain[l],
                   kn_gain[l], kr_gain[l], w_out[l])
    return x
```

```python
import functools
import math

import jax
import jax.numpy as jnp
import numpy as np
from jax import lax
from jax.experimental import pallas as pl
from jax.experimental.pallas import tpu as pltpu

F32 = jnp.float32
BF16 = jnp.bfloat16

LANES = 128
HEAD_DIM = 64
PAIR = 2 * HEAD_DIM
ROPE_DIM = 32
KV_RANK = 128
Q_RANK = 256
BLOCK = 128
DILATIONS = (1, 4, 16)
MAX_DIL = 16
DEINTERLEAVE = 4
ROPE_THETA = 10000.0
REL_BUCKETS = 32
REL_MAX_DISTANCE = 2048
EPS = 1e-6
NEG = -1e30
SAFE_MIN, SAFE_MAX = 1e-30, 1e30
LOG2E = math.log2(math.e)

ROW_TILE = 1024
OUT_ROW_TILE = 1024
MLA_Q_TILE = 2048
MLA_TILES_PER_STEP = 2
MLA_K_TILE = 512
MLA_Q_SUB = 256

VMEM_LIMIT = 56 * 1024 * 1024


def _nt_dot(a, b):
    return lax.dot_general(a, b, (((1,), (1,)), ((), ())), preferred_element_type=F32)


def _lo_mask(shape):
    return lax.broadcasted_iota(jnp.int32, shape, len(shape) - 1) < HEAD_DIM


def _silu(z):
    return z * (1.0 / (1.0 + jnp.exp(-z)))


def _group_rms(t, gain_row, group):
    lane_group = lax.broadcasted_iota(jnp.int32, t.shape, 1) // group
    sq = t * t
    ss = None
    for g in range(LANES // group):
        in_g = lane_group == g
        s = jnp.sum(jnp.where(in_g, sq, 0.0), axis=1, keepdims=True)
        ss = s if ss is None else jnp.where(in_g, s, ss)
    return t * lax.rsqrt(ss + group * EPS) * gain_row


def _proj_kernel(x_ref, pos_ref, ng_ref, w_in_ref, gq_ref, gk_ref, gcq_ref, wuq_ref, gckv_ref,
                 wukv_ref, gqn_ref, gqr_ref, gkn_ref, gkr_ref, freq_ref, spread_ref, sgn_ref,
                 qa_ref, ka_ref, va_ref, ga_ref, qn_ref, qr_ref, kcat_ref, vt_ref, gb_ref,
                 *stage_refs, off):
    n_res = qa_ref.shape[2]
    tm = x_ref.shape[1]
    sub = tm // n_res
    n_pair = qa_ref.shape[1]
    a_width = n_pair * PAIR

    x = x_ref[0]
    h = (x * lax.rsqrt(jnp.mean(x * x, axis=-1, keepdims=True) + EPS) * ng_ref[...]).astype(BF16)

    def seg(name, width):
        a = off[name]
        return jnp.dot(h, w_in_ref[:, a:a + width], preferred_element_type=F32)

    def emit_a(t, gain_ref, out_ref, rows_ref, quad_ref):
        quarter = tm // DEINTERLEAVE
        for p in range(n_pair):
            c = t[:, p * PAIR:(p + 1) * PAIR]
            if gain_ref is not None:
                c = _group_rms(c, gain_ref[...], HEAD_DIM)
            rows_ref[p] = c
            for g in range(DEINTERLEAVE):
                quad_ref[p, g * quarter:(g + 1) * quarter] = (
                    rows_ref[p, pl.ds(g, quarter, stride=DEINTERLEAVE), :])
            for r in range(n_res):
                g, h = r % DEINTERLEAVE, r // DEINTERLEAVE
                out_ref[0, p, r] = quad_ref[p, pl.ds(g * quarter + h, sub, stride=DEINTERLEAVE), :]

    ang = freq_ref[...] * pos_ref[0].astype(F32)

    def spread(table):
        hi = table.astype(BF16)
        lo = (table - hi.astype(F32)).astype(BF16)
        tn = lambda a: lax.dot_general(a, spread_ref[...], (((0,), (0,)), ((), ())),
                                       preferred_element_type=F32)
        return tn(hi) + tn(lo)

    cos_t = spread(jnp.cos(ang))
    sin_t = spread(jnp.sin(ang)) * sgn_ref[...]
    lane = lax.broadcasted_iota(jnp.int32, (tm, LANES), 1)
    first_half = (lane & (ROPE_DIM - 1)) < (ROPE_DIM // 2)

    def rope(t):
        partner = jnp.where(first_half, pltpu.roll(t, LANES - ROPE_DIM // 2, 1),
                            pltpu.roll(t, ROPE_DIM // 2, 1))
        return t * cos_t + partner * sin_t

    def rms(t, gain_ref):
        return t * lax.rsqrt(jnp.mean(t * t, axis=-1, keepdims=True) + EPS) * gain_ref[...]

    s_qa = seg("qa", a_width)
    s_ka = seg("ka", a_width)
    emit_a(s_qa, gq_ref, qa_ref, *stage_refs)
    s_cq = seg("cq", Q_RANK)
    s_ckv = seg("ckv", KV_RANK)
    s_kr = seg("kr", LANES)
    emit_a(s_ka, gk_ref, ka_ref, *stage_refs)
    q = jnp.dot(rms(s_cq, gcq_ref).astype(BF16), wuq_ref[...], preferred_element_type=F32)
    kv = jnp.dot(rms(s_ckv, gckv_ref).astype(BF16), wukv_ref[...], preferred_element_type=F32)
    s_za = seg("za", a_width)

    for p in range(n_pair):
        qn_ref[0, p] = _group_rms(q[:, p * PAIR:(p + 1) * PAIR], gqn_ref[...],
                                  HEAD_DIM).astype(BF16)
    for t in range(qr_ref.shape[2] // LANES):
        tile = q[:, a_width + t * LANES:a_width + (t + 1) * LANES]
        qr_ref[0, :, t * LANES:(t + 1) * LANES] = rope(
            _group_rms(tile, gqr_ref[...], ROPE_DIM)).astype(BF16)
    s_zb = seg("zb", a_width)

    kr = rope(rms(s_kr, gkr_ref)).astype(BF16)
    for p in range(n_pair):
        kn = _group_rms(kv[:, p * PAIR:(p + 1) * PAIR], gkn_ref[...], HEAD_DIM)
        kcat_ref[0, p, :, 0:LANES] = kn.astype(BF16)
        kcat_ref[0, p, :, LANES:2 * LANES] = kr
        vt_ref[0, p] = kv[:, a_width + p * PAIR:a_width + (p + 1) * PAIR].T.astype(BF16)

    ga_ref[0] = _silu(s_za).astype(BF16)
    s_va = seg("va", a_width)
    gb_ref[0] = _silu(s_zb).astype(BF16)
    emit_a(s_va, None, va_ref, *stage_refs)


def _input_projection(x, pos_row, ng, w_in_p, gq, gk, gcq, wuq_p, gckv, wukv_p, gqn, gqr, gkn, gkr,
                      freq, spread, sgn, off):
    b, s, d = x.shape
    tm = ROW_TILE
    n_pair = 4
    n_res = MAX_DIL
    sub = tm // n_res
    grid = (b, s // tm)

    def full(a):
        return pl.BlockSpec(a.shape, lambda i, j: (0,) * a.ndim)

    a_shape = jax.ShapeDtypeStruct((b, n_pair, n_res, s // n_res, PAIR), F32)
    a_spec = pl.BlockSpec((1, n_pair, n_res, sub, PAIR), lambda i, j: (i, 0, 0, j, 0))
    row_spec = lambda w: pl.BlockSpec((1, tm, w), lambda i, j: (i, j, 0))
    pair_spec = lambda w: pl.BlockSpec((1, n_pair, tm, w), lambda i, j: (i, 0, j, 0))

    out_shape = (
        a_shape, a_shape, a_shape,
        jax.ShapeDtypeStruct((b, s, n_pair * PAIR), BF16),
        jax.ShapeDtypeStruct((b, n_pair, s, PAIR), BF16),
        jax.ShapeDtypeStruct((b, s, 2 * LANES), BF16),
        jax.ShapeDtypeStruct((b, n_pair, s, 2 * LANES), BF16),
        jax.ShapeDtypeStruct((b, n_pair, PAIR, s), BF16),
        jax.ShapeDtypeStruct((b, s, n_pair * PAIR), BF16),
    )
    out_specs = (
        a_spec, a_spec, a_spec,
        row_spec(n_pair * PAIR),
        pair_spec(PAIR), row_spec(2 * LANES), pair_spec(2 * LANES),
        pl.BlockSpec((1, n_pair, PAIR, tm), lambda i, j: (i, 0, 0, j)),
        row_spec(n_pair * PAIR),
    )
    params = (ng, w_in_p, gq, gk, gcq, wuq_p, gckv, wukv_p, gqn, gqr, gkn, gkr, freq, spread, sgn)
    pos_spec = pl.BlockSpec((1, 1, tm), lambda i, j: (i, 0, j))
    in_specs = [row_spec(d), pos_spec] + [full(a) for a in params]
    stage = pltpu.VMEM((n_pair, tm, PAIR), F32)
    return pl.pallas_call(
        functools.partial(_proj_kernel, off=off),
        grid=grid,
        in_specs=in_specs,
        out_specs=out_specs,
        out_shape=out_shape,
        scratch_shapes=[stage] * 2,
        compiler_params=pltpu.CompilerParams(
            dimension_semantics=("arbitrary", "arbitrary"), vmem_limit_bytes=VMEM_LIMIT),
        name="input_projection",
    )(x, pos_row, *params)


DILATED_GROUP = 8


def _dilated_schedule(pat, i, g):
    if pat == 0:
        return 0, DILATED_GROUP * i + g
    if pat == 1:
        return g % 4, (DILATED_GROUP // 4) * i + g // 4
    return (DILATED_GROUP // 2) * i + g // 2, g % 2


def _dilated_kernel(q_ref, k_ref, v_ref, bias_ref, g_ref, out_ref, obuf, lbuf, mbuf, xo_ref):
    n_res = q_ref.shape[2]
    rows = q_ref.shape[3]
    lo = _lo_mask((BLOCK, PAIR))
    ones = jnp.ones((2 * BLOCK, LANES), BF16)

    def pattern_geometry(pat):
        dil = DILATIONS[pat]
        n_chunk = MAX_DIL // dil
        csize = BLOCK // n_chunk
        return dil, n_chunk, csize

    def load_block(pat, stream, n):
        dil, n_chunk, csize = pattern_geometry(pat)
        chunks = [(stream + dil * c, n * csize) for c in range(n_chunk)]
        first = n == 0

        def gather(ref, shift):
            parts = [ref[0, 0, r, pl.ds(pl.multiple_of(st - shift, 8), csize), :]
                     for r, st in chunks]
            return parts[0] if len(parts) == 1 else jnp.concatenate(parts, axis=0)

        prev_shift = jnp.where(first, 0, csize)
        qc = gather(q_ref, 0)
        lhs = jnp.concatenate([jnp.where(lo, qc, 0.0), jnp.where(lo, 0.0, qc)],
                              axis=0).astype(BF16)
        kk = jnp.concatenate([gather(k_ref, prev_shift), gather(k_ref, 0)], axis=0).astype(BF16)
        vv = jnp.concatenate([gather(v_ref, prev_shift), gather(v_ref, 0)], axis=0).astype(BF16)
        vv1 = jnp.concatenate([vv, ones], axis=1)
        bias = bias_ref[pat, jnp.where(first, 1, 0), 0]
        return chunks, csize, lhs, kk, vv1, bias

    def split_heads(r):
        return jnp.where(lo, r[:BLOCK], r[BLOCK:])

    def store_chunks(buf, lead, chunks, csize, val):
        for c, (r, st) in enumerate(chunks):
            buf[lead + (r, pl.ds(pl.multiple_of(st, 8), csize), slice(None))] = (
                val[c * csize:(c + 1) * csize])

    def fast_body(i, carry):
        slots = [(pat, g) for g in range(DILATED_GROUP) for pat in range(len(DILATIONS))]
        blocks = [load_block(pat, *_dilated_schedule(pat, i, g)) for pat, g in slots]
        ahead = 3
        scores = [_nt_dot(b[2], b[3]) for b in blocks[:ahead]]
        for k, (chunks, csize, _, _, vv1, bias) in enumerate(blocks):
            if k + ahead < len(blocks):
                nb = blocks[k + ahead]
                scores.append(_nt_dot(nb[2], nb[3]))
            p = jnp.exp2(scores[k] + bias).astype(BF16)
            r = jnp.dot(p, vv1, preferred_element_type=F32)
            lead = (slots[k][0],)
            store_chunks(obuf, lead, chunks, csize, split_heads(r[:, :LANES]))
            store_chunks(lbuf, lead, chunks, csize, split_heads(r[:, LANES:]))
        return carry

    lax.fori_loop(0, 32 // DILATED_GROUP, fast_body, 0)

    def pattern_sum(r, stats):
        num = obuf[0, r] + obuf[1, r] + obuf[2, r]
        den = lbuf[0, r] + lbuf[1, r] + lbuf[2, r]
        obuf[0, r] = num
        lbuf[0, r] = den
        den_lo, den_hi, num_hi = stats
        return (jnp.minimum(den_lo, jnp.min(den, axis=0, keepdims=True)),
                jnp.maximum(den_hi, jnp.max(den, axis=0, keepdims=True)),
                jnp.maximum(num_hi, jnp.max(jnp.abs(num), axis=0, keepdims=True)))

    row = lambda v: jnp.full((1, PAIR), v, F32)
    den_lo, den_hi, num_hi = lax.fori_loop(0, n_res, pattern_sum, (row(1.0), row(1.0), row(0.0)))
    unsafe = jnp.logical_not((jnp.min(den_lo) >= SAFE_MIN) & (jnp.max(den_hi) <= SAFE_MAX)
                             & (jnp.max(num_hi) <= SAFE_MAX))

    @pl.when(unsafe)
    def _():
        for pat in range(len(DILATIONS)):
            def body(i, carry, pat=pat):
                n_blocks = 32 // DILATIONS[pat]
                chunks, csize, lhs, kk, vv1, bias = load_block(pat, i // n_blocks, i % n_blocks)
                t = _nt_dot(lhs, kk) + bias
                m = jnp.max(t, axis=1, keepdims=True)
                p = jnp.exp2(t - m).astype(BF16)
                r = jnp.dot(p, vv1, preferred_element_type=F32)
                num_blk, den_blk = split_heads(r[:, :LANES]), split_heads(r[:, LANES:])
                m2 = jnp.where(lo, m[:BLOCK], m[BLOCK:])
                if pat == 0:
                    store_chunks(obuf, (0,), chunks, csize, num_blk)
                    store_chunks(lbuf, (0,), chunks, csize, den_blk)
                    store_chunks(mbuf, (), chunks, csize, m2)
                else:
                    for c, (r16, st) in enumerate(chunks):
                        rows_c = pl.ds(pl.multiple_of(st, 8), csize)
                        piece = slice(c * csize, (c + 1) * csize)
                        m_old = mbuf[r16, rows_c, :]
                        m_new = jnp.maximum(m_old, m2[piece])
                        a, b = jnp.exp2(m_old - m_new), jnp.exp2(m2[piece] - m_new)
                        obuf[0, r16, rows_c, :] = a * obuf[0, r16, rows_c, :] + b * num_blk[piece]
                        lbuf[0, r16, rows_c, :] = a * lbuf[0, r16, rows_c, :] + b * den_blk[piece]
                        mbuf[r16, rows_c, :] = m_new
                return carry

            lax.fori_loop(0, 32, body, 0)

    def merge(r, carry):
        xo_ref[pl.ds(r, rows, stride=n_res), :] = obuf[0, r] / lbuf[0, r]
        return carry

    lax.fori_loop(0, n_res, merge, 0)
    out_ref[0] = (xo_ref[...] * g_ref[0].astype(F32)).astype(BF16)


def _dilated_attention(qa, ka, va, bias, ga):
    b, n_pair, n_res, rows, _ = qa.shape
    s = n_res * rows
    a_spec = pl.BlockSpec((1, 1, n_res, rows, PAIR), lambda i, p: (i, p, 0, 0, 0))
    lane_spec = pl.BlockSpec((1, s, PAIR), lambda i, p: (i, 0, p))
    bias_spec = pl.BlockSpec((len(DILATIONS), 2, 1, 2 * BLOCK, 2 * BLOCK),
                             lambda i, p: (0, 0, p, 0, 0))
    buf = pltpu.VMEM((n_res, rows, PAIR), F32)
    pat_buf = pltpu.VMEM((len(DILATIONS), n_res, rows, PAIR), F32)
    return pl.pallas_call(
        _dilated_kernel,
        grid=(b, n_pair),
        in_specs=[a_spec, a_spec, a_spec, bias_spec, lane_spec],
        out_specs=lane_spec,
        out_shape=jax.ShapeDtypeStruct((b, s, n_pair * PAIR), BF16),
        scratch_shapes=[pat_buf, pat_buf, buf, pltpu.VMEM((s, PAIR), F32)],
        compiler_params=pltpu.CompilerParams(
            dimension_semantics=("arbitrary", "arbitrary"), vmem_limit_bytes=VMEM_LIMIT),
        name="dilated_attention",
    )(qa, ka, va, bias, ga)


def _mla_kernel(qn_ref, qr_ref, k_ref, vt_ref, g_ref, out_ref, qt_ref, acc_ref, l_ref):
    tq, tk, qs = MLA_Q_TILE, MLA_K_TILE, MLA_Q_SUB
    n_sub = tq // qs
    n_chain = 2 * n_sub
    s_len = k_ref.shape[2]
    lo_q = _lo_mask((qs, PAIR))
    rope_head = lax.broadcasted_iota(jnp.int32, (qs, LANES), 1) // ROPE_DIM
    even_head = (pl.program_id(1) % 2) * 2

    def scores(c, kb):
        return jnp.dot(kb, qt_ref[c], preferred_element_type=F32)

    def causal(st, key_offset):
        key = lax.broadcasted_iota(jnp.int32, st.shape, 0) + key_offset
        qry = lax.broadcasted_iota(jnp.int32, st.shape, 1)
        return jnp.where(key <= qry, st, NEG)

    def run_jobs(jobs, ls, assign):
        ahead = 4
        sts = [scores(c, kb) for c, kb, _, _ in jobs[:ahead]]
        for n, (c, kb, vtb, key_offset) in enumerate(jobs):
            if n + ahead < len(jobs):
                nc, nkb, _, _ = jobs[n + ahead]
                sts.append(scores(nc, nkb))
            st = sts[n] if key_offset is None else causal(sts[n], key_offset)
            p = jnp.exp2(st)
            col_sum = jnp.sum(p, axis=0, keepdims=True)
            pv = jnp.dot(vtb[c % 2], p.astype(BF16), preferred_element_type=F32)
            if c in assign:
                assign.remove(c)
                ls[c] = col_sum
                acc_ref[c] = pv
            else:
                ls[c] = ls[c] + col_sum
                acc_ref[c] += pv
        return ls

    def q_tile(qi, carry):
        row0 = pl.multiple_of(qi * tq, tq)
        for h in range(n_sub):
            r = pl.multiple_of(row0 + h * qs, qs)
            qn = qn_ref[0, 0, pl.ds(r, qs), :].astype(F32)
            qr = qr_ref[0, pl.ds(r, qs), :].astype(F32)
            q_even = jnp.concatenate(
                [jnp.where(lo_q, qn, 0.0), jnp.where(rope_head == even_head, qr, 0.0)], axis=1)
            q_odd = jnp.concatenate(
                [jnp.where(lo_q, 0.0, qn), jnp.where(rope_head == even_head + 1, qr, 0.0)], axis=1)
            qt_ref[2 * h] = q_even.T.astype(BF16)
            qt_ref[2 * h + 1] = q_odd.T.astype(BF16)

        def kv_block(start, size):
            start = pl.multiple_of(start, qs)
            vt = tuple(vt_ref[0, 0, hd * HEAD_DIM:(hd + 1) * HEAD_DIM, pl.ds(start, size)]
                       for hd in range(2))
            return k_ref[0, 0, pl.ds(start, size), :], vt

        jobs = []
        for h in range(n_sub):
            for k0 in range(0, (h + 1) * qs, tk):
                size = min(tk, (h + 1) * qs - k0)
                kb, vtb = kv_block(row0 + k0, size)
                offset = k0 - h * qs if k0 + size > h * qs else None
                jobs += [(2 * h, kb, vtb, offset), (2 * h + 1, kb, vtb, offset)]
        ls = run_jobs(jobs, [None] * n_chain, set(range(n_chain)))

        tiles_per_step = MLA_TILES_PER_STEP
        n_full = qi * (tq // tk)

        def full_step(j, ls_in):
            jobs = []
            for t in range(tiles_per_step):
                kb, vtb = kv_block((j * tiles_per_step + t) * tk, tk)
                jobs += [(c, kb, vtb, None) for c in range(n_chain)]
            return tuple(run_jobs(jobs, list(ls_in), set()))

        ls = lax.fori_loop(0, n_full // tiles_per_step, full_step, tuple(ls))
        for c in range(n_chain):
            l_ref[c] = jnp.broadcast_to(ls[c], (8, qs))
        den_lo = functools.reduce(jnp.minimum, [jnp.min(l) for l in ls])
        den_hi = functools.reduce(jnp.maximum, [jnp.max(l) for l in ls])
        num_hi = jnp.max(jnp.abs(acc_ref[...]))
        unsafe = jnp.logical_not((den_lo >= SAFE_MIN) & (den_hi <= SAFE_MAX) & (num_hi <= SAFE_MAX))

        @pl.when(unsafe)
        def _():
            def chain(c, carry_c):
                q_start = (c // 2) * qs

                def kv_step(j, st_in):
                    m, l = st_in
                    col0 = pl.multiple_of(j * tk, tk)
                    kb = k_ref[0, 0, pl.ds(col0, tk), :]
                    vtb = vt_ref[0, 0, pl.ds(pl.multiple_of((c % 2) * HEAD_DIM, HEAD_DIM), HEAD_DIM),
                                 pl.ds(col0, tk)]
                    st = causal(scores(c, kb), j * tk - row0 - q_start)
                    m_new = jnp.maximum(m, jnp.max(st, axis=0, keepdims=True))
                    alpha = jnp.exp2(m - m_new)
                    p = jnp.exp2(st - m_new)
                    pv = jnp.dot(vtb, p.astype(BF16), preferred_element_type=F32)
                    acc_ref[c] = alpha * acc_ref[c] + pv
                    return m_new, alpha * l + jnp.sum(p, axis=0, keepdims=True)

                acc_ref[c] = jnp.zeros((HEAD_DIM, qs), F32)
                init = (jnp.full((1, qs), NEG, F32), jnp.zeros((1, qs), F32))
                _, l = lax.fori_loop(0, n_full + tq // tk, kv_step, init)
                l_ref[c] = jnp.broadcast_to(l, (8, qs))
                return carry_c

            lax.fori_loop(0, n_chain, chain, 0)

        for h in range(n_sub):
            ot = jnp.concatenate([acc_ref[2 * h] / l_ref[2 * h, 0:1, :],
                                  acc_ref[2 * h + 1] / l_ref[2 * h + 1, 0:1, :]], axis=0)
            r = pl.multiple_of(row0 + h * qs, qs)
            gate = g_ref[0, pl.ds(r, qs), :].astype(F32)
            out_ref[0, pl.ds(r, qs), :] = (ot.T * gate).astype(BF16)
        return carry

    lax.fori_loop(0, s_len // tq, q_tile, 0)


def _latent_attention(qn, qr, kcat, vt, gb):
    b, n_pair, s, _ = qn.shape
    n_chain = 2 * (MLA_Q_TILE // MLA_Q_SUB)
    pair_spec = lambda w: pl.BlockSpec((1, 1, s, w), lambda i, p: (i, p, 0, 0))
    vt_spec = pl.BlockSpec((1, 1, PAIR, s), lambda i, p: (i, p, 0, 0))
    lane_spec = pl.BlockSpec((1, s, PAIR), lambda i, p: (i, 0, p))
    return pl.pallas_call(
        _mla_kernel,
        grid=(b, n_pair),
        in_specs=[pair_spec(PAIR), pl.BlockSpec((1, s, LANES), lambda i, p: (i, 0, p // 2)),
                  pair_spec(2 * LANES), vt_spec, lane_spec],
        out_specs=lane_spec,
        out_shape=jax.ShapeDtypeStruct((b, s, n_pair * PAIR), BF16),
        scratch_shapes=[pltpu.VMEM((n_chain, 2 * LANES, MLA_Q_SUB), BF16),
                        pltpu.VMEM((n_chain, HEAD_DIM, MLA_Q_SUB), F32),
                        pltpu.VMEM((n_chain, 8, MLA_Q_SUB), F32)],
        compiler_params=pltpu.CompilerParams(
            dimension_semantics=("arbitrary", "arbitrary"), vmem_limit_bytes=VMEM_LIMIT),
        name="latent_attention",
    )(qn, qr, kcat, vt, gb)


def _out_kernel(ma_ref, mb_ref, x_ref, wa_ref, wb_ref, out_ref):
    acc = jnp.dot(ma_ref[0], wa_ref[...], preferred_element_type=F32)
    acc = acc + jnp.dot(mb_ref[0], wb_ref[...], preferred_element_type=F32)
    out_ref[0] = x_ref[0] + acc


def _output_projection(ma, mb, x, wa, wb):
    b, s, d = x.shape
    tm = OUT_ROW_TILE
    row_spec = lambda w: pl.BlockSpec((1, tm, w), lambda i, j: (i, j, 0))
    full = lambda a: pl.BlockSpec(a.shape, lambda i, j: (0,) * a.ndim)
    return pl.pallas_call(
        _out_kernel,
        grid=(b, s // tm),
        in_specs=[row_spec(ma.shape[-1]), row_spec(mb.shape[-1]), row_spec(d), full(wa), full(wb)],
        out_specs=row_spec(d),
        out_shape=jax.ShapeDtypeStruct((b, s, d), x.dtype),
        compiler_params=pltpu.CompilerParams(
            dimension_semantics=("arbitrary", "arbitrary"), vmem_limit_bytes=VMEM_LIMIT),
        name="output_projection",
    )(ma, mb, x, wa, wb)


def _t5_bucket(dist):
    max_exact = REL_BUCKETS // 2
    d = np.maximum(dist.astype(np.float32), np.float32(1.0))
    large = max_exact + (np.log(d / np.float32(max_exact)) / np.float32(
        math.log(REL_MAX_DISTANCE / max_exact)) * np.float32(REL_BUCKETS - max_exact)
                         ).astype(np.int32)
    large = np.minimum(large, REL_BUCKETS - 1)
    return np.where(dist < max_exact, dist, large)


def _dilated_bias(rel_bias, n_heads):
    onehot, keep = [], []
    for dil in DILATIONS:
        n_chunk = MAX_DIL // dil
        csize = BLOCK // n_chunk
        idx = np.arange(BLOCK)
        true_idx = n_chunk * (idx % csize) + idx // csize
        qi = true_idx[:, None]
        ki = np.concatenate([true_idx, true_idx + BLOCK])[None, :]
        j = qi + BLOCK - ki
        valid = (j >= 0) & (j <= BLOCK)
        bucket = _t5_bucket(np.maximum(j, 0) * dil)
        onehot.append(bucket[..., None] == np.arange(REL_BUCKETS))
        keep.append(np.stack([valid, valid & (ki >= BLOCK)]))
    onehot, keep = np.stack(onehot), np.stack(keep)
    table = rel_bias.astype(F32).T[None, :, None, None, :]
    bias = jnp.sum(jnp.where(onehot[:, None], table, 0.0), axis=-1) * LOG2E
    tiles = jnp.where(keep[:, :, None], bias[:, None], NEG)
    return tiles.reshape(len(DILATIONS), 2, n_heads // 2, 2 * BLOCK, 2 * BLOCK)


def _layer(x, pos_row, rel_bias, norm_gain, w_in, a_q_gain, a_k_gain, q_c_gain, w_uq, kv_c_gain,
           w_ukv, qn_gain, qr_gain, kn_gain, kr_gain, w_out):
    d = x.shape[-1]
    n_heads = 8
    a_width = n_heads * HEAD_DIM
    row = lambda v: v.reshape(1, -1).astype(F32)

    splits = (a_width, a_width, a_width, a_width, Q_RANK, KV_RANK, ROPE_DIM, a_width)
    starts = np.concatenate([[0], np.cumsum(splits)])
    kr0, kr1 = int(starts[6]), int(starts[7])
    rope_rep = LANES // ROPE_DIM
    w_in_p = jnp.concatenate(
        [w_in[:, :kr0], jnp.tile(w_in[:, kr0:kr1], (1, rope_rep)), w_in[:, kr1:]],
        axis=1).astype(BF16)
    off = dict(qa=int(starts[0]), ka=int(starts[1]), va=int(starts[2]), za=int(starts[3]),
               cq=int(starts[4]), ckv=int(starts[5]), kr=kr0, zb=kr0 + LANES)

    qk_dim = HEAD_DIM + ROPE_DIM
    wq = w_uq.reshape(Q_RANK, n_heads, qk_dim)
    wuq_p = jnp.concatenate([wq[:, :, :HEAD_DIM].reshape(Q_RANK, -1),
                             wq[:, :, HEAD_DIM:].reshape(Q_RANK, -1)], axis=1).astype(BF16)
    wkv = w_ukv.reshape(KV_RANK, n_heads, 2 * HEAD_DIM)
    wukv_p = jnp.concatenate([wkv[:, :, :HEAD_DIM].reshape(KV_RANK, -1),
                              wkv[:, :, HEAD_DIM:].reshape(KV_RANK, -1)], axis=1).astype(BF16)

    scale_a = HEAD_DIM ** -0.5
    scale_b = qk_dim ** -0.5
    inv_freq = ROPE_THETA ** (-jnp.arange(0, ROPE_DIM, 2, dtype=F32) / ROPE_DIM)
    rope_row = lambda v: row(jnp.tile(v, rope_rep))
    half = ROPE_DIM // 2
    freq = inv_freq.reshape(half, 1)
    spread = (np.arange(LANES)[None, :] % half == np.arange(half)[:, None]).astype(np.float32)
    spread = jnp.asarray(spread, BF16)
    sgn = rope_row(jnp.concatenate([-jnp.ones(half, F32), jnp.ones(half, F32)]))
    head_norm, rope_norm = HEAD_DIM ** 0.5, ROPE_DIM ** 0.5

    qa, ka, va, ga, qn, qr, kcat, vt, gb = _input_projection(
        x, pos_row, row(norm_gain), w_in_p,
        row(jnp.tile(a_q_gain, 2) * (head_norm * scale_a * LOG2E)),
        row(jnp.tile(a_k_gain, 2) * head_norm),
        row(q_c_gain), wuq_p, row(kv_c_gain), wukv_p,
        row(jnp.tile(qn_gain, 2) * (head_norm * scale_b * LOG2E)),
        rope_row(qr_gain * (rope_norm * scale_b * LOG2E)),
        row(jnp.tile(kn_gain, 2) * head_norm), rope_row(kr_gain), freq, spread, sgn, off)

    mixed_a = _dilated_attention(qa, ka, va, _dilated_bias(rel_bias, n_heads), ga)
    mixed_b = _latent_attention(qn, qr, kcat, vt, gb)
    w_out_b = w_out.astype(BF16)
    return _output_projection(mixed_a, mixed_b, x, w_out_b[:a_width], w_out_b[a_width:])


def kernel(x, positions, rel_bias, norm_gain, w_in, a_q_gain, a_k_gain, q_c_gain, w_uq, kv_c_gain,
           w_ukv, qn_gain, qr_gain, kn_gain, kr_gain, w_out):
    pos_row = positions[:, None, :]
    for l in range(norm_gain.shape[0]):
        x = _layer(x, pos_row, rel_bias, norm_gain[l], w_in[l], a_q_gain[l], a_k_gain[l],
                   q_c_gain[l], w_uq[l], kv_c_gain[l], w_ukv[l], qn_gain[l], qr_gain[l],
                   kn_gain[l], kr_gain[l], w_out[l])
    return x
```

```python
import functools
import math

import jax
import jax.numpy as jnp
import numpy as np
from jax import lax
from jax.experimental import pallas as pl
from jax.experimental.pallas import tpu as pltpu

F32 = jnp.float32
BF16 = jnp.bfloat16

LANES = 128
HEAD_DIM = 64
PAIR = 2 * HEAD_DIM
ROPE_DIM = 32
KV_RANK = 128
Q_RANK = 256
BLOCK = 128
DILATIONS = (1, 4, 16)
MAX_DIL = 16
DEINTERLEAVE = 4
ROPE_THETA = 10000.0
REL_BUCKETS = 32
REL_MAX_DISTANCE = 2048
EPS = 1e-6
NEG = -1e30
SAFE_MIN, SAFE_MAX = 1e-30, 1e30
LOG2E = math.log2(math.e)

ROW_TILE = 1024
OUT_ROW_TILE = 1024
MLA_Q_TILE = 2048
MLA_TILES_PER_STEP = 2
MLA_K_TILE = 512
MLA_Q_SUB = 256

VMEM_LIMIT = 56 * 1024 * 1024


def _nt_dot(a, b):
    return lax.dot_general(a, b, (((1,), (1,)), ((), ())), preferred_element_type=F32)


def _lo_mask(shape):
    return lax.broadcasted_iota(jnp.int32, shape, len(shape) - 1) < HEAD_DIM


def _silu(z):
    return z * (1.0 / (1.0 + jnp.exp(-z)))


def _group_rms(t, gain_row, group):
    lane_group = lax.broadcasted_iota(jnp.int32, t.shape, 1) // group
    sq = t * t
    ss = None
    for g in range(LANES // group):
        in_g = lane_group == g
        s = jnp.sum(jnp.where(in_g, sq, 0.0), axis=1, keepdims=True)
        ss = s if ss is None else jnp.where(in_g, s, ss)
    return t * lax.rsqrt(ss + group * EPS) * gain_row


def _proj_kernel(x_ref, pos_ref, ng_ref, w_in_ref, gq_ref, gk_ref, gcq_ref, wuq_ref, gckv_ref,
                 wukv_ref, gqn_ref, gqr_ref, gkn_ref, gkr_ref, freq_ref, spread_ref, sgn_ref,
                 qa_ref, ka_ref, va_ref, ga_ref, qn_ref, qr_ref, kcat_ref, vt_ref, gb_ref,
                 *stage_refs, off):
    n_res = qa_ref.shape[2]
    tm = x_ref.shape[1]
    sub = tm // n_res
    n_pair = qa_ref.shape[1]
    a_width = n_pair * PAIR

    x = x_ref[0]
    h = (x * lax.rsqrt(jnp.mean(x * x, axis=-1, keepdims=True) + EPS) * ng_ref[...]).astype(BF16)

    def seg(name, width):
        a = off[name]
        return jnp.dot(h, w_in_ref[:, a:a + width], preferred_element_type=F32)

    def emit_a(t, gain_ref, out_ref, rows_ref, quad_ref):
        quarter = tm // DEINTERLEAVE
        for p in range(n_pair):
            c = t[:, p * PAIR:(p + 1) * PAIR]
            if gain_ref is not None:
                c = _group_rms(c, gain_ref[...], HEAD_DIM)
            rows_ref[p] = c
            for g in range(DEINTERLEAVE):
                quad_ref[p, g * quarter:(g + 1) * quarter] = (
                    rows_ref[p, pl.ds(g, quarter, stride=DEINTERLEAVE), :])
            for r in range(n_res):
                g, h = r % DEINTERLEAVE, r // DEINTERLEAVE
                out_ref[0, p, r] = quad_ref[p, pl.ds(g * quarter + h, sub, stride=DEINTERLEAVE), :]

    ang = freq_ref[...] * pos_ref[0].astype(F32)

    def spread(table):
        hi = table.astype(BF16)
        lo = (table - hi.astype(F32)).astype(BF16)
        tn = lambda a: lax.dot_general(a, spread_ref[...], (((0,), (0,)), ((), ())),
                                       preferred_element_type=F32)
        return tn(hi) + tn(lo)

    cos_t = spread(jnp.cos(ang))
    sin_t = spread(jnp.sin(ang)) * sgn_ref[...]
    lane = lax.broadcasted_iota(jnp.int32, (tm, LANES), 1)
    first_half = (lane & (ROPE_DIM - 1)) < (ROPE_DIM // 2)

    def rope(t):
        partner = jnp.where(first_half, pltpu.roll(t, LANES - ROPE_DIM // 2, 1),
                            pltpu.roll(t, ROPE_DIM // 2, 1))
        return t * cos_t + partner * sin_t

    def rms(t, gain_ref):
        return t * lax.rsqrt(jnp.mean(t * t, axis=-1, keepdims=True) + EPS) * gain_ref[...]

    s_qa = seg("qa", a_width)
    s_ka = seg("ka", a_width)
    emit_a(s_qa, gq_ref, qa_ref, *stage_refs)
    s_cq = seg("cq", Q_RANK)
    s_ckv = seg("ckv", KV_RANK)
    s_kr = seg("kr", LANES)
    emit_a(s_ka, gk_ref, ka_ref, *stage_refs)
    q = jnp.dot(rms(s_cq, gcq_ref).astype(BF16), wuq_ref[...], preferred_element_type=F32)
    kv = jnp.dot(rms(s_ckv, gckv_ref).astype(BF16), wukv_ref[...], preferred_element_type=F32)
    s_za = seg("za", a_width)

    for p in range(n_pair):
        qn_ref[0, p] = _group_rms(q[:, p * PAIR:(p + 1) * PAIR], gqn_ref[...],
                                  HEAD_DIM).astype(BF16)
    for t in range(qr_ref.shape[2] // LANES):
        tile = q[:, a_width + t * LANES:a_width + (t + 1) * LANES]
        qr_ref[0, :, t * LANES:(t + 1) * LANES] = rope(
            _group_rms(tile, gqr_ref[...], ROPE_DIM)).astype(BF16)
    s_zb = seg("zb", a_width)

    kr = rope(rms(s_kr, gkr_ref)).astype(BF16)
    for p in range(n_pair):
        kn = _group_rms(kv[:, p * PAIR:(p + 1) * PAIR], gkn_ref[...], HEAD_DIM)
        kcat_ref[0, p, :, 0:LANES] = kn.astype(BF16)
        kcat_ref[0, p, :, LANES:2 * LANES] = kr
        vt_ref[0, p] = kv[:, a_width + p * PAIR:a_width + (p + 1) * PAIR].T.astype(BF16)

    ga_ref[0] = _silu(s_za).astype(BF16)
    s_va = seg("va", a_width)
    gb_ref[0] = _silu(s_zb).astype(BF16)
    emit_a(s_va, None, va_ref, *stage_refs)


def _input_projection(x, pos_row, ng, w_in_p, gq, gk, gcq, wuq_p, gckv, wukv_p, gqn, gqr, gkn, gkr,
                      freq, spread, sgn, off):
    b, s, d = x.shape
    tm = ROW_TILE
    n_pair = 4
    n_res = MAX_DIL
    sub = tm // n_res
    grid = (b, s // tm)

    def full(a):
        return pl.BlockSpec(a.shape, lambda i, j: (0,) * a.ndim)

    a_shape = jax.ShapeDtypeStruct((b, n_pair, n_res, s // n_res, PAIR), F32)
    a_spec = pl.BlockSpec((1, n_pair, n_res, sub, PAIR), lambda i, j: (i, 0, 0, j, 0))
    row_spec = lambda w: pl.BlockSpec((1, tm, w), lambda i, j: (i, j, 0))
    pair_spec = lambda w: pl.BlockSpec((1, n_pair, tm, w), lambda i, j: (i, 0, j, 0))

    out_shape = (
        a_shape, a_shape, a_shape,
        jax.ShapeDtypeStruct((b, s, n_pair * PAIR), BF16),
        jax.ShapeDtypeStruct((b, n_pair, s, PAIR), BF16),
        jax.ShapeDtypeStruct((b, s, 2 * LANES), BF16),
        jax.ShapeDtypeStruct((b, n_pair, s, 2 * LANES), BF16),
        jax.ShapeDtypeStruct((b, n_pair, PAIR, s), BF16),
        jax.ShapeDtypeStruct((b, s, n_pair * PAIR), BF16),
    )
    out_specs = (
        a_spec, a_spec, a_spec,
        row_spec(n_pair * PAIR),
        pair_spec(PAIR), row_spec(2 * LANES), pair_spec(2 * LANES),
        pl.BlockSpec((1, n_pair, PAIR, tm), lambda i, j: (i, 0, 0, j)),
        row_spec(n_pair * PAIR),
    )
    params = (ng, w_in_p, gq, gk, gcq, wuq_p, gckv, wukv_p, gqn, gqr, gkn, gkr, freq, spread, sgn)
    pos_spec = pl.BlockSpec((1, 1, tm), lambda i, j: (i, 0, j))
    in_specs = [row_spec(d), pos_spec] + [full(a) for a in params]
    stage = pltpu.VMEM((n_pair, tm, PAIR), F32)
    return pl.pallas_call(
        functools.partial(_proj_kernel, off=off),
        grid=grid,
        in_specs=in_specs,
        out_specs=out_specs,
        out_shape=out_shape,
        scratch_shapes=[stage] * 2,
        compiler_params=pltpu.CompilerParams(
            dimension_semantics=("arbitrary", "arbitrary"), vmem_limit_bytes=VMEM_LIMIT),
        name="input_projection",
    )(x, pos_row, *params)


DILATED_GROUP = 8


def _dilated_schedule(pat, i, g):
    if pat == 0:
        return 0, DILATED_GROUP * i + g
    if pat == 1:
        return g % 4, (DILATED_GROUP // 4) * i + g // 4
    return (DILATED_GROUP // 2) * i + g // 2, g % 2


def _dilated_kernel(q_ref, k_ref, v_ref, bias_ref, g_ref, out_ref, obuf, lbuf, mbuf, xo_ref):
    n_res = q_ref.shape[2]
    rows = q_ref.shape[3]
    lo = _lo_mask((BLOCK, PAIR))
    ones = jnp.ones((2 * BLOCK, LANES), BF16)

    def pattern_geometry(pat):
        dil = DILATIONS[pat]
        n_chunk = MAX_DIL // dil
        csize = BLOCK // n_chunk
        return dil, n_chunk, csize

    def load_block(pat, stream, n):
        dil, n_chunk, csize = pattern_geometry(pat)
        chunks = [(stream + dil * c, n * csize) for c in range(n_chunk)]
        first = n == 0

        def gather(ref, shift):
            parts = [ref[0, 0, r, pl.ds(pl.multiple_of(st - shift, 8), csize), :]
                     for r, st in chunks]
            return parts[0] if len(parts) == 1 else jnp.concatenate(parts, axis=0)

        prev_shift = jnp.where(first, 0, csize)
        qc = gather(q_ref, 0)
        lhs = jnp.concatenate([jnp.where(lo, qc, 0.0), jnp.where(lo, 0.0, qc)],
                              axis=0).astype(BF16)
        kk = jnp.concatenate([gather(k_ref, prev_shift), gather(k_ref, 0)], axis=0).astype(BF16)
        vv = jnp.concatenate([gather(v_ref, prev_shift), gather(v_ref, 0)], axis=0).astype(BF16)
        vv1 = jnp.concatenate([vv, ones], axis=1)
        bias = bias_ref[pat, jnp.where(first, 1, 0), 0]
        return chunks, csize, lhs, kk, vv1, bias

    def split_heads(r):
        return jnp.where(lo, r[:BLOCK], r[BLOCK:])

    def store_chunks(buf, lead, chunks, csize, val):
        for c, (r, st) in enumerate(chunks):
            buf[lead + (r, pl.ds(pl.multiple_of(st, 8), csize), slice(None))] = (
                val[c * csize:(c + 1) * csize])

    def fast_body(i, carry):
        slots = [(pat, g) for g in range(DILATED_GROUP) for pat in range(len(DILATIONS))]
        blocks = [load_block(pat, *_dilated_schedule(pat, i, g)) for pat, g in slots]
        ahead = 3
        scores = [_nt_dot(b[2], b[3]) for b in blocks[:ahead]]
        for k, (chunks, csize, _, _, vv1, bias) in enumerate(blocks):
            if k + ahead < len(blocks):
                nb = blocks[k + ahead]
                scores.append(_nt_dot(nb[2], nb[3]))
            p = jnp.exp2(scores[k] + bias).astype(BF16)
            r = jnp.dot(p, vv1, preferred_element_type=F32)
            lead = (slots[k][0],)
            store_chunks(obuf, lead, chunks, csize, split_heads(r[:, :LANES]))
            store_chunks(lbuf, lead, chunks, csize, split_heads(r[:, LANES:]))
        return carry

    lax.fori_loop(0, 32 // DILATED_GROUP, fast_body, 0)

    def to_positions(r, value):
        xo_ref[pl.ds(r, rows, stride=n_res), :] = value

    def pattern_sum(r, stats):
        num = obuf[0, r] + obuf[1, r] + obuf[2, r]
        den = lbuf[0, r] + lbuf[1, r] + lbuf[2, r]
        to_positions(r, num / den)
        den_lo, den_hi, num_hi = stats
        return (jnp.minimum(den_lo, jnp.min(den, axis=0, keepdims=True)),
                jnp.maximum(den_hi, jnp.max(den, axis=0, keepdims=True)),
                jnp.maximum(num_hi, jnp.max(jnp.abs(num), axis=0, keepdims=True)))

    row = lambda v: jnp.full((1, PAIR), v, F32)
    den_lo, den_hi, num_hi = lax.fori_loop(0, n_res, pattern_sum, (row(1.0), row(1.0), row(0.0)))
    unsafe = jnp.logical_not((jnp.min(den_lo) >= SAFE_MIN) & (jnp.max(den_hi) <= SAFE_MAX)
                             & (jnp.max(num_hi) <= SAFE_MAX))

    @pl.when(unsafe)
    def _():
        for pat in range(len(DILATIONS)):
            def body(i, carry, pat=pat):
                n_blocks = 32 // DILATIONS[pat]
                chunks, csize, lhs, kk, vv1, bias = load_block(pat, i // n_blocks, i % n_blocks)
                t = _nt_dot(lhs, kk) + bias
                m = jnp.max(t, axis=1, keepdims=True)
                p = jnp.exp2(t - m).astype(BF16)
                r = jnp.dot(p, vv1, preferred_element_type=F32)
                num_blk, den_blk = split_heads(r[:, :LANES]), split_heads(r[:, LANES:])
                m2 = jnp.where(lo, m[:BLOCK], m[BLOCK:])
                if pat == 0:
                    store_chunks(obuf, (0,), chunks, csize, num_blk)
                    store_chunks(lbuf, (0,), chunks, csize, den_blk)
                    store_chunks(mbuf, (), chunks, csize, m2)
                else:
                    for c, (r16, st) in enumerate(chunks):
                        rows_c = pl.ds(pl.multiple_of(st, 8), csize)
                        piece = slice(c * csize, (c + 1) * csize)
                        m_old = mbuf[r16, rows_c, :]
                        m_new = jnp.maximum(m_old, m2[piece])
                        a, b = jnp.exp2(m_old - m_new), jnp.exp2(m2[piece] - m_new)
                        obuf[0, r16, rows_c, :] = a * obuf[0, r16, rows_c, :] + b * num_blk[piece]
                        lbuf[0, r16, rows_c, :] = a * lbuf[0, r16, rows_c, :] + b * den_blk[piece]
                        mbuf[r16, rows_c, :] = m_new
                return carry

            lax.fori_loop(0, 32, body, 0)

        def merge(r, carry):
            to_positions(r, obuf[0, r] / lbuf[0, r])
            return carry

        lax.fori_loop(0, n_res, merge, 0)

    out_ref[0] = (xo_ref[...] * g_ref[0].astype(F32)).astype(BF16)


def _dilated_attention(qa, ka, va, bias, ga):
    b, n_pair, n_res, rows, _ = qa.shape
    s = n_res * rows
    a_spec = pl.BlockSpec((1, 1, n_res, rows, PAIR), lambda i, p: (i, p, 0, 0, 0))
    lane_spec = pl.BlockSpec((1, s, PAIR), lambda i, p: (i, 0, p))
    bias_spec = pl.BlockSpec((len(DILATIONS), 2, 1, 2 * BLOCK, 2 * BLOCK),
                             lambda i, p: (0, 0, p, 0, 0))
    buf = pltpu.VMEM((n_res, rows, PAIR), F32)
    pat_buf = pltpu.VMEM((len(DILATIONS), n_res, rows, PAIR), F32)
    return pl.pallas_call(
        _dilated_kernel,
        grid=(b, n_pair),
        in_specs=[a_spec, a_spec, a_spec, bias_spec, lane_spec],
        out_specs=lane_spec,
        out_shape=jax.ShapeDtypeStruct((b, s, n_pair * PAIR), BF16),
        scratch_shapes=[pat_buf, pat_buf, buf, pltpu.VMEM((s, PAIR), F32)],
        compiler_params=pltpu.CompilerParams(
            dimension_semantics=("arbitrary", "arbitrary"), vmem_limit_bytes=VMEM_LIMIT),
        name="dilated_attention",
    )(qa, ka, va, bias, ga)


def _mla_kernel(qn_ref, qr_ref, k_ref, vt_ref, g_ref, out_ref, qt_ref, acc_ref, l_ref):
    tq, tk, qs = MLA_Q_TILE, MLA_K_TILE, MLA_Q_SUB
    n_sub = tq // qs
    n_chain = 2 * n_sub
    s_len = k_ref.shape[2]
    lo_q = _lo_mask((qs, PAIR))
    rope_head = lax.broadcasted_iota(jnp.int32, (qs, LANES), 1) // ROPE_DIM
    even_head = (pl.program_id(1) % 2) * 2

    def scores(c, kb):
        return jnp.dot(kb, qt_ref[c], preferred_element_type=F32)

    def causal(st, key_offset):
        key = lax.broadcasted_iota(jnp.int32, st.shape, 0) + key_offset
        qry = lax.broadcasted_iota(jnp.int32, st.shape, 1)
        return jnp.where(key <= qry, st, NEG)

    def run_jobs(jobs, ls, assign):
        ahead = 4
        sts = [scores(c, kb) for c, kb, _, _ in jobs[:ahead]]
        for n, (c, kb, vtb, key_offset) in enumerate(jobs):
            if n + ahead < len(jobs):
                nc, nkb, _, _ = jobs[n + ahead]
                sts.append(scores(nc, nkb))
            st = sts[n] if key_offset is None else causal(sts[n], key_offset)
            p = jnp.exp2(st)
            col_sum = jnp.sum(p, axis=0, keepdims=True)
            pv = jnp.dot(vtb[c % 2], p.astype(BF16), preferred_element_type=F32)
            if c in assign:
                assign.remove(c)
                ls[c] = col_sum
                acc_ref[c] = pv
            else:
                ls[c] = ls[c] + col_sum
                acc_ref[c] += pv
        return ls

    def q_tile(qi, carry):
        row0 = pl.multiple_of(qi * tq, tq)
        for h in range(n_sub):
            r = pl.multiple_of(row0 + h * qs, qs)
            qn = qn_ref[0, 0, pl.ds(r, qs), :].astype(F32)
            qr = qr_ref[0, pl.ds(r, qs), :].astype(F32)
            q_even = jnp.concatenate(
                [jnp.where(lo_q, qn, 0.0), jnp.where(rope_head == even_head, qr, 0.0)], axis=1)
            q_odd = jnp.concatenate(
                [jnp.where(lo_q, 0.0, qn), jnp.where(rope_head == even_head + 1, qr, 0.0)], axis=1)
            qt_ref[2 * h] = q_even.T.astype(BF16)
            qt_ref[2 * h + 1] = q_odd.T.astype(BF16)

        def kv_block(start, size):
            start = pl.multiple_of(start, qs)
            vt = tuple(vt_ref[0, 0, hd * HEAD_DIM:(hd + 1) * HEAD_DIM, pl.ds(start, size)]
                       for hd in range(2))
            return k_ref[0, 0, pl.ds(start, size), :], vt

        jobs = []
        for h in range(n_sub):
            for k0 in range(0, (h + 1) * qs, tk):
                size = min(tk, (h + 1) * qs - k0)
                kb, vtb = kv_block(row0 + k0, size)
                offset = k0 - h * qs if k0 + size > h * qs else None
                jobs += [(2 * h, kb, vtb, offset), (2 * h + 1, kb, vtb, offset)]
        ls = run_jobs(jobs, [None] * n_chain, set(range(n_chain)))

        tiles_per_step = MLA_TILES_PER_STEP
        n_full = qi * (tq // tk)

        def full_step(j, ls_in):
            jobs = []
            for t in range(tiles_per_step):
                kb, vtb = kv_block((j * tiles_per_step + t) * tk, tk)
                jobs += [(c, kb, vtb, None) for c in range(n_chain)]
            return tuple(run_jobs(jobs, list(ls_in), set()))

        ls = lax.fori_loop(0, n_full // tiles_per_step, full_step, tuple(ls))
        for c in range(n_chain):
            l_ref[c] = jnp.broadcast_to(ls[c], (8, qs))
        den_lo = functools.reduce(jnp.minimum, [jnp.min(l) for l in ls])
        den_hi = functools.reduce(jnp.maximum, [jnp.max(l) for l in ls])
        num_hi = jnp.max(jnp.abs(acc_ref[...]))
        unsafe = jnp.logical_not((den_lo >= SAFE_MIN) & (den_hi <= SAFE_MAX) & (num_hi <= SAFE_MAX))

        @pl.when(unsafe)
        def _():
            def chain(c, carry_c):
                q_start = (c // 2) * qs

                def kv_step(j, st_in):
                    m, l = st_in
                    col0 = pl.multiple_of(j * tk, tk)
                    kb = k_ref[0, 0, pl.ds(col0, tk), :]
                    vtb = vt_ref[0, 0, pl.ds(pl.multiple_of((c % 2) * HEAD_DIM, HEAD_DIM), HEAD_DIM),
                                 pl.ds(col0, tk)]
                    st = causal(scores(c, kb), j * tk - row0 - q_start)
                    m_new = jnp.maximum(m, jnp.max(st, axis=0, keepdims=True))
                    alpha = jnp.exp2(m - m_new)
                    p = jnp.exp2(st - m_new)
                    pv = jnp.dot(vtb, p.astype(BF16), preferred_element_type=F32)
                    acc_ref[c] = alpha * acc_ref[c] + pv
                    return m_new, alpha * l + jnp.sum(p, axis=0, keepdims=True)

                acc_ref[c] = jnp.zeros((HEAD_DIM, qs), F32)
                init = (jnp.full((1, qs), NEG, F32), jnp.zeros((1, qs), F32))
                _, l = lax.fori_loop(0, n_full + tq // tk, kv_step, init)
                l_ref[c] = jnp.broadcast_to(l, (8, qs))
                return carry_c

            lax.fori_loop(0, n_chain, chain, 0)

        for h in range(n_sub):
            ot = jnp.concatenate([acc_ref[2 * h] / l_ref[2 * h, 0:1, :],
                                  acc_ref[2 * h + 1] / l_ref[2 * h + 1, 0:1, :]], axis=0)
            r = pl.multiple_of(row0 + h * qs, qs)
            gate = g_ref[0, pl.ds(r, qs), :].astype(F32)
            out_ref[0, pl.ds(r, qs), :] = (ot.T * gate).astype(BF16)
        return carry

    lax.fori_loop(0, s_len // tq, q_tile, 0)


def _latent_attention(qn, qr, kcat, vt, gb):
    b, n_pair, s, _ = qn.shape
    n_chain = 2 * (MLA_Q_TILE // MLA_Q_SUB)
    pair_spec = lambda w: pl.BlockSpec((1, 1, s, w), lambda i, p: (i, p, 0, 0))
    vt_spec = pl.BlockSpec((1, 1, PAIR, s), lambda i, p: (i, p, 0, 0))
    lane_spec = pl.BlockSpec((1, s, PAIR), lambda i, p: (i, 0, p))
    return pl.pallas_call(
        _mla_kernel,
        grid=(b, n_pair),
        in_specs=[pair_spec(PAIR), pl.BlockSpec((1, s, LANES), lambda i, p: (i, 0, p // 2)),
                  pair_spec(2 * LANES), vt_spec, lane_spec],
        out_specs=lane_spec,
        out_shape=jax.ShapeDtypeStruct((b, s, n_pair * PAIR), BF16),
        scratch_shapes=[pltpu.VMEM((n_chain, 2 * LANES, MLA_Q_SUB), BF16),
                        pltpu.VMEM((n_chain, HEAD_DIM, MLA_Q_SUB), F32),
                        pltpu.VMEM((n_chain, 8, MLA_Q_SUB), F32)],
        compiler_params=pltpu.CompilerParams(
            dimension_semantics=("arbitrary", "arbitrary"), vmem_limit_bytes=VMEM_LIMIT),
        name="latent_attention",
    )(qn, qr, kcat, vt, gb)


def _out_kernel(ma_ref, mb_ref, x_ref, wa_ref, wb_ref, out_ref):
    acc = jnp.dot(ma_ref[0], wa_ref[...], preferred_element_type=F32)
    acc = acc + jnp.dot(mb_ref[0], wb_ref[...], preferred_element_type=F32)
    out_ref[0] = x_ref[0] + acc


def _output_projection(ma, mb, x, wa, wb):
    b, s, d = x.shape
    tm = OUT_ROW_TILE
    row_spec = lambda w: pl.BlockSpec((1, tm, w), lambda i, j: (i, j, 0))
    full = lambda a: pl.BlockSpec(a.shape, lambda i, j: (0,) * a.ndim)
    return pl.pallas_call(
        _out_kernel,
        grid=(b, s // tm),
        in_specs=[row_spec(ma.shape[-1]), row_spec(mb.shape[-1]), row_spec(d), full(wa), full(wb)],
        out_specs=row_spec(d),
        out_shape=jax.ShapeDtypeStruct((b, s, d), x.dtype),
        compiler_params=pltpu.CompilerParams(
            dimension_semantics=("arbitrary", "arbitrary"), vmem_limit_bytes=VMEM_LIMIT),
        name="output_projection",
    )(ma, mb, x, wa, wb)


def _t5_bucket(dist):
    max_exact = REL_BUCKETS // 2
    d = np.maximum(dist.astype(np.float32), np.float32(1.0))
    large = max_exact + (np.log(d / np.float32(max_exact)) / np.float32(
        math.log(REL_MAX_DISTANCE / max_exact)) * np.float32(REL_BUCKETS - max_exact)
                         ).astype(np.int32)
    large = np.minimum(large, REL_BUCKETS - 1)
    return np.where(dist < max_exact, dist, large)


def _dilated_bias(rel_bias, n_heads):
    onehot, keep = [], []
    for dil in DILATIONS:
        n_chunk = MAX_DIL // dil
        csize = BLOCK // n_chunk
        idx = np.arange(BLOCK)
        true_idx = n_chunk * (idx % csize) + idx // csize
        qi = true_idx[:, None]
        ki = np.concatenate([true_idx, true_idx + BLOCK])[None, :]
        j = qi + BLOCK - ki
        valid = (j >= 0) & (j <= BLOCK)
        bucket = _t5_bucket(np.maximum(j, 0) * dil)
        onehot.append(np.arange(REL_BUCKETS)[:, None, None] == bucket)
        keep.append(np.stack([valid, valid & (ki >= BLOCK)]))
    onehot, keep = np.stack(onehot, axis=1), np.stack(keep)
    table = rel_bias.astype(F32)[:, None, :, None, None]
    bias = jnp.sum(jnp.where(onehot[:, :, None], table, 0.0), axis=0) * LOG2E
    tiles = jnp.where(keep[:, :, None], bias[:, None], NEG)
    return tiles.reshape(len(DILATIONS), 2, n_heads // 2, 2 * BLOCK, 2 * BLOCK)


def _layer(x, pos_row, rel_bias, norm_gain, w_in, a_q_gain, a_k_gain, q_c_gain, w_uq, kv_c_gain,
           w_ukv, qn_gain, qr_gain, kn_gain, kr_gain, w_out):
    d = x.shape[-1]
    n_heads = 8
    a_width = n_heads * HEAD_DIM
    row = lambda v: v.reshape(1, -1).astype(F32)

    splits = (a_width, a_width, a_width, a_width, Q_RANK, KV_RANK, ROPE_DIM, a_width)
    starts = np.concatenate([[0], np.cumsum(splits)])
    kr0, kr1 = int(starts[6]), int(starts[7])
    rope_rep = LANES // ROPE_DIM
    w_in_p = jnp.concatenate(
        [w_in[:, :kr0], jnp.tile(w_in[:, kr0:kr1], (1, rope_rep)), w_in[:, kr1:]],
        axis=1).astype(BF16)
    off = dict(qa=int(starts[0]), ka=int(starts[1]), va=int(starts[2]), za=int(starts[3]),
               cq=int(starts[4]), ckv=int(starts[5]), kr=kr0, zb=kr0 + LANES)

    qk_dim = HEAD_DIM + ROPE_DIM
    wq = w_uq.reshape(Q_RANK, n_heads, qk_dim)
    wuq_p = jnp.concatenate([wq[:, :, :HEAD_DIM].reshape(Q_RANK, -1),
                             wq[:, :, HEAD_DIM:].reshape(Q_RANK, -1)], axis=1).astype(BF16)
    wkv = w_ukv.reshape(KV_RANK, n_heads, 2 * HEAD_DIM)
    wukv_p = jnp.concatenate([wkv[:, :, :HEAD_DIM].reshape(KV_RANK, -1),
                              wkv[:, :, HEAD_DIM:].reshape(KV_RANK, -1)], axis=1).astype(BF16)

    scale_a = HEAD_DIM ** -0.5
    scale_b = qk_dim ** -0.5
    inv_freq = ROPE_THETA ** (-jnp.arange(0, ROPE_DIM, 2, dtype=F32) / ROPE_DIM)
    rope_row = lambda v: row(jnp.tile(v, rope_rep))
    half = ROPE_DIM // 2
    freq = inv_freq.reshape(half, 1)
    spread = (np.arange(LANES)[None, :] % half == np.arange(half)[:, None]).astype(np.float32)
    spread = jnp.asarray(spread, BF16)
    sgn = rope_row(jnp.concatenate([-jnp.ones(half, F32), jnp.ones(half, F32)]))
    head_norm, rope_norm = HEAD_DIM ** 0.5, ROPE_DIM ** 0.5

    qa, ka, va, ga, qn, qr, kcat, vt, gb = _input_projection(
        x, pos_row, row(norm_gain), w_in_p,
        row(jnp.tile(a_q_gain, 2) * (head_norm * scale_a * LOG2E)),
        row(jnp.tile(a_k_gain, 2) * head_norm),
        row(q_c_gain), wuq_p, row(kv_c_gain), wukv_p,
        row(jnp.tile(qn_gain, 2) * (head_norm * scale_b * LOG2E)),
        rope_row(qr_gain * (rope_norm * scale_b * LOG2E)),
        row(jnp.tile(kn_gain, 2) * head_norm), rope_row(kr_gain), freq, spread, sgn, off)

    mixed_a = _dilated_attention(qa, ka, va, _dilated_bias(rel_bias, n_heads), ga)
    mixed_b = _latent_attention(qn, qr, kcat, vt, gb)
    w_out_b = w_out.astype(BF16)
    return _output_projection(mixed_a, mixed_b, x, w_out_b[:a_width], w_out_b[a_width:])


def kernel(x, positions, rel_bias, norm_gain, w_in, a_q_gain, a_k_gain, q_c_gain, w_uq, kv_c_gain,
           w_ukv, qn_gain, qr_gain, kn_gain, kr_gain, w_out):
    pos_row = positions[:, None, :]
    for l in range(norm_gain.shape[0]):
        x = _layer(x, pos_row, rel_bias, norm_gain[l], w_in[l], a_q_gain[l], a_k_gain[l],
                   q_c_gain[l], w_uq[l], kv_c_gain[l], w_ukv[l], qn_gain[l], qr_gain[l],
                   kn_gain[l], kr_gain[l], w_out[l])
    return x
```

```python
import functools
import math

import jax
import jax.numpy as jnp
import numpy as np
from jax import lax
from jax.experimental import pallas as pl
from jax.experimental.pallas import tpu as pltpu

F32 = jnp.float32
BF16 = jnp.bfloat16

LANES = 128
HEAD_DIM = 64
PAIR = 2 * HEAD_DIM
ROPE_DIM = 32
KV_RANK = 128
Q_RANK = 256
BLOCK = 128
DILATIONS = (1, 4, 16)
MAX_DIL = 16
DEINTERLEAVE = 4
ROPE_THETA = 10000.0
REL_BUCKETS = 32
REL_MAX_DISTANCE = 2048
EPS = 1e-6
NEG = -1e30
SAFE_MIN, SAFE_MAX = 1e-30, 1e30
LOG2E = math.log2(math.e)

ROW_TILE = 1024
OUT_ROW_TILE = 1024
MLA_Q_TILE = 2048
MLA_TILES_PER_STEP = 2
MLA_K_TILE = 512
MLA_Q_SUB = 256

VMEM_LIMIT = 56 * 1024 * 1024


def _nt_dot(a, b):
    return lax.dot_general(a, b, (((1,), (1,)), ((), ())), preferred_element_type=F32)


def _lo_mask(shape):
    return lax.broadcasted_iota(jnp.int32, shape, len(shape) - 1) < HEAD_DIM


def _silu(z):
    return z * (1.0 / (1.0 + jnp.exp(-z)))


def _group_rms(t, gain_row, group):
    lane_group = lax.broadcasted_iota(jnp.int32, t.shape, 1) // group
    sq = t * t
    ss = None
    for g in range(LANES // group):
        in_g = lane_group == g
        s = jnp.sum(jnp.where(in_g, sq, 0.0), axis=1, keepdims=True)
        ss = s if ss is None else jnp.where(in_g, s, ss)
    return t * lax.rsqrt(ss + group * EPS) * gain_row


def _proj_kernel(x_ref, pos_ref, ng_ref, w_in_ref, gq_ref, gk_ref, gcq_ref, wuq_ref, gckv_ref,
                 wukv_ref, gqn_ref, gqr_ref, gkn_ref, gkr_ref, freq_ref, spread_ref, sgn_ref,
                 qa_ref, ka_ref, va_ref, ga_ref, qn_ref, qr_ref, kcat_ref, vt_ref, gb_ref,
                 *stage_refs, off):
    n_res = qa_ref.shape[2]
    tm = x_ref.shape[1]
    sub = tm // n_res
    n_pair = qa_ref.shape[1]
    a_width = n_pair * PAIR

    x = x_ref[0]
    h = (x * lax.rsqrt(jnp.mean(x * x, axis=-1, keepdims=True) + EPS) * ng_ref[...]).astype(BF16)

    def seg(name, width):
        a = off[name]
        return jnp.dot(h, w_in_ref[:, a:a + width], preferred_element_type=F32)

    def emit_a(t, gain_ref, out_ref, rows_ref, quad_ref):
        quarter = tm // DEINTERLEAVE
        for p in range(n_pair):
            c = t[:, p * PAIR:(p + 1) * PAIR]
            if gain_ref is not None:
                c = _group_rms(c, gain_ref[...], HEAD_DIM)
            rows_ref[p] = c
            for g in range(DEINTERLEAVE):
                quad_ref[p, g * quarter:(g + 1) * quarter] = (
                    rows_ref[p, pl.ds(g, quarter, stride=DEINTERLEAVE), :])
            for r in range(n_res):
                g, h = r % DEINTERLEAVE, r // DEINTERLEAVE
                out_ref[0, p, r] = quad_ref[p, pl.ds(g * quarter + h, sub, stride=DEINTERLEAVE), :]

    ang = freq_ref[...] * pos_ref[0].astype(F32)

    def spread(table):
        hi = table.astype(BF16)
        lo = (table - hi.astype(F32)).astype(BF16)
        tn = lambda a: lax.dot_general(a, spread_ref[...], (((0,), (0,)), ((), ())),
                                       preferred_element_type=F32)
        return tn(hi) + tn(lo)

    cos_t = spread(jnp.cos(ang))
    sin_t = spread(jnp.sin(ang)) * sgn_ref[...]
    lane = lax.broadcasted_iota(jnp.int32, (tm, LANES), 1)
    first_half = (lane & (ROPE_DIM - 1)) < (ROPE_DIM // 2)

    def rope(t):
        partner = jnp.where(first_half, pltpu.roll(t, LANES - ROPE_DIM // 2, 1),
                            pltpu.roll(t, ROPE_DIM // 2, 1))
        return t * cos_t + partner * sin_t

    def rms(t, gain_ref):
        return t * lax.rsqrt(jnp.mean(t * t, axis=-1, keepdims=True) + EPS) * gain_ref[...]

    s_qa = seg("qa", a_width)
    s_ka = seg("ka", a_width)
    emit_a(s_qa, gq_ref, qa_ref, *stage_refs)
    s_cq = seg("cq", Q_RANK)
    s_ckv = seg("ckv", KV_RANK)
    s_kr = seg("kr", LANES)
    emit_a(s_ka, gk_ref, ka_ref, *stage_refs)
    q = jnp.dot(rms(s_cq, gcq_ref).astype(BF16), wuq_ref[...], preferred_element_type=F32)
    kv = jnp.dot(rms(s_ckv, gckv_ref).astype(BF16), wukv_ref[...], preferred_element_type=F32)
    s_za = seg("za", a_width)

    for p in range(n_pair):
        qn_ref[0, p] = _group_rms(q[:, p * PAIR:(p + 1) * PAIR], gqn_ref[...],
                                  HEAD_DIM).astype(BF16)
    for t in range(qr_ref.shape[2] // LANES):
        tile = q[:, a_width + t * LANES:a_width + (t + 1) * LANES]
        qr_ref[0, :, t * LANES:(t + 1) * LANES] = rope(
            _group_rms(tile, gqr_ref[...], ROPE_DIM)).astype(BF16)
    s_zb = seg("zb", a_width)

    kr = rope(rms(s_kr, gkr_ref)).astype(BF16)
    for p in range(n_pair):
        kn = _group_rms(kv[:, p * PAIR:(p + 1) * PAIR], gkn_ref[...], HEAD_DIM)
        kcat_ref[0, p, :, 0:LANES] = kn.astype(BF16)
        kcat_ref[0, p, :, LANES:2 * LANES] = kr
        vt_ref[0, p] = kv[:, a_width + p * PAIR:a_width + (p + 1) * PAIR].T.astype(BF16)

    ga_ref[0] = _silu(s_za).astype(BF16)
    s_va = seg("va", a_width)
    gb_ref[0] = _silu(s_zb).astype(BF16)
    emit_a(s_va, None, va_ref, *stage_refs)


def _input_projection(x, pos_row, ng, w_in_p, gq, gk, gcq, wuq_p, gckv, wukv_p, gqn, gqr, gkn, gkr,
                      freq, spread, sgn, off):
    b, s, d = x.shape
    tm = ROW_TILE
    n_pair = 4
    n_res = MAX_DIL
    sub = tm // n_res
    grid = (b, s // tm)

    def full(a):
        return pl.BlockSpec(a.shape, lambda i, j: (0,) * a.ndim)

    a_shape = jax.ShapeDtypeStruct((b, n_pair, n_res, s // n_res, PAIR), F32)
    a_spec = pl.BlockSpec((1, n_pair, n_res, sub, PAIR), lambda i, j: (i, 0, 0, j, 0))
    row_spec = lambda w: pl.BlockSpec((1, tm, w), lambda i, j: (i, j, 0))
    pair_spec = lambda w: pl.BlockSpec((1, n_pair, tm, w), lambda i, j: (i, 0, j, 0))

    out_shape = (
        a_shape, a_shape, a_shape,
        jax.ShapeDtypeStruct((b, s, n_pair * PAIR), BF16),
        jax.ShapeDtypeStruct((b, n_pair, s, PAIR), BF16),
        jax.ShapeDtypeStruct((b, s, 2 * LANES), BF16),
        jax.ShapeDtypeStruct((b, n_pair, s, 2 * LANES), BF16),
        jax.ShapeDtypeStruct((b, n_pair, PAIR, s), BF16),
        jax.ShapeDtypeStruct((b, s, n_pair * PAIR), BF16),
    )
    out_specs = (
        a_spec, a_spec, a_spec,
        row_spec(n_pair * PAIR),
        pair_spec(PAIR), row_spec(2 * LANES), pair_spec(2 * LANES),
        pl.BlockSpec((1, n_pair, PAIR, tm), lambda i, j: (i, 0, 0, j)),
        row_spec(n_pair * PAIR),
    )
    params = (ng, w_in_p, gq, gk, gcq, wuq_p, gckv, wukv_p, gqn, gqr, gkn, gkr, freq, spread, sgn)
    pos_spec = pl.BlockSpec((1, 1, tm), lambda i, j: (i, 0, j))
    in_specs = [row_spec(d), pos_spec] + [full(a) for a in params]
    stage = pltpu.VMEM((n_pair, tm, PAIR), F32)
    return pl.pallas_call(
        functools.partial(_proj_kernel, off=off),
        grid=grid,
        in_specs=in_specs,
        out_specs=out_specs,
        out_shape=out_shape,
        scratch_shapes=[stage] * 2,
        compiler_params=pltpu.CompilerParams(
            dimension_semantics=("arbitrary", "arbitrary"), vmem_limit_bytes=VMEM_LIMIT),
        name="input_projection",
    )(x, pos_row, *params)


DILATED_GROUP = 8


def _dilated_schedule(pat, i, g):
    if pat == 0:
        return 0, DILATED_GROUP * i + g
    if pat == 1:
        return g % 4, (DILATED_GROUP // 4) * i + g // 4
    return (DILATED_GROUP // 2) * i + g // 2, g % 2


def _dilated_kernel(q_ref, k_ref, v_ref, bias_ref, g_ref, out_ref, obuf, lbuf, mbuf, xo_ref):
    n_res = q_ref.shape[2]
    rows = q_ref.shape[3]
    lo = _lo_mask((BLOCK, PAIR))
    ones = jnp.ones((2 * BLOCK, LANES), BF16)

    def pattern_geometry(pat):
        dil = DILATIONS[pat]
        n_chunk = MAX_DIL // dil
        csize = BLOCK // n_chunk
        return dil, n_chunk, csize

    def load_block(pat, stream, n):
        dil, n_chunk, csize = pattern_geometry(pat)
        chunks = [(stream + dil * c, n * csize) for c in range(n_chunk)]
        first = n == 0

        def gather(ref, shift):
            parts = [ref[0, 0, r, pl.ds(pl.multiple_of(st - shift, 8), csize), :]
                     for r, st in chunks]
            return parts[0] if len(parts) == 1 else jnp.concatenate(parts, axis=0)

        prev_shift = jnp.where(first, 0, csize)
        qc = gather(q_ref, 0)
        lhs = jnp.concatenate([jnp.where(lo, qc, 0.0), jnp.where(lo, 0.0, qc)],
                              axis=0).astype(BF16)
        kk = jnp.concatenate([gather(k_ref, prev_shift), gather(k_ref, 0)], axis=0).astype(BF16)
        vv = jnp.concatenate([gather(v_ref, prev_shift), gather(v_ref, 0)], axis=0).astype(BF16)
        vv1 = jnp.concatenate([vv, ones], axis=1)
        bias = bias_ref[pat, jnp.where(first, 1, 0), 0]
        return chunks, csize, lhs, kk, vv1, bias

    def split_heads(r):
        return jnp.where(lo, r[:BLOCK], r[BLOCK:])

    def store_chunks(buf, lead, chunks, csize, val):
        for c, (r, st) in enumerate(chunks):
            buf[lead + (r, pl.ds(pl.multiple_of(st, 8), csize), slice(None))] = (
                val[c * csize:(c + 1) * csize])

    def fast_body(i, carry):
        slots = [(pat, g) for g in range(DILATED_GROUP) for pat in range(len(DILATIONS))]
        blocks = [load_block(pat, *_dilated_schedule(pat, i, g)) for pat, g in slots]
        ahead = 3
        scores = [_nt_dot(b[2], b[3]) for b in blocks[:ahead]]
        for k, (chunks, csize, _, _, vv1, bias) in enumerate(blocks):
            if k + ahead < len(blocks):
                nb = blocks[k + ahead]
                scores.append(_nt_dot(nb[2], nb[3]))
            p = jnp.exp2(scores[k] + bias).astype(BF16)
            r = jnp.dot(p, vv1, preferred_element_type=F32)
            lead = (slots[k][0],)
            store_chunks(obuf, lead, chunks, csize, split_heads(r[:, :LANES]))
            store_chunks(lbuf, lead, chunks, csize, split_heads(r[:, LANES:]))
        return carry

    lax.fori_loop(0, 32 // DILATED_GROUP, fast_body, 0)

    def to_positions(r, value):
        xo_ref[pl.ds(r, rows, stride=n_res), :] = value

    def pattern_sum(r, stats):
        num = obuf[0, r] + obuf[1, r] + obuf[2, r]
        den = lbuf[0, r] + lbuf[1, r] + lbuf[2, r]
        to_positions(r, num / den)
        den_lo, den_hi, num_hi = stats
        return (jnp.minimum(den_lo, jnp.min(den, axis=0, keepdims=True)),
                jnp.maximum(den_hi, jnp.max(den, axis=0, keepdims=True)),
                jnp.maximum(num_hi, jnp.max(jnp.abs(num), axis=0, keepdims=True)))

    row = lambda v: jnp.full((1, PAIR), v, F32)
    den_lo, den_hi, num_hi = lax.fori_loop(0, n_res, pattern_sum, (row(1.0), row(1.0), row(0.0)))
    unsafe = jnp.logical_not((jnp.min(den_lo) >= SAFE_MIN) & (jnp.max(den_hi) <= SAFE_MAX)
                             & (jnp.max(num_hi) <= SAFE_MAX))

    @pl.when(unsafe)
    def _():
        for pat in range(len(DILATIONS)):
            def body(i, carry, pat=pat):
                n_blocks = 32 // DILATIONS[pat]
                chunks, csize, lhs, kk, vv1, bias = load_block(pat, i // n_blocks, i % n_blocks)
                t = _nt_dot(lhs, kk) + bias
                m = jnp.max(t, axis=1, keepdims=True)
                p = jnp.exp2(t - m).astype(BF16)
                r = jnp.dot(p, vv1, preferred_element_type=F32)
                num_blk, den_blk = split_heads(r[:, :LANES]), split_heads(r[:, LANES:])
                m2 = jnp.where(lo, m[:BLOCK], m[BLOCK:])
                if pat == 0:
                    store_chunks(obuf, (0,), chunks, csize, num_blk)
                    store_chunks(lbuf, (0,), chunks, csize, den_blk)
                    store_chunks(mbuf, (), chunks, csize, m2)
                else:
                    for c, (r16, st) in enumerate(chunks):
                        rows_c = pl.ds(pl.multiple_of(st, 8), csize)
                        piece = slice(c * csize, (c + 1) * csize)
                        m_old = mbuf[r16, rows_c, :]
                        m_new = jnp.maximum(m_old, m2[piece])
                        a, b = jnp.exp2(m_old - m_new), jnp.exp2(m2[piece] - m_new)
                        obuf[0, r16, rows_c, :] = a * obuf[0, r16, rows_c, :] + b * num_blk[piece]
                        lbuf[0, r16, rows_c, :] = a * lbuf[0, r16, rows_c, :] + b * den_blk[piece]
                        mbuf[r16, rows_c, :] = m_new
                return carry

            lax.fori_loop(0, 32, body, 0)

        def merge(r, carry):
            to_positions(r, obuf[0, r] / lbuf[0, r])
            return carry

        lax.fori_loop(0, n_res, merge, 0)

    out_ref[0] = (xo_ref[...] * g_ref[0].astype(F32)).astype(BF16)


def _dilated_attention(qa, ka, va, bias, ga):
    b, n_pair, n_res, rows, _ = qa.shape
    s = n_res * rows
    a_spec = pl.BlockSpec((1, 1, n_res, rows, PAIR), lambda i, p: (i, p, 0, 0, 0))
    lane_spec = pl.BlockSpec((1, s, PAIR), lambda i, p: (i, 0, p))
    bias_spec = pl.BlockSpec((len(DILATIONS), 2, 1, 2 * BLOCK, 2 * BLOCK),
                             lambda i, p: (0, 0, p, 0, 0))
    buf = pltpu.VMEM((n_res, rows, PAIR), F32)
    pat_buf = pltpu.VMEM((len(DILATIONS), n_res, rows, PAIR), F32)
    return pl.pallas_call(
        _dilated_kernel,
        grid=(b, n_pair),
        in_specs=[a_spec, a_spec, a_spec, bias_spec, lane_spec],
        out_specs=lane_spec,
        out_shape=jax.ShapeDtypeStruct((b, s, n_pair * PAIR), BF16),
        scratch_shapes=[pat_buf, pat_buf, buf, pltpu.VMEM((s, PAIR), F32)],
        compiler_params=pltpu.CompilerParams(
            dimension_semantics=("arbitrary", "arbitrary"), vmem_limit_bytes=VMEM_LIMIT),
        name="dilated_attention",
    )(qa, ka, va, bias, ga)


def _mla_kernel(qn_ref, qr_ref, k_ref, vt_ref, g_ref, out_ref, qt_ref, acc_ref, l_ref):
    tq, tk, qs = MLA_Q_TILE, MLA_K_TILE, MLA_Q_SUB
    n_sub = tq // qs
    n_chain = 2 * n_sub
    s_len = k_ref.shape[2]
    lo_q = _lo_mask((qs, PAIR))
    rope_head = lax.broadcasted_iota(jnp.int32, (qs, LANES), 1) // ROPE_DIM
    even_head = (pl.program_id(1) % 2) * 2

    def scores(c, kb):
        return jnp.dot(kb, qt_ref[c], preferred_element_type=F32)

    def causal(st, key_offset):
        key = lax.broadcasted_iota(jnp.int32, st.shape, 0) + key_offset
        qry = lax.broadcasted_iota(jnp.int32, st.shape, 1)
        return jnp.where(key <= qry, st, NEG)

    def run_jobs(jobs, ls, assign):
        ahead = 4
        sts = [scores(c, kb) for c, kb, _, _ in jobs[:ahead]]
        for n, (c, kb, vtb, key_offset) in enumerate(jobs):
            if n + ahead < len(jobs):
                nc, nkb, _, _ = jobs[n + ahead]
                sts.append(scores(nc, nkb))
            st = sts[n] if key_offset is None else causal(sts[n], key_offset)
            p = jnp.exp2(st)
            col_sum = jnp.sum(p, axis=0, keepdims=True)
            pv = jnp.dot(vtb[c % 2], p.astype(BF16), preferred_element_type=F32)
            if c in assign:
                assign.remove(c)
                ls[c] = col_sum
                acc_ref[c] = pv
            else:
                ls[c] = ls[c] + col_sum
                acc_ref[c] += pv
        return ls

    def q_tile(qi, carry):
        row0 = qi * tq
        for h in range(n_sub):
            r = pl.multiple_of(row0 + h * qs, qs)
            qn = qn_ref[0, 0, pl.ds(r, qs), :].astype(F32)
            qr = qr_ref[0, pl.ds(r, qs), :].astype(F32)
            q_even = jnp.concatenate(
                [jnp.where(lo_q, qn, 0.0), jnp.where(rope_head == even_head, qr, 0.0)], axis=1)
            q_odd = jnp.concatenate(
                [jnp.where(lo_q, 0.0, qn), jnp.where(rope_head == even_head + 1, qr, 0.0)], axis=1)
            qt_ref[2 * h] = q_even.T.astype(BF16)
            qt_ref[2 * h + 1] = q_odd.T.astype(BF16)

        def kv_block(start, size):
            start = pl.multiple_of(start, qs)
            vt = tuple(vt_ref[0, 0, hd * HEAD_DIM:(hd + 1) * HEAD_DIM, pl.ds(start, size)]
                       for hd in range(2))
            return k_ref[0, 0, pl.ds(start, size), :], vt

        jobs = []
        for h in range(n_sub):
            for k0 in range(0, (h + 1) * qs, tk):
                size = min(tk, (h + 1) * qs - k0)
                kb, vtb = kv_block(row0 + k0, size)
                offset = k0 - h * qs if k0 + size > h * qs else None
                jobs += [(2 * h, kb, vtb, offset), (2 * h + 1, kb, vtb, offset)]
        ls = run_jobs(jobs, [None] * n_chain, set(range(n_chain)))

        tiles_per_step = MLA_TILES_PER_STEP
        n_full = qi * (tq // tk)

        def full_step(j, ls_in):
            jobs = []
            for t in range(tiles_per_step):
                kb, vtb = kv_block((j * tiles_per_step + t) * tk, tk)
                jobs += [(c, kb, vtb, None) for c in range(n_chain)]
            return tuple(run_jobs(jobs, list(ls_in), set()))

        ls = lax.fori_loop(0, n_full // tiles_per_step, full_step, tuple(ls))
        for c in range(n_chain):
            l_ref[c] = jnp.broadcast_to(ls[c], (8, qs))
        den_lo = functools.reduce(jnp.minimum, [jnp.min(l) for l in ls])
        den_hi = functools.reduce(jnp.maximum, [jnp.max(l) for l in ls])
        num_hi = jnp.max(jnp.abs(acc_ref[...]))
        unsafe = jnp.logical_not((den_lo >= SAFE_MIN) & (den_hi <= SAFE_MAX) & (num_hi <= SAFE_MAX))

        @pl.when(unsafe)
        def _():
            def chain(c, carry_c):
                q_start = (c // 2) * qs

                def kv_step(j, st_in):
                    m, l = st_in
                    col0 = pl.multiple_of(j * tk, tk)
                    kb = k_ref[0, 0, pl.ds(col0, tk), :]
                    vtb = vt_ref[0, 0, pl.ds(pl.multiple_of((c % 2) * HEAD_DIM, HEAD_DIM), HEAD_DIM),
                                 pl.ds(col0, tk)]
                    st = causal(scores(c, kb), j * tk - row0 - q_start)
                    m_new = jnp.maximum(m, jnp.max(st, axis=0, keepdims=True))
                    alpha = jnp.exp2(m - m_new)
                    p = jnp.exp2(st - m_new)
                    pv = jnp.dot(vtb, p.astype(BF16), preferred_element_type=F32)
                    acc_ref[c] = alpha * acc_ref[c] + pv
                    return m_new, alpha * l + jnp.sum(p, axis=0, keepdims=True)

                acc_ref[c] = jnp.zeros((HEAD_DIM, qs), F32)
                init = (jnp.full((1, qs), NEG, F32), jnp.zeros((1, qs), F32))
                _, l = lax.fori_loop(0, n_full + tq // tk, kv_step, init)
                l_ref[c] = jnp.broadcast_to(l, (8, qs))
                return carry_c

            lax.fori_loop(0, n_chain, chain, 0)

        for h in range(n_sub):
            ot = jnp.concatenate([acc_ref[2 * h] / l_ref[2 * h, 0:1, :],
                                  acc_ref[2 * h + 1] / l_ref[2 * h + 1, 0:1, :]], axis=0)
            r = pl.multiple_of(row0 + h * qs, qs)
            gate = g_ref[0, pl.ds(r, qs), :].astype(F32)
            out_ref[0, pl.ds(r, qs), :] = (ot.T * gate).astype(BF16)
        return carry

    for qi in range(s_len // tq):
        q_tile(qi, 0)


def _latent_attention(qn, qr, kcat, vt, gb):
    b, n_pair, s, _ = qn.shape
    n_chain = 2 * (MLA_Q_TILE // MLA_Q_SUB)
    pair_spec = lambda w: pl.BlockSpec((1, 1, s, w), lambda i, p: (i, p, 0, 0))
    vt_spec = pl.BlockSpec((1, 1, PAIR, s), lambda i, p: (i, p, 0, 0))
    lane_spec = pl.BlockSpec((1, s, PAIR), lambda i, p: (i, 0, p))
    return pl.pallas_call(
        _mla_kernel,
        grid=(b, n_pair),
        in_specs=[pair_spec(PAIR), pl.BlockSpec((1, s, LANES), lambda i, p: (i, 0, p // 2)),
                  pair_spec(2 * LANES), vt_spec, lane_spec],
        out_specs=lane_spec,
        out_shape=jax.ShapeDtypeStruct((b, s, n_pair * PAIR), BF16),
        scratch_shapes=[pltpu.VMEM((n_chain, 2 * LANES, MLA_Q_SUB), BF16),
                        pltpu.VMEM((n_chain, HEAD_DIM, MLA_Q_SUB), F32),
                        pltpu.VMEM((n_chain, 8, MLA_Q_SUB), F32)],
        compiler_params=pltpu.CompilerParams(
            dimension_semantics=("arbitrary", "arbitrary"), vmem_limit_bytes=VMEM_LIMIT),
        name="latent_attention",
    )(qn, qr, kcat, vt, gb)


def _out_kernel(ma_ref, mb_ref, x_ref, w_ref, out_ref):
    a = ma_ref.shape[2]
    acc = jnp.dot(ma_ref[0], w_ref[0:a], preferred_element_type=F32)
    acc = acc + jnp.dot(mb_ref[0], w_ref[a:], preferred_element_type=F32)
    out_ref[0] = x_ref[0] + acc


def _output_projection(ma, mb, x, w):
    b, s, d = x.shape
    tm = OUT_ROW_TILE
    row_spec = lambda w: pl.BlockSpec((1, tm, w), lambda i, j: (i, j, 0))
    full = lambda a: pl.BlockSpec(a.shape, lambda i, j: (0,) * a.ndim)
    return pl.pallas_call(
        _out_kernel,
        grid=(b, s // tm),
        in_specs=[row_spec(ma.shape[-1]), row_spec(mb.shape[-1]), row_spec(d), full(w)],
        out_specs=row_spec(d),
        out_shape=jax.ShapeDtypeStruct((b, s, d), x.dtype),
        compiler_params=pltpu.CompilerParams(
            dimension_semantics=("arbitrary", "arbitrary"), vmem_limit_bytes=VMEM_LIMIT),
        name="output_projection",
    )(ma, mb, x, w)


def _t5_bucket(dist):
    max_exact = REL_BUCKETS // 2
    d = np.maximum(dist.astype(np.float32), np.float32(1.0))
    large = max_exact + (np.log(d / np.float32(max_exact)) / np.float32(
        math.log(REL_MAX_DISTANCE / max_exact)) * np.float32(REL_BUCKETS - max_exact)
                         ).astype(np.int32)
    large = np.minimum(large, REL_BUCKETS - 1)
    return np.where(dist < max_exact, dist, large)


def _dilated_bias(rel_bias, n_heads):
    onehot, keep = [], []
    for dil in DILATIONS:
        n_chunk = MAX_DIL // dil
        csize = BLOCK // n_chunk
        idx = np.arange(BLOCK)
        true_idx = n_chunk * (idx % csize) + idx // csize
        qi = true_idx[:, None]
        ki = np.concatenate([true_idx, true_idx + BLOCK])[None, :]
        j = qi + BLOCK - ki
        valid = (j >= 0) & (j <= BLOCK)
        bucket = _t5_bucket(np.maximum(j, 0) * dil)
        onehot.append(np.arange(REL_BUCKETS)[:, None, None] == bucket)
        keep.append(np.stack([valid, valid & (ki >= BLOCK)]))
    onehot, keep = np.stack(onehot, axis=1), np.stack(keep)
    table = rel_bias.astype(F32)[:, None, :, None, None]
    bias = jnp.sum(jnp.where(onehot[:, :, None], table, 0.0), axis=0) * LOG2E
    tiles = jnp.where(keep[:, :, None], bias[:, None], NEG)
    return tiles.reshape(len(DILATIONS), 2, n_heads // 2, 2 * BLOCK, 2 * BLOCK)


def _layer(x, pos_row, rel_bias, norm_gain, w_in, a_q_gain, a_k_gain, q_c_gain, w_uq, kv_c_gain,
           w_ukv, qn_gain, qr_gain, kn_gain, kr_gain, w_out):
    d = x.shape[-1]
    n_heads = 8
    a_width = n_heads * HEAD_DIM
    row = lambda v: v.reshape(1, -1).astype(F32)

    splits = (a_width, a_width, a_width, a_width, Q_RANK, KV_RANK, ROPE_DIM, a_width)
    starts = np.concatenate([[0], np.cumsum(splits)])
    kr0, kr1 = int(starts[6]), int(starts[7])
    rope_rep = LANES // ROPE_DIM
    w_in_p = jnp.concatenate(
        [w_in[:, :kr0], jnp.tile(w_in[:, kr0:kr1], (1, rope_rep)), w_in[:, kr1:]],
        axis=1).astype(BF16)
    off = dict(qa=int(starts[0]), ka=int(starts[1]), va=int(starts[2]), za=int(starts[3]),
               cq=int(starts[4]), ckv=int(starts[5]), kr=kr0, zb=kr0 + LANES)

    qk_dim = HEAD_DIM + ROPE_DIM
    wq = w_uq.reshape(Q_RANK, n_heads, qk_dim)
    wuq_p = jnp.concatenate([wq[:, :, :HEAD_DIM].reshape(Q_RANK, -1),
                             wq[:, :, HEAD_DIM:].reshape(Q_RANK, -1)], axis=1).astype(BF16)
    wkv = w_ukv.reshape(KV_RANK, n_heads, 2 * HEAD_DIM)
    wukv_p = jnp.concatenate([wkv[:, :, :HEAD_DIM].reshape(KV_RANK, -1),
                              wkv[:, :, HEAD_DIM:].reshape(KV_RANK, -1)], axis=1).astype(BF16)

    scale_a = HEAD_DIM ** -0.5
    scale_b = qk_dim ** -0.5
    inv_freq = ROPE_THETA ** (-jnp.arange(0, ROPE_DIM, 2, dtype=F32) / ROPE_DIM)
    rope_row = lambda v: row(jnp.tile(v, rope_rep))
    half = ROPE_DIM // 2
    freq = inv_freq.reshape(half, 1)
    spread = (np.arange(LANES)[None, :] % half == np.arange(half)[:, None]).astype(np.float32)
    spread = jnp.asarray(spread, BF16)
    sgn = rope_row(jnp.concatenate([-jnp.ones(half, F32), jnp.ones(half, F32)]))
    head_norm, rope_norm = HEAD_DIM ** 0.5, ROPE_DIM ** 0.5

    qa, ka, va, ga, qn, qr, kcat, vt, gb = _input_projection(
        x, pos_row, row(norm_gain), w_in_p,
        row(jnp.tile(a_q_gain, 2) * (head_norm * scale_a * LOG2E)),
        row(jnp.tile(a_k_gain, 2) * head_norm),
        row(q_c_gain), wuq_p, row(kv_c_gain), wukv_p,
        row(jnp.tile(qn_gain, 2) * (head_norm * scale_b * LOG2E)),
        rope_row(qr_gain * (rope_norm * scale_b * LOG2E)),
        row(jnp.tile(kn_gain, 2) * head_norm), rope_row(kr_gain), freq, spread, sgn, off)

    mixed_a = _dilated_attention(qa, ka, va, _dilated_bias(rel_bias, n_heads), ga)
    mixed_b = _latent_attention(qn, qr, kcat, vt, gb)
    return _output_projection(mixed_a, mixed_b, x, w_out.astype(BF16))


def kernel(x, positions, rel_bias, norm_gain, w_in, a_q_gain, a_k_gain, q_c_gain, w_uq, kv_c_gain,
           w_ukv, qn_gain, qr_gain, kn_gain, kr_gain, w_out):
    pos_row = positions[:, None, :]
    for l in range(norm_gain.shape[0]):
        x = _layer(x, pos_row, rel_bias, norm_gain[l], w_in[l], a_q_gain[l], a_k_gain[l],
                   q_c_gain[l], w_uq[l], kv_c_gain[l], w_ukv[l], qn_gain[l], qr_gain[l],
                   kn_gain[l], kr_gain[l], w_out[l])
    return x
```

```python
import functools
import math

import jax
import jax.numpy as jnp
import numpy as np
from jax import lax
from jax.experimental import pallas as pl
from jax.experimental.pallas import tpu as pltpu

F32 = jnp.float32
BF16 = jnp.bfloat16

LANES = 128
HEAD_DIM = 64
PAIR = 2 * HEAD_DIM
ROPE_DIM = 32
KV_RANK = 128
Q_RANK = 256
BLOCK = 128
DILATIONS = (1, 4, 16)
MAX_DIL = 16
DEINTERLEAVE = 4
ROPE_THETA = 10000.0
REL_BUCKETS = 32
REL_MAX_DISTANCE = 2048
EPS = 1e-6
NEG = -1e30
SAFE_MIN, SAFE_MAX = 1e-30, 1e30
LOG2E = math.log2(math.e)

ROW_TILE = 1024
OUT_ROW_TILE = 1024
MLA_Q_TILE = 2048
MLA_TILES_PER_STEP = 2
MLA_K_TILE = 512
MLA_Q_SUB = 256

VMEM_LIMIT = 56 * 1024 * 1024


def _nt_dot(a, b):
    return lax.dot_general(a, b, (((1,), (1,)), ((), ())), preferred_element_type=F32)


def _lo_mask(shape):
    return lax.broadcasted_iota(jnp.int32, shape, len(shape) - 1) < HEAD_DIM


def _silu(z):
    return z * (1.0 / (1.0 + jnp.exp(-z)))


def _group_rms(t, gain_row, group):
    lane_group = lax.broadcasted_iota(jnp.int32, t.shape, 1) // group
    sq = t * t
    ss = None
    for g in range(LANES // group):
        in_g = lane_group == g
        s = jnp.sum(jnp.where(in_g, sq, 0.0), axis=1, keepdims=True)
        ss = s if ss is None else jnp.where(in_g, s, ss)
    return t * lax.rsqrt(ss + group * EPS) * gain_row


def _proj_kernel(x_ref, pos_ref, ng_ref, w_in_ref, gq_ref, gk_ref, gcq_ref, wuq_ref, gckv_ref,
                 wukv_ref, gqn_ref, gqr_ref, gkn_ref, gkr_ref, freq_ref, spread_ref, sgn_ref,
                 qa_ref, ka_ref, va_ref, ga_ref, qn_ref, qr_ref, kcat_ref, vt_ref, gb_ref,
                 *stage_refs, off):
    n_res = qa_ref.shape[2]
    tm = x_ref.shape[1]
    sub = tm // n_res
    n_pair = qa_ref.shape[1]
    a_width = n_pair * PAIR

    x = x_ref[0]
    h = (x * lax.rsqrt(jnp.mean(x * x, axis=-1, keepdims=True) + EPS) * ng_ref[...]).astype(BF16)

    def seg(name, width):
        a = off[name]
        return jnp.dot(h, w_in_ref[:, a:a + width], preferred_element_type=F32)

    def emit_a(t, gain_ref, out_ref, rows_ref, quad_ref):
        quarter = tm // DEINTERLEAVE
        for p in range(n_pair):
            c = t[:, p * PAIR:(p + 1) * PAIR]
            if gain_ref is not None:
                c = _group_rms(c, gain_ref[...], HEAD_DIM)
            rows_ref[p] = c
            for g in range(DEINTERLEAVE):
                quad_ref[p, g * quarter:(g + 1) * quarter] = (
                    rows_ref[p, pl.ds(g, quarter, stride=DEINTERLEAVE), :])
            for r in range(n_res):
                g, h = r % DEINTERLEAVE, r // DEINTERLEAVE
                out_ref[0, p, r] = quad_ref[p, pl.ds(g * quarter + h, sub, stride=DEINTERLEAVE), :]

    ang = freq_ref[...] * pos_ref[0].astype(F32)

    def spread(table):
        hi = table.astype(BF16)
        lo = (table - hi.astype(F32)).astype(BF16)
        tn = lambda a: lax.dot_general(a, spread_ref[...], (((0,), (0,)), ((), ())),
                                       preferred_element_type=F32)
        return tn(hi) + tn(lo)

    cos_t = spread(jnp.cos(ang))
    sin_t = spread(jnp.sin(ang)) * sgn_ref[...]
    lane = lax.broadcasted_iota(jnp.int32, (tm, LANES), 1)
    first_half = (lane & (ROPE_DIM - 1)) < (ROPE_DIM // 2)

    def rope(t):
        partner = jnp.where(first_half, pltpu.roll(t, LANES - ROPE_DIM // 2, 1),
                            pltpu.roll(t, ROPE_DIM // 2, 1))
        return t * cos_t + partner * sin_t

    def rms(t, gain_ref):
        return t * lax.rsqrt(jnp.mean(t * t, axis=-1, keepdims=True) + EPS) * gain_ref[...]

    s_qa = seg("qa", a_width)
    s_ka = seg("ka", a_width)
    emit_a(s_qa, gq_ref, qa_ref, *stage_refs)
    s_cq = seg("cq", Q_RANK)
    s_ckv = seg("ckv", KV_RANK)
    s_kr = seg("kr", LANES)
    emit_a(s_ka, gk_ref, ka_ref, *stage_refs)
    q = jnp.dot(rms(s_cq, gcq_ref).astype(BF16), wuq_ref[...], preferred_element_type=F32)
    kv = jnp.dot(rms(s_ckv, gckv_ref).astype(BF16), wukv_ref[...], preferred_element_type=F32)
    s_za = seg("za", a_width)

    for p in range(n_pair):
        qn_ref[0, p] = _group_rms(q[:, p * PAIR:(p + 1) * PAIR], gqn_ref[...],
                                  HEAD_DIM).astype(BF16)
    for t in range(qr_ref.shape[2] // LANES):
        tile = q[:, a_width + t * LANES:a_width + (t + 1) * LANES]
        qr_ref[0, :, t * LANES:(t + 1) * LANES] = rope(
            _group_rms(tile, gqr_ref[...], ROPE_DIM)).astype(BF16)
    s_zb = seg("zb", a_width)

    kr = rope(rms(s_kr, gkr_ref)).astype(BF16)
    for p in range(n_pair):
        kn = _group_rms(kv[:, p * PAIR:(p + 1) * PAIR], gkn_ref[...], HEAD_DIM)
        kcat_ref[0, p, :, 0:LANES] = kn.astype(BF16)
        kcat_ref[0, p, :, LANES:2 * LANES] = kr
        vt_ref[0, p] = kv[:, a_width + p * PAIR:a_width + (p + 1) * PAIR].T.astype(BF16)

    ga_ref[0] = _silu(s_za).astype(BF16)
    s_va = seg("va", a_width)
    gb_ref[0] = _silu(s_zb).astype(BF16)
    emit_a(s_va, None, va_ref, *stage_refs)


def _input_projection(x, pos_row, ng, w_in_p, gq, gk, gcq, wuq_p, gckv, wukv_p, gqn, gqr, gkn, gkr,
                      freq, spread, sgn, off):
    b, s, d = x.shape
    tm = ROW_TILE
    n_pair = 4
    n_res = MAX_DIL
    sub = tm // n_res
    grid = (b, s // tm)

    def full(a):
        return pl.BlockSpec(a.shape, lambda i, j: (0,) * a.ndim)

    a_shape = jax.ShapeDtypeStruct((b, n_pair, n_res, s // n_res, PAIR), F32)
    a_spec = pl.BlockSpec((1, n_pair, n_res, sub, PAIR), lambda i, j: (i, 0, 0, j, 0))
    row_spec = lambda w: pl.BlockSpec((1, tm, w), lambda i, j: (i, j, 0))
    pair_spec = lambda w: pl.BlockSpec((1, n_pair, tm, w), lambda i, j: (i, 0, j, 0))

    out_shape = (
        a_shape, a_shape, a_shape,
        jax.ShapeDtypeStruct((b, s, n_pair * PAIR), BF16),
        jax.ShapeDtypeStruct((b, n_pair, s, PAIR), BF16),
        jax.ShapeDtypeStruct((b, s, 2 * LANES), BF16),
        jax.ShapeDtypeStruct((b, n_pair, s, 2 * LANES), BF16),
        jax.ShapeDtypeStruct((b, n_pair, PAIR, s), BF16),
        jax.ShapeDtypeStruct((b, s, n_pair * PAIR), BF16),
    )
    out_specs = (
        a_spec, a_spec, a_spec,
        row_spec(n_pair * PAIR),
        pair_spec(PAIR), row_spec(2 * LANES), pair_spec(2 * LANES),
        pl.BlockSpec((1, n_pair, PAIR, tm), lambda i, j: (i, 0, 0, j)),
        row_spec(n_pair * PAIR),
    )
    params = (ng, w_in_p, gq, gk, gcq, wuq_p, gckv, wukv_p, gqn, gqr, gkn, gkr, freq, spread, sgn)
    pos_spec = pl.BlockSpec((1, 1, tm), lambda i, j: (i, 0, j))
    in_specs = [row_spec(d), pos_spec] + [full(a) for a in params]
    stage = pltpu.VMEM((n_pair, tm, PAIR), F32)
    return pl.pallas_call(
        functools.partial(_proj_kernel, off=off),
        grid=grid,
        in_specs=in_specs,
        out_specs=out_specs,
        out_shape=out_shape,
        scratch_shapes=[stage] * 2,
        compiler_params=pltpu.CompilerParams(
            dimension_semantics=("arbitrary", "arbitrary"), vmem_limit_bytes=VMEM_LIMIT),
        name="input_projection",
    )(x, pos_row, *params)


DILATED_GROUP = 32


def _dilated_schedule(pat, i, g):
    if pat == 0:
        return 0, DILATED_GROUP * i + g
    if pat == 1:
        return g % 4, (DILATED_GROUP // 4) * i + g // 4
    return (DILATED_GROUP // 2) * i + g // 2, g % 2


def _dilated_kernel(q_ref, k_ref, v_ref, bias_ref, g_ref, out_ref, obuf, lbuf, mbuf, xo_ref):
    n_res = q_ref.shape[2]
    rows = q_ref.shape[3]
    lo = _lo_mask((BLOCK, PAIR))
    ones = jnp.ones((2 * BLOCK, LANES), BF16)

    def pattern_geometry(pat):
        dil = DILATIONS[pat]
        n_chunk = MAX_DIL // dil
        csize = BLOCK // n_chunk
        return dil, n_chunk, csize

    def load_block(pat, stream, n):
        dil, n_chunk, csize = pattern_geometry(pat)
        chunks = [(stream + dil * c, n * csize) for c in range(n_chunk)]
        first = n == 0

        def gather(ref, shift):
            parts = [ref[0, 0, r, pl.ds(pl.multiple_of(st - shift, 8), csize), :]
                     for r, st in chunks]
            return parts[0] if len(parts) == 1 else jnp.concatenate(parts, axis=0)

        prev_shift = jnp.where(first, 0, csize)
        qc = gather(q_ref, 0)
        lhs = jnp.concatenate([jnp.where(lo, qc, 0.0), jnp.where(lo, 0.0, qc)],
                              axis=0).astype(BF16)
        kk = jnp.concatenate([gather(k_ref, prev_shift), gather(k_ref, 0)], axis=0).astype(BF16)
        vv = jnp.concatenate([gather(v_ref, prev_shift), gather(v_ref, 0)], axis=0).astype(BF16)
        vv1 = jnp.concatenate([vv, ones], axis=1)
        bias = bias_ref[pat, jnp.where(first, 1, 0), 0]
        return chunks, csize, lhs, kk, vv1, bias

    def split_heads(r):
        return jnp.where(lo, r[:BLOCK], r[BLOCK:])

    def store_chunks(buf, lead, chunks, csize, val):
        for c, (r, st) in enumerate(chunks):
            buf[lead + (r, pl.ds(pl.multiple_of(st, 8), csize), slice(None))] = (
                val[c * csize:(c + 1) * csize])

    def fast_body(i, carry):
        slots = [(pat, g) for g in range(DILATED_GROUP) for pat in range(len(DILATIONS))]
        blocks = [load_block(pat, *_dilated_schedule(pat, i, g)) for pat, g in slots]
        ahead = 3
        scores = [_nt_dot(b[2], b[3]) for b in blocks[:ahead]]
        for k, (chunks, csize, _, _, vv1, bias) in enumerate(blocks):
            if k + ahead < len(blocks):
                nb = blocks[k + ahead]
                scores.append(_nt_dot(nb[2], nb[3]))
            p = jnp.exp2(scores[k] + bias).astype(BF16)
            r = jnp.dot(p, vv1, preferred_element_type=F32)
            lead = (slots[k][0],)
            store_chunks(obuf, lead, chunks, csize, split_heads(r[:, :LANES]))
            store_chunks(lbuf, lead, chunks, csize, split_heads(r[:, LANES:]))
        return carry

    lax.fori_loop(0, 32 // DILATED_GROUP, fast_body, 0)

    def to_positions(r, value):
        xo_ref[pl.ds(r, rows, stride=n_res), :] = value

    def pattern_sum(r, stats):
        num = obuf[0, r] + obuf[1, r] + obuf[2, r]
        den = lbuf[0, r] + lbuf[1, r] + lbuf[2, r]
        to_positions(r, num / den)
        den_lo, den_hi, num_hi = stats
        return (jnp.minimum(den_lo, jnp.min(den, axis=0, keepdims=True)),
                jnp.maximum(den_hi, jnp.max(den, axis=0, keepdims=True)),
                jnp.maximum(num_hi, jnp.max(jnp.abs(num), axis=0, keepdims=True)))

    row = lambda v: jnp.full((1, PAIR), v, F32)
    den_lo, den_hi, num_hi = lax.fori_loop(0, n_res, pattern_sum, (row(1.0), row(1.0), row(0.0)))
    unsafe = jnp.logical_not((jnp.min(den_lo) >= SAFE_MIN) & (jnp.max(den_hi) <= SAFE_MAX)
                             & (jnp.max(num_hi) <= SAFE_MAX))

    @pl.when(unsafe)
    def _():
        for pat in range(len(DILATIONS)):
            def body(i, carry, pat=pat):
                n_blocks = 32 // DILATIONS[pat]
                chunks, csize, lhs, kk, vv1, bias = load_block(pat, i // n_blocks, i % n_blocks)
                t = _nt_dot(lhs, kk) + bias
                m = jnp.max(t, axis=1, keepdims=True)
                p = jnp.exp2(t - m).astype(BF16)
                r = jnp.dot(p, vv1, preferred_element_type=F32)
                num_blk, den_blk = split_heads(r[:, :LANES]), split_heads(r[:, LANES:])
                m2 = jnp.where(lo, m[:BLOCK], m[BLOCK:])
                if pat == 0:
                    store_chunks(obuf, (0,), chunks, csize, num_blk)
                    store_chunks(lbuf, (0,), chunks, csize, den_blk)
                    store_chunks(mbuf, (), chunks, csize, m2)
                else:
                    for c, (r16, st) in enumerate(chunks):
                        rows_c = pl.ds(pl.multiple_of(st, 8), csize)
                        piece = slice(c * csize, (c + 1) * csize)
                        m_old = mbuf[r16, rows_c, :]
                        m_new = jnp.maximum(m_old, m2[piece])
                        a, b = jnp.exp2(m_old - m_new), jnp.exp2(m2[piece] - m_new)
                        obuf[0, r16, rows_c, :] = a * obuf[0, r16, rows_c, :] + b * num_blk[piece]
                        lbuf[0, r16, rows_c, :] = a * lbuf[0, r16, rows_c, :] + b * den_blk[piece]
                        mbuf[r16, rows_c, :] = m_new
                return carry

            lax.fori_loop(0, 32, body, 0)

        def merge(r, carry):
            to_positions(r, obuf[0, r] / lbuf[0, r])
            return carry

        lax.fori_loop(0, n_res, merge, 0)

    out_ref[0] = (xo_ref[...] * g_ref[0].astype(F32)).astype(BF16)


def _dilated_attention(qa, ka, va, bias, ga):
    b, n_pair, n_res, rows, _ = qa.shape
    s = n_res * rows
    a_spec = pl.BlockSpec((1, 1, n_res, rows, PAIR), lambda i, p: (i, p, 0, 0, 0))
    lane_spec = pl.BlockSpec((1, s, PAIR), lambda i, p: (i, 0, p))
    bias_spec = pl.BlockSpec((len(DILATIONS), 2, 1, 2 * BLOCK, 2 * BLOCK),
                             lambda i, p: (0, 0, p, 0, 0))
    buf = pltpu.VMEM((n_res, rows, PAIR), F32)
    pat_buf = pltpu.VMEM((len(DILATIONS), n_res, rows, PAIR), F32)
    return pl.pallas_call(
        _dilated_kernel,
        grid=(b, n_pair),
        in_specs=[a_spec, a_spec, a_spec, bias_spec, lane_spec],
        out_specs=lane_spec,
        out_shape=jax.ShapeDtypeStruct((b, s, n_pair * PAIR), BF16),
        scratch_shapes=[pat_buf, pat_buf, buf, pltpu.VMEM((s, PAIR), F32)],
        compiler_params=pltpu.CompilerParams(
            dimension_semantics=("arbitrary", "arbitrary"), vmem_limit_bytes=VMEM_LIMIT),
        name="dilated_attention",
    )(qa, ka, va, bias, ga)


def _mla_kernel(qn_ref, qr_ref, k_ref, vt_ref, g_ref, out_ref, qt_ref, acc_ref, l_ref):
    tq, tk, qs = MLA_Q_TILE, MLA_K_TILE, MLA_Q_SUB
    n_sub = tq // qs
    n_chain = 2 * n_sub
    s_len = k_ref.shape[2]
    lo_q = _lo_mask((qs, PAIR))
    rope_head = lax.broadcasted_iota(jnp.int32, (qs, LANES), 1) // ROPE_DIM
    even_head = (pl.program_id(1) % 2) * 2

    def scores(c, kb):
        return jnp.dot(kb, qt_ref[c], preferred_element_type=F32)

    def causal(st, key_offset):
        key = lax.broadcasted_iota(jnp.int32, st.shape, 0) + key_offset
        qry = lax.broadcasted_iota(jnp.int32, st.shape, 1)
        return jnp.where(key <= qry, st, NEG)

    def run_jobs(jobs, ls, assign):
        ahead = 4
        sts = [scores(c, kb) for c, kb, _, _ in jobs[:ahead]]
        for n, (c, kb, vtb, key_offset) in enumerate(jobs):
            if n + ahead < len(jobs):
                nc, nkb, _, _ = jobs[n + ahead]
                sts.append(scores(nc, nkb))
            st = sts[n] if key_offset is None else causal(sts[n], key_offset)
            p = jnp.exp2(st)
            col_sum = jnp.sum(p, axis=0, keepdims=True)
            pv = jnp.dot(vtb[c % 2], p.astype(BF16), preferred_element_type=F32)
            if c in assign:
                assign.remove(c)
                ls[c] = col_sum
                acc_ref[c] = pv
            else:
                ls[c] = ls[c] + col_sum
                acc_ref[c] += pv
        return ls

    def q_tile(qi, carry):
        row0 = qi * tq
        for h in range(n_sub):
            r = pl.multiple_of(row0 + h * qs, qs)
            qn = qn_ref[0, 0, pl.ds(r, qs), :].astype(F32)
            qr = qr_ref[0, pl.ds(r, qs), :].astype(F32)
            q_even = jnp.concatenate(
                [jnp.where(lo_q, qn, 0.0), jnp.where(rope_head == even_head, qr, 0.0)], axis=1)
            q_odd = jnp.concatenate(
                [jnp.where(lo_q, 0.0, qn), jnp.where(rope_head == even_head + 1, qr, 0.0)], axis=1)
            qt_ref[2 * h] = q_even.T.astype(BF16)
            qt_ref[2 * h + 1] = q_odd.T.astype(BF16)

        def kv_block(start, size):
            start = pl.multiple_of(start, qs)
            vt = tuple(vt_ref[0, 0, hd * HEAD_DIM:(hd + 1) * HEAD_DIM, pl.ds(start, size)]
                       for hd in range(2))
            return k_ref[0, 0, pl.ds(start, size), :], vt

        jobs = []
        for h in range(n_sub):
            for k0 in range(0, (h + 1) * qs, tk):
                size = min(tk, (h + 1) * qs - k0)
                kb, vtb = kv_block(row0 + k0, size)
                offset = k0 - h * qs if k0 + size > h * qs else None
                jobs += [(2 * h, kb, vtb, offset), (2 * h + 1, kb, vtb, offset)]
        ls = run_jobs(jobs, [None] * n_chain, set(range(n_chain)))

        tiles_per_step = MLA_TILES_PER_STEP
        n_full = qi * (tq // tk)

        def full_step(j, ls_in):
            jobs = []
            for t in range(tiles_per_step):
                kb, vtb = kv_block((j * tiles_per_step + t) * tk, tk)
                jobs += [(c, kb, vtb, None) for c in range(n_chain)]
            return tuple(run_jobs(jobs, list(ls_in), set()))

        ls = lax.fori_loop(0, n_full // tiles_per_step, full_step, tuple(ls))
        for c in range(n_chain):
            l_ref[c] = jnp.broadcast_to(ls[c], (8, qs))
        den_lo = functools.reduce(jnp.minimum, [jnp.min(l) for l in ls])
        den_hi = functools.reduce(jnp.maximum, [jnp.max(l) for l in ls])
        num_hi = jnp.max(jnp.abs(acc_ref[...]))
        unsafe = jnp.logical_not((den_lo >= SAFE_MIN) & (den_hi <= SAFE_MAX) & (num_hi <= SAFE_MAX))

        @pl.when(unsafe)
        def _():
            def chain(c, carry_c):
                q_start = (c // 2) * qs

                def kv_step(j, st_in):
                    m, l = st_in
                    col0 = pl.multiple_of(j * tk, tk)
                    kb = k_ref[0, 0, pl.ds(col0, tk), :]
                    vtb = vt_ref[0, 0, pl.ds(pl.multiple_of((c % 2) * HEAD_DIM, HEAD_DIM), HEAD_DIM),
                                 pl.ds(col0, tk)]
                    st = causal(scores(c, kb), j * tk - row0 - q_start)
                    m_new = jnp.maximum(m, jnp.max(st, axis=0, keepdims=True))
                    alpha = jnp.exp2(m - m_new)
                    p = jnp.exp2(st - m_new)
                    pv = jnp.dot(vtb, p.astype(BF16), preferred_element_type=F32)
                    acc_ref[c] = alpha * acc_ref[c] + pv
                    return m_new, alpha * l + jnp.sum(p, axis=0, keepdims=True)

                acc_ref[c] = jnp.zeros((HEAD_DIM, qs), F32)
                init = (jnp.full((1, qs), NEG, F32), jnp.zeros((1, qs), F32))
                _, l = lax.fori_loop(0, n_full + tq // tk, kv_step, init)
                l_ref[c] = jnp.broadcast_to(l, (8, qs))
                return carry_c

            lax.fori_loop(0, n_chain, chain, 0)

        for h in range(n_sub):
            ot = jnp.concatenate([acc_ref[2 * h] / l_ref[2 * h, 0:1, :],
                                  acc_ref[2 * h + 1] / l_ref[2 * h + 1, 0:1, :]], axis=0)
            r = pl.multiple_of(row0 + h * qs, qs)
            gate = g_ref[0, pl.ds(r, qs), :].astype(F32)
            out_ref[0, pl.ds(r, qs), :] = (ot.T * gate).astype(BF16)
        return carry

    for qi in range(s_len // tq):
        q_tile(qi, 0)


def _latent_attention(qn, qr, kcat, vt, gb):
    b, n_pair, s, _ = qn.shape
    n_chain = 2 * (MLA_Q_TILE // MLA_Q_SUB)
    pair_spec = lambda w: pl.BlockSpec((1, 1, s, w), lambda i, p: (i, p, 0, 0))
    vt_spec = pl.BlockSpec((1, 1, PAIR, s), lambda i, p: (i, p, 0, 0))
    lane_spec = pl.BlockSpec((1, s, PAIR), lambda i, p: (i, 0, p))
    return pl.pallas_call(
        _mla_kernel,
        grid=(b, n_pair),
        in_specs=[pair_spec(PAIR), pl.BlockSpec((1, s, LANES), lambda i, p: (i, 0, p // 2)),
                  pair_spec(2 * LANES), vt_spec, lane_spec],
        out_specs=lane_spec,
        out_shape=jax.ShapeDtypeStruct((b, s, n_pair * PAIR), BF16),
        scratch_shapes=[pltpu.VMEM((n_chain, 2 * LANES, MLA_Q_SUB), BF16),
                        pltpu.VMEM((n_chain, HEAD_DIM, MLA_Q_SUB), F32),
                        pltpu.VMEM((n_chain, 8, MLA_Q_SUB), F32)],
        compiler_params=pltpu.CompilerParams(
            dimension_semantics=("arbitrary", "arbitrary"), vmem_limit_bytes=VMEM_LIMIT),
        name="latent_attention",
    )(qn, qr, kcat, vt, gb)


def _out_kernel(ma_ref, mb_ref, x_ref, w_ref, out_ref):
    a = ma_ref.shape[2]
    acc = jnp.dot(ma_ref[0], w_ref[0:a], preferred_element_type=F32)
    acc = acc + jnp.dot(mb_ref[0], w_ref[a:], preferred_element_type=F32)
    out_ref[0] = x_ref[0] + acc


def _output_projection(ma, mb, x, w):
    b, s, d = x.shape
    tm = OUT_ROW_TILE
    row_spec = lambda w: pl.BlockSpec((1, tm, w), lambda i, j: (i, j, 0))
    full = lambda a: pl.BlockSpec(a.shape, lambda i, j: (0,) * a.ndim)
    return pl.pallas_call(
        _out_kernel,
        grid=(b, s // tm),
        in_specs=[row_spec(ma.shape[-1]), row_spec(mb.shape[-1]), row_spec(d), full(w)],
        out_specs=row_spec(d),
        out_shape=jax.ShapeDtypeStruct((b, s, d), x.dtype),
        compiler_params=pltpu.CompilerParams(
            dimension_semantics=("arbitrary", "arbitrary"), vmem_limit_bytes=VMEM_LIMIT),
        name="output_projection",
    )(ma, mb, x, w)


def _t5_bucket(dist):
    max_exact = REL_BUCKETS // 2
    d = np.maximum(dist.astype(np.float32), np.float32(1.0))
    large = max_exact + (np.log(d / np.float32(max_exact)) / np.float32(
        math.log(REL_MAX_DISTANCE / max_exact)) * np.float32(REL_BUCKETS - max_exact)
                         ).astype(np.int32)
    large = np.minimum(large, REL_BUCKETS - 1)
    return np.where(dist < max_exact, dist, large)


def _dilated_bias(rel_bias, n_heads):
    onehot, keep = [], []
    for dil in DILATIONS:
        n_chunk = MAX_DIL // dil
        csize = BLOCK // n_chunk
        idx = np.arange(BLOCK)
        true_idx = n_chunk * (idx % csize) + idx // csize
        qi = true_idx[:, None]
        ki = np.concatenate([true_idx, true_idx + BLOCK])[None, :]
        j = qi + BLOCK - ki
        valid = (j >= 0) & (j <= BLOCK)
        bucket = _t5_bucket(np.maximum(j, 0) * dil)
        onehot.append(np.arange(REL_BUCKETS)[:, None, None] == bucket)
        keep.append(np.stack([valid, valid & (ki >= BLOCK)]))
    onehot, keep = np.stack(onehot, axis=1), np.stack(keep)
    table = rel_bias.astype(F32)[:, None, :, None, None]
    bias = jnp.sum(jnp.where(onehot[:, :, None], table, 0.0), axis=0) * LOG2E
    tiles = jnp.where(keep[:, :, None], bias[:, None], NEG)
    return tiles.reshape(len(DILATIONS), 2, n_heads // 2, 2 * BLOCK, 2 * BLOCK)


def _layer(x, pos_row, rel_bias, norm_gain, w_in, a_q_gain, a_k_gain, q_c_gain, w_uq, kv_c_gain,
           w_ukv, qn_gain, qr_gain, kn_gain, kr_gain, w_out):
    d = x.shape[-1]
    n_heads = 8
    a_width = n_heads * HEAD_DIM
    row = lambda v: v.reshape(1, -1).astype(F32)

    splits = (a_width, a_width, a_width, a_width, Q_RANK, KV_RANK, ROPE_DIM, a_width)
    starts = np.concatenate([[0], np.cumsum(splits)])
    kr0, kr1 = int(starts[6]), int(starts[7])
    rope_rep = LANES // ROPE_DIM
    w_in_p = jnp.concatenate(
        [w_in[:, :kr0], jnp.tile(w_in[:, kr0:kr1], (1, rope_rep)), w_in[:, kr1:]],
        axis=1).astype(BF16)
    off = dict(qa=int(starts[0]), ka=int(starts[1]), va=int(starts[2]), za=int(starts[3]),
               cq=int(starts[4]), ckv=int(starts[5]), kr=kr0, zb=kr0 + LANES)

    qk_dim = HEAD_DIM + ROPE_DIM
    wq = w_uq.reshape(Q_RANK, n_heads, qk_dim)
    wuq_p = jnp.concatenate([wq[:, :, :HEAD_DIM].reshape(Q_RANK, -1),
                             wq[:, :, HEAD_DIM:].reshape(Q_RANK, -1)], axis=1).astype(BF16)
    wkv = w_ukv.reshape(KV_RANK, n_heads, 2 * HEAD_DIM)
    wukv_p = jnp.concatenate([wkv[:, :, :HEAD_DIM].reshape(KV_RANK, -1),
                              wkv[:, :, HEAD_DIM:].reshape(KV_RANK, -1)], axis=1).astype(BF16)

    scale_a = HEAD_DIM ** -0.5
    scale_b = qk_dim ** -0.5
    inv_freq = ROPE_THETA ** (-jnp.arange(0, ROPE_DIM, 2, dtype=F32) / ROPE_DIM)
    rope_row = lambda v: row(jnp.tile(v, rope_rep))
    half = ROPE_DIM // 2
    freq = inv_freq.reshape(half, 1)
    spread = (np.arange(LANES)[None, :] % half == np.arange(half)[:, None]).astype(np.float32)
    spread = jnp.asarray(spread, BF16)
    sgn = rope_row(jnp.concatenate([-jnp.ones(half, F32), jnp.ones(half, F32)]))
    head_norm, rope_norm = HEAD_DIM ** 0.5, ROPE_DIM ** 0.5

    qa, ka, va, ga, qn, qr, kcat, vt, gb = _input_projection(
        x, pos_row, row(norm_gain), w_in_p,
        row(jnp.tile(a_q_gain, 2) * (head_norm * scale_a * LOG2E)),
        row(jnp.tile(a_k_gain, 2) * head_norm),
        row(q_c_gain), wuq_p, row(kv_c_gain), wukv_p,
        row(jnp.tile(qn_gain, 2) * (head_norm * scale_b * LOG2E)),
        rope_row(qr_gain * (rope_norm * scale_b * LOG2E)),
        row(jnp.tile(kn_gain, 2) * head_norm), rope_row(kr_gain), freq, spread, sgn, off)

    mixed_a = _dilated_attention(qa, ka, va, _dilated_bias(rel_bias, n_heads), ga)
    mixed_b = _latent_attention(qn, qr, kcat, vt, gb)
    return _output_projection(mixed_a, mixed_b, x, w_out.astype(BF16))


def kernel(x, positions, rel_bias, norm_gain, w_in, a_q_gain, a_k_gain, q_c_gain, w_uq, kv_c_gain,
           w_ukv, qn_gain, qr_gain, kn_gain, kr_gain, w_out):
    pos_row = positions[:, None, :]
    for l in range(norm_gain.shape[0]):
        x = _layer(x, pos_row, rel_bias, norm_gain[l], w_in[l], a_q_gain[l], a_k_gain[l],
                   q_c_gain[l], w_uq[l], kv_c_gain[l], w_ukv[l], qn_gain[l], qr_gain[l],
                   kn_gain[l], kr_gain[l], w_out[l])
    return x
```

```python
import functools
import math

import jax
import jax.numpy as jnp
import numpy as np
from jax import lax
from jax.experimental import pallas as pl
from jax.experimental.pallas import tpu as pltpu

F32 = jnp.float32
BF16 = jnp.bfloat16

LANES = 128
HEAD_DIM = 64
PAIR = 2 * HEAD_DIM
ROPE_DIM = 32
KV_RANK = 128
Q_RANK = 256
BLOCK = 128
DILATIONS = (1, 4, 16)
MAX_DIL = 16
DEINTERLEAVE = 4
ROPE_THETA = 10000.0
REL_BUCKETS = 32
REL_MAX_DISTANCE = 2048
EPS = 1e-6
NEG = -1e30
SAFE_MIN, SAFE_MAX = 1e-30, 1e30
LOG2E = math.log2(math.e)

ROW_TILE = 1024
OUT_ROW_TILE = 1024
MLA_Q_TILE = 2048
MLA_K_TILE = 512
MLA_Q_SUB = 256

VMEM_LIMIT = 56 * 1024 * 1024


def _nt_dot(a, b):
    return lax.dot_general(a, b, (((1,), (1,)), ((), ())), preferred_element_type=F32)


def _lo_mask(shape):
    return lax.broadcasted_iota(jnp.int32, shape, len(shape) - 1) < HEAD_DIM


def _silu(z):
    return z * (1.0 / (1.0 + jnp.exp(-z)))


def _group_rms(t, gain_row, group):
    lane_group = lax.broadcasted_iota(jnp.int32, t.shape, 1) // group
    sq = t * t
    ss = None
    for g in range(LANES // group):
        in_g = lane_group == g
        s = jnp.sum(jnp.where(in_g, sq, 0.0), axis=1, keepdims=True)
        ss = s if ss is None else jnp.where(in_g, s, ss)
    return t * lax.rsqrt(ss + group * EPS) * gain_row


def _proj_kernel(x_ref, pos_ref, ng_ref, w_in_ref, gq_ref, gk_ref, gcq_ref, wuq_ref, gckv_ref,
                 wukv_ref, gqn_ref, gqr_ref, gkn_ref, gkr_ref, freq_ref, spread_ref, sgn_ref,
                 qa_ref, ka_ref, va_ref, ga_ref, qn_ref, qr_ref, kcat_ref, vt_ref, gb_ref,
                 *stage_refs, off):
    n_res = qa_ref.shape[2]
    tm = x_ref.shape[1]
    sub = tm // n_res
    n_pair = qa_ref.shape[1]
    a_width = n_pair * PAIR

    x = x_ref[0]
    h = (x * lax.rsqrt(jnp.mean(x * x, axis=-1, keepdims=True) + EPS) * ng_ref[...]).astype(BF16)

    def seg(name, width):
        a = off[name]
        return jnp.dot(h, w_in_ref[:, a:a + width], preferred_element_type=F32)

    def emit_a(t, gain_ref, out_ref, rows_ref, quad_ref):
        quarter = tm // DEINTERLEAVE
        for p in range(n_pair):
            c = t[:, p * PAIR:(p + 1) * PAIR]
            if gain_ref is not None:
                c = _group_rms(c, gain_ref[...], HEAD_DIM)
            rows_ref[p] = c
            for g in range(DEINTERLEAVE):
                quad_ref[p, g * quarter:(g + 1) * quarter] = (
                    rows_ref[p, pl.ds(g, quarter, stride=DEINTERLEAVE), :])
            for r in range(n_res):
                g, h = r % DEINTERLEAVE, r // DEINTERLEAVE
                out_ref[0, p, r] = quad_ref[p, pl.ds(g * quarter + h, sub, stride=DEINTERLEAVE), :]

    ang = freq_ref[...] * pos_ref[0].astype(F32)

    def spread(table):
        hi = table.astype(BF16)
        lo = (table - hi.astype(F32)).astype(BF16)
        tn = lambda a: lax.dot_general(a, spread_ref[...], (((0,), (0,)), ((), ())),
                                       preferred_element_type=F32)
        return tn(hi) + tn(lo)

    cos_t = spread(jnp.cos(ang))
    sin_t = spread(jnp.sin(ang)) * sgn_ref[...]
    lane = lax.broadcasted_iota(jnp.int32, (tm, LANES), 1)
    first_half = (lane & (ROPE_DIM - 1)) < (ROPE_DIM // 2)

    def rope(t):
        partner = jnp.where(first_half, pltpu.roll(t, LANES - ROPE_DIM // 2, 1),
                            pltpu.roll(t, ROPE_DIM // 2, 1))
        return t * cos_t + partner * sin_t

    def rms(t, gain_ref):
        return t * lax.rsqrt(jnp.mean(t * t, axis=-1, keepdims=True) + EPS) * gain_ref[...]

    s_qa = seg("qa", a_width)
    s_ka = seg("ka", a_width)
    emit_a(s_qa, gq_ref, qa_ref, *stage_refs)
    s_cq = seg("cq", Q_RANK)
    s_ckv = seg("ckv", KV_RANK)
    s_kr = seg("kr", LANES)
    emit_a(s_ka, gk_ref, ka_ref, *stage_refs)
    q = jnp.dot(rms(s_cq, gcq_ref).astype(BF16), wuq_ref[...], preferred_element_type=F32)
    kv = jnp.dot(rms(s_ckv, gckv_ref).astype(BF16), wukv_ref[...], preferred_element_type=F32)
    s_za = seg("za", a_width)

    for p in range(n_pair):
        qn_ref[0, p] = _group_rms(q[:, p * PAIR:(p + 1) * PAIR], gqn_ref[...],
                                  HEAD_DIM).astype(BF16)
    for t in range(qr_ref.shape[2] // LANES):
        tile = q[:, a_width + t * LANES:a_width + (t + 1) * LANES]
        qr_ref[0, :, t * LANES:(t + 1) * LANES] = rope(
            _group_rms(tile, gqr_ref[...], ROPE_DIM)).astype(BF16)
    s_zb = seg("zb", a_width)

    kr = rope(rms(s_kr, gkr_ref)).astype(BF16)
    for p in range(n_pair):
        kn = _group_rms(kv[:, p * PAIR:(p + 1) * PAIR], gkn_ref[...], HEAD_DIM)
        kcat_ref[0, p, :, 0:LANES] = kn.astype(BF16)
        kcat_ref[0, p, :, LANES:2 * LANES] = kr
        vt_ref[0, p] = kv[:, a_width + p * PAIR:a_width + (p + 1) * PAIR].T.astype(BF16)

    ga_ref[0] = _silu(s_za).astype(BF16)
    s_va = seg("va", a_width)
    gb_ref[0] = _silu(s_zb).astype(BF16)
    emit_a(s_va, None, va_ref, *stage_refs)


def _input_projection(x, pos_row, ng, w_in_p, gq, gk, gcq, wuq_p, gckv, wukv_p, gqn, gqr, gkn, gkr,
                      freq, spread, sgn, off):
    b, s, d = x.shape
    tm = ROW_TILE
    n_pair = 4
    n_res = MAX_DIL
    sub = tm // n_res
    grid = (b, s // tm)

    def full(a):
        return pl.BlockSpec(a.shape, lambda i, j: (0,) * a.ndim)

    a_shape = jax.ShapeDtypeStruct((b, n_pair, n_res, s // n_res, PAIR), F32)
    a_spec = pl.BlockSpec((1, n_pair, n_res, sub, PAIR), lambda i, j: (i, 0, 0, j, 0))
    row_spec = lambda w: pl.BlockSpec((1, tm, w), lambda i, j: (i, j, 0))
    pair_spec = lambda w: pl.BlockSpec((1, n_pair, tm, w), lambda i, j: (i, 0, j, 0))

    out_shape = (
        a_shape, a_shape, a_shape,
        jax.ShapeDtypeStruct((b, s, n_pair * PAIR), BF16),
        jax.ShapeDtypeStruct((b, n_pair, s, PAIR), BF16),
        jax.ShapeDtypeStruct((b, s, 2 * LANES), BF16),
        jax.ShapeDtypeStruct((b, n_pair, s, 2 * LANES), BF16),
        jax.ShapeDtypeStruct((b, n_pair, PAIR, s), BF16),
        jax.ShapeDtypeStruct((b, s, n_pair * PAIR), BF16),
    )
    out_specs = (
        a_spec, a_spec, a_spec,
        row_spec(n_pair * PAIR),
        pair_spec(PAIR), row_spec(2 * LANES), pair_spec(2 * LANES),
        pl.BlockSpec((1, n_pair, PAIR, tm), lambda i, j: (i, 0, 0, j)),
        row_spec(n_pair * PAIR),
    )
    params = (ng, w_in_p, gq, gk, gcq, wuq_p, gckv, wukv_p, gqn, gqr, gkn, gkr, freq, spread, sgn)
    pos_spec = pl.BlockSpec((1, 1, tm), lambda i, j: (i, 0, j))
    in_specs = [row_spec(d), pos_spec] + [full(a) for a in params]
    stage = pltpu.VMEM((n_pair, tm, PAIR), F32)
    return pl.pallas_call(
        functools.partial(_proj_kernel, off=off),
        grid=grid,
        in_specs=in_specs,
        out_specs=out_specs,
        out_shape=out_shape,
        scratch_shapes=[stage] * 2,
        compiler_params=pltpu.CompilerParams(
            dimension_semantics=("arbitrary", "arbitrary"), vmem_limit_bytes=VMEM_LIMIT),
        name="input_projection",
    )(x, pos_row, *params)


DILATED_GROUP = 32


def _dilated_schedule(pat, i, g):
    if pat == 0:
        return 0, DILATED_GROUP * i + g
    if pat == 1:
        return g % 4, (DILATED_GROUP // 4) * i + g // 4
    return (DILATED_GROUP // 2) * i + g // 2, g % 2


def _dilated_kernel(q_ref, k_ref, v_ref, bias_ref, g_ref, out_ref, obuf, lbuf, mbuf, xo_ref):
    n_res = q_ref.shape[2]
    rows = q_ref.shape[3]
    lo = _lo_mask((BLOCK, PAIR))
    ones = jnp.ones((2 * BLOCK, LANES), BF16)

    def pattern_geometry(pat):
        dil = DILATIONS[pat]
        n_chunk = MAX_DIL // dil
        csize = BLOCK // n_chunk
        return dil, n_chunk, csize

    def load_block(pat, stream, n):
        dil, n_chunk, csize = pattern_geometry(pat)
        chunks = [(stream + dil * c, n * csize) for c in range(n_chunk)]
        first = n == 0

        def gather(ref, shift):
            parts = [ref[0, 0, r, pl.ds(pl.multiple_of(st - shift, 8), csize), :]
                     for r, st in chunks]
            return parts[0] if len(parts) == 1 else jnp.concatenate(parts, axis=0)

        prev_shift = jnp.where(first, 0, csize)
        qc = gather(q_ref, 0)
        lhs = jnp.concatenate([jnp.where(lo, qc, 0.0), jnp.where(lo, 0.0, qc)],
                              axis=0).astype(BF16)
        kk = jnp.concatenate([gather(k_ref, prev_shift), gather(k_ref, 0)], axis=0).astype(BF16)
        vv = jnp.concatenate([gather(v_ref, prev_shift), gather(v_ref, 0)], axis=0).astype(BF16)
        vv1 = jnp.concatenate([vv, ones], axis=1)
        bias = bias_ref[pat, jnp.where(first, 1, 0), 0]
        return chunks, csize, lhs, kk, vv1, bias

    def split_heads(r):
        return jnp.where(lo, r[:BLOCK], r[BLOCK:])

    def store_chunks(buf, lead, chunks, csize, val):
        for c, (r, st) in enumerate(chunks):
            buf[lead + (r, pl.ds(pl.multiple_of(st, 8), csize), slice(None))] = (
                val[c * csize:(c + 1) * csize])

    def fast_body(i, carry):
        slots = [(pat, g) for g in range(DILATED_GROUP) for pat in range(len(DILATIONS))]
        blocks = [load_block(pat, *_dilated_schedule(pat, i, g)) for pat, g in slots]
        ahead = 3
        scores = [_nt_dot(b[2], b[3]) for b in blocks[:ahead]]
        for k, (chunks, csize, _, _, vv1, bias) in enumerate(blocks):
            if k + ahead < len(blocks):
                nb = blocks[k + ahead]
                scores.append(_nt_dot(nb[2], nb[3]))
            p = jnp.exp2(scores[k] + bias).astype(BF16)
            r = jnp.dot(p, vv1, preferred_element_type=F32)
            lead = (slots[k][0],)
            store_chunks(obuf, lead, chunks, csize, split_heads(r[:, :LANES]))
            store_chunks(lbuf, lead, chunks, csize, split_heads(r[:, LANES:]))
        return carry

    lax.fori_loop(0, 32 // DILATED_GROUP, fast_body, 0)

    def to_positions(r, value):
        xo_ref[pl.ds(r, rows, stride=n_res), :] = value

    def pattern_sum(r, stats):
        num = obuf[0, r] + obuf[1, r] + obuf[2, r]
        den = lbuf[0, r] + lbuf[1, r] + lbuf[2, r]
        to_positions(r, num / den)
        den_lo, den_hi, num_hi = stats
        return (jnp.minimum(den_lo, jnp.min(den, axis=0, keepdims=True)),
                jnp.maximum(den_hi, jnp.max(den, axis=0, keepdims=True)),
                jnp.maximum(num_hi, jnp.max(jnp.abs(num), axis=0, keepdims=True)))

    row = lambda v: jnp.full((1, PAIR), v, F32)
    den_lo, den_hi, num_hi = lax.fori_loop(0, n_res, pattern_sum, (row(1.0), row(1.0), row(0.0)))
    unsafe = jnp.logical_not((jnp.min(den_lo) >= SAFE_MIN) & (jnp.max(den_hi) <= SAFE_MAX)
                             & (jnp.max(num_hi) <= SAFE_MAX))

    @pl.when(unsafe)
    def _():
        for pat in range(len(DILATIONS)):
            def body(i, carry, pat=pat):
                n_blocks = 32 // DILATIONS[pat]
                chunks, csize, lhs, kk, vv1, bias = load_block(pat, i // n_blocks, i % n_blocks)
                t = _nt_dot(lhs, kk) + bias
                m = jnp.max(t, axis=1, keepdims=True)
                p = jnp.exp2(t - m).astype(BF16)
                r = jnp.dot(p, vv1, preferred_element_type=F32)
                num_blk, den_blk = split_heads(r[:, :LANES]), split_heads(r[:, LANES:])
                m2 = jnp.where(lo, m[:BLOCK], m[BLOCK:])
                if pat == 0:
                    store_chunks(obuf, (0,), chunks, csize, num_blk)
                    store_chunks(lbuf, (0,), chunks, csize, den_blk)
                    store_chunks(mbuf, (), chunks, csize, m2)
                else:
                    for c, (r16, st) in enumerate(chunks):
                        rows_c = pl.ds(pl.multiple_of(st, 8), csize)
                        piece = slice(c * csize, (c + 1) * csize)
                        m_old = mbuf[r16, rows_c, :]
                        m_new = jnp.maximum(m_old, m2[piece])
                        a, b = jnp.exp2(m_old - m_new), jnp.exp2(m2[piece] - m_new)
                        obuf[0, r16, rows_c, :] = a * obuf[0, r16, rows_c, :] + b * num_blk[piece]
                        lbuf[0, r16, rows_c, :] = a * lbuf[0, r16, rows_c, :] + b * den_blk[piece]
                        mbuf[r16, rows_c, :] = m_new
                return carry

            lax.fori_loop(0, 32, body, 0)

        def merge(r, carry):
            to_positions(r, obuf[0, r] / lbuf[0, r])
            return carry

        lax.fori_loop(0, n_res, merge, 0)

    out_ref[0] = (xo_ref[...] * g_ref[0].astype(F32)).astype(BF16)


def _dilated_attention(qa, ka, va, bias, ga):
    b, n_pair, n_res, rows, _ = qa.shape
    s = n_res * rows
    a_spec = pl.BlockSpec((1, 1, n_res, rows, PAIR), lambda i, p: (i, p, 0, 0, 0))
    lane_spec = pl.BlockSpec((1, s, PAIR), lambda i, p: (i, 0, p))
    bias_spec = pl.BlockSpec((len(DILATIONS), 2, 1, 2 * BLOCK, 2 * BLOCK),
                             lambda i, p: (0, 0, p, 0, 0))
    buf = pltpu.VMEM((n_res, rows, PAIR), F32)
    pat_buf = pltpu.VMEM((len(DILATIONS), n_res, rows, PAIR), F32)
    return pl.pallas_call(
        _dilated_kernel,
        grid=(b, n_pair),
        in_specs=[a_spec, a_spec, a_spec, bias_spec, lane_spec],
        out_specs=lane_spec,
        out_shape=jax.ShapeDtypeStruct((b, s, n_pair * PAIR), BF16),
        scratch_shapes=[pat_buf, pat_buf, buf, pltpu.VMEM((s, PAIR), F32)],
        compiler_params=pltpu.CompilerParams(
            dimension_semantics=("arbitrary", "arbitrary"), vmem_limit_bytes=VMEM_LIMIT),
        name="dilated_attention",
    )(qa, ka, va, bias, ga)


def _mla_kernel(qn_ref, qr_ref, k_ref, vt_ref, g_ref, out_ref, qt_ref, acc_ref, l_ref):
    tq, tk, qs = MLA_Q_TILE, MLA_K_TILE, MLA_Q_SUB
    n_sub = tq // qs
    n_chain = 2 * n_sub
    s_len = k_ref.shape[2]
    lo_q = _lo_mask((qs, PAIR))
    rope_head = lax.broadcasted_iota(jnp.int32, (qs, LANES), 1) // ROPE_DIM
    even_head = (pl.program_id(1) % 2) * 2

    def scores(c, kb):
        return jnp.dot(kb, qt_ref[c], preferred_element_type=F32)

    def causal(st, key_offset):
        key = lax.broadcasted_iota(jnp.int32, st.shape, 0) + key_offset
        qry = lax.broadcasted_iota(jnp.int32, st.shape, 1)
        return jnp.where(key <= qry, st, NEG)

    def run_jobs(jobs, ls, assign):
        ahead = 4
        sts = [scores(c, kb) for c, kb, _, _ in jobs[:ahead]]
        for n, (c, kb, vtb, key_offset) in enumerate(jobs):
            if n + ahead < len(jobs):
                nc, nkb, _, _ = jobs[n + ahead]
                sts.append(scores(nc, nkb))
            st = sts[n] if key_offset is None else causal(sts[n], key_offset)
            p = jnp.exp2(st)
            col_sum = jnp.sum(p, axis=0, keepdims=True)
            pv = jnp.dot(vtb[c % 2], p.astype(BF16), preferred_element_type=F32)
            if c in assign:
                assign.remove(c)
                ls[c] = col_sum
                acc_ref[c] = pv
            else:
                ls[c] = ls[c] + col_sum
                acc_ref[c] += pv
        return ls

    def q_tile(qi, carry):
        row0 = qi * tq
        for h in range(n_sub):
            r = pl.multiple_of(row0 + h * qs, qs)
            qn = qn_ref[0, 0, pl.ds(r, qs), :].astype(F32)
            qr = qr_ref[0, pl.ds(r, qs), :].astype(F32)
            q_even = jnp.concatenate(
                [jnp.where(lo_q, qn, 0.0), jnp.where(rope_head == even_head, qr, 0.0)], axis=1)
            q_odd = jnp.concatenate(
                [jnp.where(lo_q, 0.0, qn), jnp.where(rope_head == even_head + 1, qr, 0.0)], axis=1)
            qt_ref[2 * h] = q_even.T.astype(BF16)
            qt_ref[2 * h + 1] = q_odd.T.astype(BF16)

        def kv_block(start, size):
            start = pl.multiple_of(start, qs)
            vt = tuple(vt_ref[0, 0, hd * HEAD_DIM:(hd + 1) * HEAD_DIM, pl.ds(start, size)]
                       for hd in range(2))
            return k_ref[0, 0, pl.ds(start, size), :], vt

        jobs = []
        for h in range(n_sub):
            for k0 in range(0, (h + 1) * qs, tk):
                size = min(tk, (h + 1) * qs - k0)
                kb, vtb = kv_block(row0 + k0, size)
                offset = k0 - h * qs if k0 + size > h * qs else None
                jobs += [(2 * h, kb, vtb, offset), (2 * h + 1, kb, vtb, offset)]
        n_full = qi * (tq // tk)
        for t in range(n_full):
            kb, vtb = kv_block(t * tk, tk)
            jobs += [(c, kb, vtb, None) for c in range(n_chain)]
        ls = run_jobs(jobs, [None] * n_chain, set(range(n_chain)))
        for c in range(n_chain):
            l_ref[c] = jnp.broadcast_to(ls[c], (8, qs))
        den_lo = functools.reduce(jnp.minimum, [jnp.min(l) for l in ls])
        den_hi = functools.reduce(jnp.maximum, [jnp.max(l) for l in ls])
        num_hi = jnp.max(jnp.abs(acc_ref[...]))
        unsafe = jnp.logical_not((den_lo >= SAFE_MIN) & (den_hi <= SAFE_MAX) & (num_hi <= SAFE_MAX))

        @pl.when(unsafe)
        def _():
            def chain(c, carry_c):
                q_start = (c // 2) * qs

                def kv_step(j, st_in):
                    m, l = st_in
                    col0 = pl.multiple_of(j * tk, tk)
                    kb = k_ref[0, 0, pl.ds(col0, tk), :]
                    vtb = vt_ref[0, 0, pl.ds(pl.multiple_of((c % 2) * HEAD_DIM, HEAD_DIM), HEAD_DIM),
                                 pl.ds(col0, tk)]
                    st = causal(scores(c, kb), j * tk - row0 - q_start)
                    m_new = jnp.maximum(m, jnp.max(st, axis=0, keepdims=True))
                    alpha = jnp.exp2(m - m_new)
                    p = jnp.exp2(st - m_new)
                    pv = jnp.dot(vtb, p.astype(BF16), preferred_element_type=F32)
                    acc_ref[c] = alpha * acc_ref[c] + pv
                    return m_new, alpha * l + jnp.sum(p, axis=0, keepdims=True)

                acc_ref[c] = jnp.zeros((HEAD_DIM, qs), F32)
                init = (jnp.full((1, qs), NEG, F32), jnp.zeros((1, qs), F32))
                _, l = lax.fori_loop(0, n_full + tq // tk, kv_step, init)
                l_ref[c] = jnp.broadcast_to(l, (8, qs))
                return carry_c

            lax.fori_loop(0, n_chain, chain, 0)

        for h in range(n_sub):
            ot = jnp.concatenate([acc_ref[2 * h] / l_ref[2 * h, 0:1, :],
                                  acc_ref[2 * h + 1] / l_ref[2 * h + 1, 0:1, :]], axis=0)
            r = pl.multiple_of(row0 + h * qs, qs)
            gate = g_ref[0, pl.ds(r, qs), :].astype(F32)
            out_ref[0, pl.ds(r, qs), :] = (ot.T * gate).astype(BF16)
        return carry

    for qi in range(s_len // tq):
        q_tile(qi, 0)


def _latent_attention(qn, qr, kcat, vt, gb):
    b, n_pair, s, _ = qn.shape
    n_chain = 2 * (MLA_Q_TILE // MLA_Q_SUB)
    pair_spec = lambda w: pl.BlockSpec((1, 1, s, w), lambda i, p: (i, p, 0, 0))
    vt_spec = pl.BlockSpec((1, 1, PAIR, s), lambda i, p: (i, p, 0, 0))
    lane_spec = pl.BlockSpec((1, s, PAIR), lambda i, p: (i, 0, p))
    return pl.pallas_call(
        _mla_kernel,
        grid=(b, n_pair),
        in_specs=[pair_spec(PAIR), pl.BlockSpec((1, s, LANES), lambda i, p: (i, 0, p // 2)),
                  pair_spec(2 * LANES), vt_spec, lane_spec],
        out_specs=lane_spec,
        out_shape=jax.ShapeDtypeStruct((b, s, n_pair * PAIR), BF16),
        scratch_shapes=[pltpu.VMEM((n_chain, 2 * LANES, MLA_Q_SUB), BF16),
                        pltpu.VMEM((n_chain, HEAD_DIM, MLA_Q_SUB), F32),
                        pltpu.VMEM((n_chain, 8, MLA_Q_SUB), F32)],
        compiler_params=pltpu.CompilerParams(
            dimension_semantics=("arbitrary", "arbitrary"), vmem_limit_bytes=VMEM_LIMIT),
        name="latent_attention",
    )(qn, qr, kcat, vt, gb)


def _out_kernel(ma_ref, mb_ref, x_ref, w_ref, out_ref):
    a = ma_ref.shape[2]
    acc = jnp.dot(ma_ref[0], w_ref[0:a], preferred_element_type=F32)
    acc = acc + jnp.dot(mb_ref[0], w_ref[a:], preferred_element_type=F32)
    out_ref[0] = x_ref[0] + acc


def _output_projection(ma, mb, x, w):
    b, s, d = x.shape
    tm = OUT_ROW_TILE
    row_spec = lambda w: pl.BlockSpec((1, tm, w), lambda i, j: (i, j, 0))
    full = lambda a: pl.BlockSpec(a.shape, lambda i, j: (0,) * a.ndim)
    return pl.pallas_call(
        _out_kernel,
        grid=(b, s // tm),
        in_specs=[row_spec(ma.shape[-1]), row_spec(mb.shape[-1]), row_spec(d), full(w)],
        out_specs=row_spec(d),
        out_shape=jax.ShapeDtypeStruct((b, s, d), x.dtype),
        compiler_params=pltpu.CompilerParams(
            dimension_semantics=("arbitrary", "arbitrary"), vmem_limit_bytes=VMEM_LIMIT),
        name="output_projection",
    )(ma, mb, x, w)


def _t5_bucket(dist):
    max_exact = REL_BUCKETS // 2
    d = np.maximum(dist.astype(np.float32), np.float32(1.0))
    large = max_exact + (np.log(d / np.float32(max_exact)) / np.float32(
        math.log(REL_MAX_DISTANCE / max_exact)) * np.float32(REL_BUCKETS - max_exact)
                         ).astype(np.int32)
    large = np.minimum(large, REL_BUCKETS - 1)
    return np.where(dist < max_exact, dist, large)


def _dilated_bias(rel_bias, n_heads):
    onehot, keep = [], []
    for dil in DILATIONS:
        n_chunk = MAX_DIL // dil
        csize = BLOCK // n_chunk
        idx = np.arange(BLOCK)
        true_idx = n_chunk * (idx % csize) + idx // csize
        qi = true_idx[:, None]
        ki = np.concatenate([true_idx, true_idx + BLOCK])[None, :]
        j = qi + BLOCK - ki
        valid = (j >= 0) & (j <= BLOCK)
        bucket = _t5_bucket(np.maximum(j, 0) * dil)
        onehot.append(np.arange(REL_BUCKETS)[:, None, None] == bucket)
        keep.append(np.stack([valid, valid & (ki >= BLOCK)]))
    onehot, keep = np.stack(onehot, axis=1), np.stack(keep)
    table = rel_bias.astype(F32)[:, None, :, None, None]
    bias = jnp.sum(jnp.where(onehot[:, :, None], table, 0.0), axis=0) * LOG2E
    tiles = jnp.where(keep[:, :, None], bias[:, None], NEG)
    return tiles.reshape(len(DILATIONS), 2, n_heads // 2, 2 * BLOCK, 2 * BLOCK)


def _layer(x, pos_row, rel_bias, norm_gain, w_in, a_q_gain, a_k_gain, q_c_gain, w_uq, kv_c_gain,
           w_ukv, qn_gain, qr_gain, kn_gain, kr_gain, w_out):
    d = x.shape[-1]
    n_heads = 8
    a_width = n_heads * HEAD_DIM
    row = lambda v: v.reshape(1, -1).astype(F32)

    splits = (a_width, a_width, a_width, a_width, Q_RANK, KV_RANK, ROPE_DIM, a_width)
    starts = np.concatenate([[0], np.cumsum(splits)])
    kr0, kr1 = int(starts[6]), int(starts[7])
    rope_rep = LANES // ROPE_DIM
    w_in_p = jnp.concatenate(
        [w_in[:, :kr0], jnp.tile(w_in[:, kr0:kr1], (1, rope_rep)), w_in[:, kr1:]],
        axis=1).astype(BF16)
    off = dict(qa=int(starts[0]), ka=int(starts[1]), va=int(starts[2]), za=int(starts[3]),
               cq=int(starts[4]), ckv=int(starts[5]), kr=kr0, zb=kr0 + LANES)

    qk_dim = HEAD_DIM + ROPE_DIM
    wq = w_uq.reshape(Q_RANK, n_heads, qk_dim)
    wuq_p = jnp.concatenate([wq[:, :, :HEAD_DIM].reshape(Q_RANK, -1),
                             wq[:, :, HEAD_DIM:].reshape(Q_RANK, -1)], axis=1).astype(BF16)
    wkv = w_ukv.reshape(KV_RANK, n_heads, 2 * HEAD_DIM)
    wukv_p = jnp.concatenate([wkv[:, :, :HEAD_DIM].reshape(KV_RANK, -1),
                              wkv[:, :, HEAD_DIM:].reshape(KV_RANK, -1)], axis=1).astype(BF16)

    scale_a = HEAD_DIM ** -0.5
    scale_b = qk_dim ** -0.5
    inv_freq = ROPE_THETA ** (-jnp.arange(0, ROPE_DIM, 2, dtype=F32) / ROPE_DIM)
    rope_row = lambda v: row(jnp.tile(v, rope_rep))
    half = ROPE_DIM // 2
    freq = inv_freq.reshape(half, 1)
    spread = (np.arange(LANES)[None, :] % half == np.arange(half)[:, None]).astype(np.float32)
    spread = jnp.asarray(spread, BF16)
    sgn = rope_row(jnp.concatenate([-jnp.ones(half, F32), jnp.ones(half, F32)]))
    head_norm, rope_norm = HEAD_DIM ** 0.5, ROPE_DIM ** 0.5

    qa, ka, va, ga, qn, qr, kcat, vt, gb = _input_projection(
        x, pos_row, row(norm_gain), w_in_p,
        row(jnp.tile(a_q_gain, 2) * (head_norm * scale_a * LOG2E)),
        row(jnp.tile(a_k_gain, 2) * head_norm),
        row(q_c_gain), wuq_p, row(kv_c_gain), wukv_p,
        row(jnp.tile(qn_gain, 2) * (head_norm * scale_b * LOG2E)),
        rope_row(qr_gain * (rope_norm * scale_b * LOG2E)),
        row(jnp.tile(kn_gain, 2) * head_norm), rope_row(kr_gain), freq, spread, sgn, off)

    mixed_a = _dilated_attention(qa, ka, va, _dilated_bias(rel_bias, n_heads), ga)
    mixed_b = _latent_attention(qn, qr, kcat, vt, gb)
    return _output_projection(mixed_a, mixed_b, x, w_out.astype(BF16))


def kernel(x, positions, rel_bias, norm_gain, w_in, a_q_gain, a_k_gain, q_c_gain, w_uq, kv_c_gain,
           w_ukv, qn_gain, qr_gain, kn_gain, kr_gain, w_out):
    pos_row = positions[:, None, :]
    for l in range(norm_gain.shape[0]):
        x = _layer(x, pos_row, rel_bias, norm_gain[l], w_in[l], a_q_gain[l], a_k_gain[l],
                   q_c_gain[l], w_uq[l], kv_c_gain[l], w_ukv[l], qn_gain[l], qr_gain[l],
                   kn_gain[l], kr_gain[l], w_out[l])
    return x
```

```python
import functools
import math

import jax
import jax.numpy as jnp
import numpy as np
from jax import lax
from jax.experimental import pallas as pl
from jax.experimental.pallas import tpu as pltpu

F32 = jnp.float32
BF16 = jnp.bfloat16

LANES = 128
HEAD_DIM = 64
PAIR = 2 * HEAD_DIM
ROPE_DIM = 32
KV_RANK = 128
Q_RANK = 256
BLOCK = 128
DILATIONS = (1, 4, 16)
MAX_DIL = 16
DEINTERLEAVE = 4
ROPE_THETA = 10000.0
REL_BUCKETS = 32
REL_MAX_DISTANCE = 2048
EPS = 1e-6
NEG = -1e30
SAFE_MIN, SAFE_MAX = 1e-30, 1e30
LOG2E = math.log2(math.e)

ROW_TILE = 1024
OUT_ROW_TILE = 2048
MLA_Q_TILE = 2048
MLA_K_TILE = 512
MLA_Q_SUB = 256

VMEM_LIMIT = 56 * 1024 * 1024


def _nt_dot(a, b):
    return lax.dot_general(a, b, (((1,), (1,)), ((), ())), preferred_element_type=F32)


def _lo_mask(shape):
    return lax.broadcasted_iota(jnp.int32, shape, len(shape) - 1) < HEAD_DIM


def _silu(z):
    return z * (1.0 / (1.0 + jnp.exp(-z)))


def _group_rms(t, gain_row, group):
    lane_group = lax.broadcasted_iota(jnp.int32, t.shape, 1) // group
    sq = t * t
    ss = None
    for g in range(LANES // group):
        in_g = lane_group == g
        s = jnp.sum(jnp.where(in_g, sq, 0.0), axis=1, keepdims=True)
        ss = s if ss is None else jnp.where(in_g, s, ss)
    return t * lax.rsqrt(ss + group * EPS) * gain_row


def _proj_kernel(x_ref, pos_ref, ng_ref, w_in_ref, gq_ref, gk_ref, gcq_ref, wuq_ref, gckv_ref,
                 wukv_ref, gqn_ref, gqr_ref, gkn_ref, gkr_ref, freq_ref, spread_ref, sgn_ref,
                 qa_ref, ka_ref, va_ref, ga_ref, qn_ref, qr_ref, kcat_ref, vt_ref, gb_ref,
                 *stage_refs, off):
    n_res = qa_ref.shape[2]
    tm = x_ref.shape[1]
    sub = tm // n_res
    n_pair = qa_ref.shape[1]
    a_width = n_pair * PAIR

    x = x_ref[0]
    h = (x * lax.rsqrt(jnp.mean(x * x, axis=-1, keepdims=True) + EPS) * ng_ref[...]).astype(BF16)

    def seg(name, width):
        a = off[name]
        return jnp.dot(h, w_in_ref[:, a:a + width], preferred_element_type=F32)

    def emit_a(t, gain_ref, out_ref, rows_ref, quad_ref):
        quarter = tm // DEINTERLEAVE
        for p in range(n_pair):
            c = t[:, p * PAIR:(p + 1) * PAIR]
            if gain_ref is not None:
                c = _group_rms(c, gain_ref[...], HEAD_DIM)
            rows_ref[p] = c
            for g in range(DEINTERLEAVE):
                quad_ref[p, g * quarter:(g + 1) * quarter] = (
                    rows_ref[p, pl.ds(g, quarter, stride=DEINTERLEAVE), :])
            for r in range(n_res):
                g, h = r % DEINTERLEAVE, r // DEINTERLEAVE
                out_ref[0, p, r] = quad_ref[p, pl.ds(g * quarter + h, sub, stride=DEINTERLEAVE), :]

    ang = freq_ref[...] * pos_ref[0].astype(F32)

    def spread(table):
        hi = table.astype(BF16)
        lo = (table - hi.astype(F32)).astype(BF16)
        tn = lambda a: lax.dot_general(a, spread_ref[...], (((0,), (0,)), ((), ())),
                                       preferred_element_type=F32)
        return tn(hi) + tn(lo)

    cos_t = spread(jnp.cos(ang))
    sin_t = spread(jnp.sin(ang)) * sgn_ref[...]
    lane = lax.broadcasted_iota(jnp.int32, (tm, LANES), 1)
    first_half = (lane & (ROPE_DIM - 1)) < (ROPE_DIM // 2)

    def rope(t):
        partner = jnp.where(first_half, pltpu.roll(t, LANES - ROPE_DIM // 2, 1),
                            pltpu.roll(t, ROPE_DIM // 2, 1))
        return t * cos_t + partner * sin_t

    def rms(t, gain_ref):
        return t * lax.rsqrt(jnp.mean(t * t, axis=-1, keepdims=True) + EPS) * gain_ref[...]

    s_qa = seg("qa", a_width)
    s_ka = seg("ka", a_width)
    emit_a(s_qa, gq_ref, qa_ref, *stage_refs)
    s_cq = seg("cq", Q_RANK)
    s_ckv = seg("ckv", KV_RANK)
    s_kr = seg("kr", LANES)
    emit_a(s_ka, gk_ref, ka_ref, *stage_refs)
    q = jnp.dot(rms(s_cq, gcq_ref).astype(BF16), wuq_ref[...], preferred_element_type=F32)
    kv = jnp.dot(rms(s_ckv, gckv_ref).astype(BF16), wukv_ref[...], preferred_element_type=F32)
    s_za = seg("za", a_width)

    for p in range(n_pair):
        qn_ref[0, p] = _group_rms(q[:, p * PAIR:(p + 1) * PAIR], gqn_ref[...],
                                  HEAD_DIM).astype(BF16)
    for t in range(qr_ref.shape[2] // LANES):
        tile = q[:, a_width + t * LANES:a_width + (t + 1) * LANES]
        qr_ref[0, :, t * LANES:(t + 1) * LANES] = rope(
            _group_rms(tile, gqr_ref[...], ROPE_DIM)).astype(BF16)
    s_zb = seg("zb", a_width)

    kr = rope(rms(s_kr, gkr_ref)).astype(BF16)
    for p in range(n_pair):
        kn = _group_rms(kv[:, p * PAIR:(p + 1) * PAIR], gkn_ref[...], HEAD_DIM)
        kcat_ref[0, p, :, 0:LANES] = kn.astype(BF16)
        kcat_ref[0, p, :, LANES:2 * LANES] = kr
        vt_ref[0, p] = kv[:, a_width + p * PAIR:a_width + (p + 1) * PAIR].T.astype(BF16)

    ga_ref[0] = _silu(s_za).astype(BF16)
    s_va = seg("va", a_width)
    gb_ref[0] = _silu(s_zb).astype(BF16)
    emit_a(s_va, None, va_ref, *stage_refs)


def _input_projection(x, pos_row, ng, w_in_p, gq, gk, gcq, wuq_p, gckv, wukv_p, gqn, gqr, gkn, gkr,
                      freq, spread, sgn, off):
    b, s, d = x.shape
    tm = ROW_TILE
    n_pair = 4
    n_res = MAX_DIL
    sub = tm // n_res
    grid = (b, s // tm)

    def full(a):
        return pl.BlockSpec(a.shape, lambda i, j: (0,) * a.ndim)

    a_shape = jax.ShapeDtypeStruct((b, n_pair, n_res, s // n_res, PAIR), F32)
    a_spec = pl.BlockSpec((1, n_pair, n_res, sub, PAIR), lambda i, j: (i, 0, 0, j, 0))
    row_spec = lambda w: pl.BlockSpec((1, tm, w), lambda i, j: (i, j, 0))
    pair_spec = lambda w: pl.BlockSpec((1, n_pair, tm, w), lambda i, j: (i, 0, j, 0))

    out_shape = (
        a_shape, a_shape, a_shape,
        jax.ShapeDtypeStruct((b, s, n_pair * PAIR), BF16),
        jax.ShapeDtypeStruct((b, n_pair, s, PAIR), BF16),
        jax.ShapeDtypeStruct((b, s, 2 * LANES), BF16),
        jax.ShapeDtypeStruct((b, n_pair, s, 2 * LANES), BF16),
        jax.ShapeDtypeStruct((b, n_pair, PAIR, s), BF16),
        jax.ShapeDtypeStruct((b, s, n_pair * PAIR), BF16),
    )
    out_specs = (
        a_spec, a_spec, a_spec,
        row_spec(n_pair * PAIR),
        pair_spec(PAIR), row_spec(2 * LANES), pair_spec(2 * LANES),
        pl.BlockSpec((1, n_pair, PAIR, tm), lambda i, j: (i, 0, 0, j)),
        row_spec(n_pair * PAIR),
    )
    params = (ng, w_in_p, gq, gk, gcq, wuq_p, gckv, wukv_p, gqn, gqr, gkn, gkr, freq, spread, sgn)
    pos_spec = pl.BlockSpec((1, 1, tm), lambda i, j: (i, 0, j))
    in_specs = [row_spec(d), pos_spec] + [full(a) for a in params]
    stage = pltpu.VMEM((n_pair, tm, PAIR), F32)
    return pl.pallas_call(
        functools.partial(_proj_kernel, off=off),
        grid=grid,
        in_specs=in_specs,
        out_specs=out_specs,
        out_shape=out_shape,
        scratch_shapes=[stage] * 2,
        compiler_params=pltpu.CompilerParams(
            dimension_semantics=("arbitrary", "arbitrary"), vmem_limit_bytes=VMEM_LIMIT),
        name="input_projection",
    )(x, pos_row, *params)


DILATED_GROUP = 32


def _dilated_schedule(pat, i, g):
    if pat == 0:
        return 0, DILATED_GROUP * i + g
    if pat == 1:
        return g % 4, (DILATED_GROUP // 4) * i + g // 4
    return (DILATED_GROUP // 2) * i + g // 2, g % 2


def _dilated_kernel(q_ref, k_ref, v_ref, bias_ref, g_ref, out_ref, obuf, lbuf, mbuf, xo_ref):
    n_res = q_ref.shape[2]
    rows = q_ref.shape[3]
    lo = _lo_mask((BLOCK, PAIR))
    ones = jnp.ones((2 * BLOCK, LANES), BF16)

    def pattern_geometry(pat):
        dil = DILATIONS[pat]
        n_chunk = MAX_DIL // dil
        csize = BLOCK // n_chunk
        return dil, n_chunk, csize

    def load_block(pat, stream, n):
        dil, n_chunk, csize = pattern_geometry(pat)
        chunks = [(stream + dil * c, n * csize) for c in range(n_chunk)]
        first = n == 0

        def gather(ref, shift):
            parts = [ref[0, 0, r, pl.ds(pl.multiple_of(st - shift, 8), csize), :]
                     for r, st in chunks]
            return parts[0] if len(parts) == 1 else jnp.concatenate(parts, axis=0)

        prev_shift = jnp.where(first, 0, csize)
        qc = gather(q_ref, 0)
        lhs = jnp.concatenate([jnp.where(lo, qc, 0.0), jnp.where(lo, 0.0, qc)],
                              axis=0).astype(BF16)
        kk = jnp.concatenate([gather(k_ref, prev_shift), gather(k_ref, 0)], axis=0).astype(BF16)
        vv = jnp.concatenate([gather(v_ref, prev_shift), gather(v_ref, 0)], axis=0).astype(BF16)
        vv1 = jnp.concatenate([vv, ones], axis=1)
        bias = bias_ref[pat, jnp.where(first, 1, 0), 0]
        return chunks, csize, lhs, kk, vv1, bias

    def split_heads(r):
        return jnp.where(lo, r[:BLOCK], r[BLOCK:])

    def store_chunks(buf, lead, chunks, csize, val):
        for c, (r, st) in enumerate(chunks):
            buf[lead + (r, pl.ds(pl.multiple_of(st, 8), csize), slice(None))] = (
                val[c * csize:(c + 1) * csize])

    def fast_body(i, carry):
        slots = [(pat, g) for g in range(DILATED_GROUP) for pat in range(len(DILATIONS))]
        blocks = [load_block(pat, *_dilated_schedule(pat, i, g)) for pat, g in slots]
        ahead = 3
        scores = [_nt_dot(b[2], b[3]) for b in blocks[:ahead]]
        for k, (chunks, csize, _, _, vv1, bias) in enumerate(blocks):
            if k + ahead < len(blocks):
                nb = blocks[k + ahead]
                scores.append(_nt_dot(nb[2], nb[3]))
            p = jnp.exp2(scores[k] + bias).astype(BF16)
            r = jnp.dot(p, vv1, preferred_element_type=F32)
            lead = (slots[k][0],)
            store_chunks(obuf, lead, chunks, csize, split_heads(r[:, :LANES]))
            store_chunks(lbuf, lead, chunks, csize, split_heads(r[:, LANES:]))
        return carry

    lax.fori_loop(0, 32 // DILATED_GROUP, fast_body, 0)

    def to_positions(r, value):
        xo_ref[pl.ds(r, rows, stride=n_res), :] = value

    def pattern_sum(r, stats):
        num = obuf[0, r] + obuf[1, r] + obuf[2, r]
        den = lbuf[0, r] + lbuf[1, r] + lbuf[2, r]
        to_positions(r, num / den)
        den_lo, den_hi, num_hi = stats
        return (jnp.minimum(den_lo, jnp.min(den, axis=0, keepdims=True)),
                jnp.maximum(den_hi, jnp.max(den, axis=0, keepdims=True)),
                jnp.maximum(num_hi, jnp.max(jnp.abs(num), axis=0, keepdims=True)))

    row = lambda v: jnp.full((1, PAIR), v, F32)
    den_lo, den_hi, num_hi = lax.fori_loop(0, n_res, pattern_sum, (row(1.0), row(1.0), row(0.0)))
    unsafe = jnp.logical_not((jnp.min(den_lo) >= SAFE_MIN) & (jnp.max(den_hi) <= SAFE_MAX)
                             & (jnp.max(num_hi) <= SAFE_MAX))

    @pl.when(unsafe)
    def _():
        for pat in range(len(DILATIONS)):
            def body(i, carry, pat=pat):
                n_blocks = 32 // DILATIONS[pat]
                chunks, csize, lhs, kk, vv1, bias = load_block(pat, i // n_blocks, i % n_blocks)
                t = _nt_dot(lhs, kk) + bias
                m = jnp.max(t, axis=1, keepdims=True)
                p = jnp.exp2(t - m).astype(BF16)
                r = jnp.dot(p, vv1, preferred_element_type=F32)
                num_blk, den_blk = split_heads(r[:, :LANES]), split_heads(r[:, LANES:])
                m2 = jnp.where(lo, m[:BLOCK], m[BLOCK:])
                if pat == 0:
                    store_chunks(obuf, (0,), chunks, csize, num_blk)
                    store_chunks(lbuf, (0,), chunks, csize, den_blk)
                    store_chunks(mbuf, (), chunks, csize, m2)
                else:
                    for c, (r16, st) in enumerate(chunks):
                        rows_c = pl.ds(pl.multiple_of(st, 8), csize)
                        piece = slice(c * csize, (c + 1) * csize)
                        m_old = mbuf[r16, rows_c, :]
                        m_new = jnp.maximum(m_old, m2[piece])
                        a, b = jnp.exp2(m_old - m_new), jnp.exp2(m2[piece] - m_new)
                        obuf[0, r16, rows_c, :] = a * obuf[0, r16, rows_c, :] + b * num_blk[piece]
                        lbuf[0, r16, rows_c, :] = a * lbuf[0, r16, rows_c, :] + b * den_blk[piece]
                        mbuf[r16, rows_c, :] = m_new
                return carry

            lax.fori_loop(0, 32, body, 0)

        def merge(r, carry):
            to_positions(r, obuf[0, r] / lbuf[0, r])
            return carry

        lax.fori_loop(0, n_res, merge, 0)

    out_ref[0] = (xo_ref[...] * g_ref[0].astype(F32)).astype(BF16)


def _dilated_attention(qa, ka, va, bias, ga):
    b, n_pair, n_res, rows, _ = qa.shape
    s = n_res * rows
    a_spec = pl.BlockSpec((1, 1, n_res, rows, PAIR), lambda i, p: (i, p, 0, 0, 0))
    lane_spec = pl.BlockSpec((1, s, PAIR), lambda i, p: (i, 0, p))
    bias_spec = pl.BlockSpec((len(DILATIONS), 2, 1, 2 * BLOCK, 2 * BLOCK),
                             lambda i, p: (0, 0, p, 0, 0))
    buf = pltpu.VMEM((n_res, rows, PAIR), F32)
    pat_buf = pltpu.VMEM((len(DILATIONS), n_res, rows, PAIR), F32)
    return pl.pallas_call(
        _dilated_kernel,
        grid=(b, n_pair),
        in_specs=[a_spec, a_spec, a_spec, bias_spec, lane_spec],
        out_specs=lane_spec,
        out_shape=jax.ShapeDtypeStruct((b, s, n_pair * PAIR), BF16),
        scratch_shapes=[pat_buf, pat_buf, buf, pltpu.VMEM((s, PAIR), F32)],
        compiler_params=pltpu.CompilerParams(
            dimension_semantics=("arbitrary", "arbitrary"), vmem_limit_bytes=VMEM_LIMIT),
        name="dilated_attention",
    )(qa, ka, va, bias, ga)


def _mla_kernel(qn_ref, qr_ref, k_ref, vt_ref, g_ref, out_ref, qt_ref, acc_ref, l_ref):
    tq, tk, qs = MLA_Q_TILE, MLA_K_TILE, MLA_Q_SUB
    n_sub = tq // qs
    n_chain = 2 * n_sub
    s_len = k_ref.shape[2]
    lo_q = _lo_mask((qs, PAIR))
    rope_head = lax.broadcasted_iota(jnp.int32, (qs, LANES), 1) // ROPE_DIM
    even_head = (pl.program_id(1) % 2) * 2

    def scores(c, kb):
        return jnp.dot(kb, qt_ref[c], preferred_element_type=F32)

    def causal(st, key_offset):
        key = lax.broadcasted_iota(jnp.int32, st.shape, 0) + key_offset
        qry = lax.broadcasted_iota(jnp.int32, st.shape, 1)
        return jnp.where(key <= qry, st, NEG)

    def run_jobs(jobs, ls, assign):
        ahead = 4
        sts = [scores(c, kb) for c, kb, _, _ in jobs[:ahead]]
        for n, (c, kb, vtb, key_offset) in enumerate(jobs):
            if n + ahead < len(jobs):
                nc, nkb, _, _ = jobs[n + ahead]
                sts.append(scores(nc, nkb))
            st = sts[n] if key_offset is None else causal(sts[n], key_offset)
            p = jnp.exp2(st)
            col_sum = jnp.sum(p, axis=0, keepdims=True)
            pv = jnp.dot(vtb[c % 2], p.astype(BF16), preferred_element_type=F32)
            if c in assign:
                assign.remove(c)
                ls[c] = col_sum
                acc_ref[c] = pv
            else:
                ls[c] = ls[c] + col_sum
                acc_ref[c] += pv
        return ls

    def q_tile(qi, carry):
        row0 = qi * tq
        for h in range(n_sub):
            r = pl.multiple_of(row0 + h * qs, qs)
            qn = qn_ref[0, 0, pl.ds(r, qs), :].astype(F32)
            qr = qr_ref[0, pl.ds(r, qs), :].astype(F32)
            q_even = jnp.concatenate(
                [jnp.where(lo_q, qn, 0.0), jnp.where(rope_head == even_head, qr, 0.0)], axis=1)
            q_odd = jnp.concatenate(
                [jnp.where(lo_q, 0.0, qn), jnp.where(rope_head == even_head + 1, qr, 0.0)], axis=1)
            qt_ref[2 * h] = q_even.T.astype(BF16)
            qt_ref[2 * h + 1] = q_odd.T.astype(BF16)

        def kv_block(start, size):
            start = pl.multiple_of(start, qs)
            vt = tuple(vt_ref[0, 0, hd * HEAD_DIM:(hd + 1) * HEAD_DIM, pl.ds(start, size)]
                       for hd in range(2))
            return k_ref[0, 0, pl.ds(start, size), :], vt

        jobs = []
        for h in range(n_sub):
            for k0 in range(0, (h + 1) * qs, tk):
                size = min(tk, (h + 1) * qs - k0)
                kb, vtb = kv_block(row0 + k0, size)
                offset = k0 - h * qs if k0 + size > h * qs else None
                jobs += [(2 * h, kb, vtb, offset), (2 * h + 1, kb, vtb, offset)]
        n_full = qi * (tq // tk)
        for t in range(n_full):
            kb, vtb = kv_block(t * tk, tk)
            jobs += [(c, kb, vtb, None) for c in range(n_chain)]
        ls = run_jobs(jobs, [None] * n_chain, set(range(n_chain)))
        for c in range(n_chain):
            l_ref[c] = jnp.broadcast_to(ls[c], (8, qs))
        den_lo = functools.reduce(jnp.minimum, [jnp.min(l) for l in ls])
        den_hi = functools.reduce(jnp.maximum, [jnp.max(l) for l in ls])
        num_hi = jnp.max(jnp.abs(acc_ref[...]))
        unsafe = jnp.logical_not((den_lo >= SAFE_MIN) & (den_hi <= SAFE_MAX) & (num_hi <= SAFE_MAX))

        @pl.when(unsafe)
        def _():
            def chain(c, carry_c):
                q_start = (c // 2) * qs

                def kv_step(j, st_in):
                    m, l = st_in
                    col0 = pl.multiple_of(j * tk, tk)
                    kb = k_ref[0, 0, pl.ds(col0, tk), :]
                    vtb = vt_ref[0, 0, pl.ds(pl.multiple_of((c % 2) * HEAD_DIM, HEAD_DIM), HEAD_DIM),
                                 pl.ds(col0, tk)]
                    st = causal(scores(c, kb), j * tk - row0 - q_start)
                    m_new = jnp.maximum(m, jnp.max(st, axis=0, keepdims=True))
                    alpha = jnp.exp2(m - m_new)
                    p = jnp.exp2(st - m_new)
                    pv = jnp.dot(vtb, p.astype(BF16), preferred_element_type=F32)
                    acc_ref[c] = alpha * acc_ref[c] + pv
                    return m_new, alpha * l + jnp.sum(p, axis=0, keepdims=True)

                acc_ref[c] = jnp.zeros((HEAD_DIM, qs), F32)
                init = (jnp.full((1, qs), NEG, F32), jnp.zeros((1, qs), F32))
                _, l = lax.fori_loop(0, n_full + tq // tk, kv_step, init)
                l_ref[c] = jnp.broadcast_to(l, (8, qs))
                return carry_c

            lax.fori_loop(0, n_chain, chain, 0)

        for h in range(n_sub):
            ot = jnp.concatenate([acc_ref[2 * h] / l_ref[2 * h, 0:1, :],
                                  acc_ref[2 * h + 1] / l_ref[2 * h + 1, 0:1, :]], axis=0)
            r = pl.multiple_of(row0 + h * qs, qs)
            gate = g_ref[0, pl.ds(r, qs), :].astype(F32)
            out_ref[0, pl.ds(r, qs), :] = (ot.T * gate).astype(BF16)
        return carry

    for qi in range(s_len // tq):
        q_tile(qi, 0)


def _latent_attention(qn, qr, kcat, vt, gb):
    b, n_pair, s, _ = qn.shape
    n_chain = 2 * (MLA_Q_TILE // MLA_Q_SUB)
    pair_spec = lambda w: pl.BlockSpec((1, 1, s, w), lambda i, p: (i, p, 0, 0))
    vt_spec = pl.BlockSpec((1, 1, PAIR, s), lambda i, p: (i, p, 0, 0))
    lane_spec = pl.BlockSpec((1, s, PAIR), lambda i, p: (i, 0, p))
    return pl.pallas_call(
        _mla_kernel,
        grid=(b, n_pair),
        in_specs=[pair_spec(PAIR), pl.BlockSpec((1, s, LANES), lambda i, p: (i, 0, p // 2)),
                  pair_spec(2 * LANES), vt_spec, lane_spec],
        out_specs=lane_spec,
        out_shape=jax.ShapeDtypeStruct((b, s, n_pair * PAIR), BF16),
        scratch_shapes=[pltpu.VMEM((n_chain, 2 * LANES, MLA_Q_SUB), BF16),
                        pltpu.VMEM((n_chain, HEAD_DIM, MLA_Q_SUB), F32),
                        pltpu.VMEM((n_chain, 8, MLA_Q_SUB), F32)],
        compiler_params=pltpu.CompilerParams(
            dimension_semantics=("arbitrary", "arbitrary"), vmem_limit_bytes=VMEM_LIMIT),
        name="latent_attention",
    )(qn, qr, kcat, vt, gb)


def _out_kernel(ma_ref, mb_ref, x_ref, w_ref, out_ref):
    a = ma_ref.shape[2]
    acc = jnp.dot(ma_ref[0], w_ref[0:a], preferred_element_type=F32)
    acc = acc + jnp.dot(mb_ref[0], w_ref[a:], preferred_element_type=F32)
    out_ref[0] = x_ref[0] + acc


def _output_projection(ma, mb, x, w):
    b, s, d = x.shape
    tm = OUT_ROW_TILE
    row_spec = lambda w: pl.BlockSpec((1, tm, w), lambda i, j: (i, j, 0))
    full = lambda a: pl.BlockSpec(a.shape, lambda i, j: (0,) * a.ndim)
    return pl.pallas_call(
        _out_kernel,
        grid=(b, s // tm),
        in_specs=[row_spec(ma.shape[-1]), row_spec(mb.shape[-1]), row_spec(d), full(w)],
        out_specs=row_spec(d),
        out_shape=jax.ShapeDtypeStruct((b, s, d), x.dtype),
        compiler_params=pltpu.CompilerParams(
            dimension_semantics=("arbitrary", "arbitrary"), vmem_limit_bytes=VMEM_LIMIT),
        name="output_projection",
    )(ma, mb, x, w)


def _t5_bucket(dist):
    max_exact = REL_BUCKETS // 2
    d = np.maximum(dist.astype(np.float32), np.float32(1.0))
    large = max_exact + (np.log(d / np.float32(max_exact)) / np.float32(
        math.log(REL_MAX_DISTANCE / max_exact)) * np.float32(REL_BUCKETS - max_exact)
                         ).astype(np.int32)
    large = np.minimum(large, REL_BUCKETS - 1)
    return np.where(dist < max_exact, dist, large)


def _dilated_bias(rel_bias, n_heads):
    onehot, keep = [], []
    for dil in DILATIONS:
        n_chunk = MAX_DIL // dil
        csize = BLOCK // n_chunk
        idx = np.arange(BLOCK)
        true_idx = n_chunk * (idx % csize) + idx // csize
        qi = true_idx[:, None]
        ki = np.concatenate([true_idx, true_idx + BLOCK])[None, :]
        j = qi + BLOCK - ki
        valid = (j >= 0) & (j <= BLOCK)
        bucket = _t5_bucket(np.maximum(j, 0) * dil)
        onehot.append(np.arange(REL_BUCKETS)[:, None, None] == bucket)
        keep.append(np.stack([valid, valid & (ki >= BLOCK)]))
    onehot, keep = np.stack(onehot, axis=1), np.stack(keep)
    table = rel_bias.astype(F32)[:, None, :, None, None]
    bias = jnp.sum(jnp.where(onehot[:, :, None], table, 0.0), axis=0) * LOG2E
    tiles = jnp.where(keep[:, :, None], bias[:, None], NEG)
    return tiles.reshape(len(DILATIONS), 2, n_heads // 2, 2 * BLOCK, 2 * BLOCK)


def _layer(x, pos_row, rel_bias, norm_gain, w_in, a_q_gain, a_k_gain, q_c_gain, w_uq, kv_c_gain,
           w_ukv, qn_gain, qr_gain, kn_gain, kr_gain, w_out):
    d = x.shape[-1]
    n_heads = 8
    a_width = n_heads * HEAD_DIM
    row = lambda v: v.reshape(1, -1).astype(F32)

    splits = (a_width, a_width, a_width, a_width, Q_RANK, KV_RANK, ROPE_DIM, a_width)
    starts = np.concatenate([[0], np.cumsum(splits)])
    kr0, kr1 = int(starts[6]), int(starts[7])
    rope_rep = LANES // ROPE_DIM
    w_in_p = jnp.concatenate(
        [w_in[:, :kr0], jnp.tile(w_in[:, kr0:kr1], (1, rope_rep)), w_in[:, kr1:]],
        axis=1).astype(BF16)
    off = dict(qa=int(starts[0]), ka=int(starts[1]), va=int(starts[2]), za=int(starts[3]),
               cq=int(starts[4]), ckv=int(starts[5]), kr=kr0, zb=kr0 + LANES)

    qk_dim = HEAD_DIM + ROPE_DIM
    wq = w_uq.reshape(Q_RANK, n_heads, qk_dim)
    wuq_p = jnp.concatenate([wq[:, :, :HEAD_DIM].reshape(Q_RANK, -1),
                             wq[:, :, HEAD_DIM:].reshape(Q_RANK, -1)], axis=1).astype(BF16)
    wkv = w_ukv.reshape(KV_RANK, n_heads, 2 * HEAD_DIM)
    wukv_p = jnp.concatenate([wkv[:, :, :HEAD_DIM].reshape(KV_RANK, -1),
                              wkv[:, :, HEAD_DIM:].reshape(KV_RANK, -1)], axis=1).astype(BF16)

    scale_a = HEAD_DIM ** -0.5
    scale_b = qk_dim ** -0.5
    inv_freq = ROPE_THETA ** (-jnp.arange(0, ROPE_DIM, 2, dtype=F32) / ROPE_DIM)
    rope_row = lambda v: row(jnp.tile(v, rope_rep))
    half = ROPE_DIM // 2
    freq = inv_freq.reshape(half, 1)
    spread = (np.arange(LANES)[None, :] % half == np.arange(half)[:, None]).astype(np.float32)
    spread = jnp.asarray(spread, BF16)
    sgn = rope_row(jnp.concatenate([-jnp.ones(half, F32), jnp.ones(half, F32)]))
    head_norm, rope_norm = HEAD_DIM ** 0.5, ROPE_DIM ** 0.5

    qa, ka, va, ga, qn, qr, kcat, vt, gb = _input_projection(
        x, pos_row, row(norm_gain), w_in_p,
        row(jnp.tile(a_q_gain, 2) * (head_norm * scale_a * LOG2E)),
        row(jnp.tile(a_k_gain, 2) * head_norm),
        row(q_c_gain), wuq_p, row(kv_c_gain), wukv_p,
        row(jnp.tile(qn_gain, 2) * (head_norm * scale_b * LOG2E)),
        rope_row(qr_gain * (rope_norm * scale_b * LOG2E)),
        row(jnp.tile(kn_gain, 2) * head_norm), rope_row(kr_gain), freq, spread, sgn, off)

    mixed_a = _dilated_attention(qa, ka, va, _dilated_bias(rel_bias, n_heads), ga)
    mixed_b = _latent_attention(qn, qr, kcat, vt, gb)
    return _output_projection(mixed_a, mixed_b, x, w_out.astype(BF16))


def kernel(x, positions, rel_bias, norm_gain, w_in, a_q_gain, a_k_gain, q_c_gain, w_uq, kv_c_gain,
           w_ukv, qn_gain, qr_gain, kn_gain, kr_gain, w_out):
    pos_row = positions[:, None, :]
    for l in range(norm_gain.shape[0]):
        x = _layer(x, pos_row, rel_bias, norm_gain[l], w_in[l], a_q_gain[l], a_k_gain[l],
                   q_c_gain[l], w_uq[l], kv_c_gain[l], w_ukv[l], qn_gain[l], qr_gain[l],
                   kn_gain[l], kr_gain[l], w_out[l])
    return x
```
